```python
import math
import jax, jax.numpy as jnp
from jax import lax
import numpy as np

D_MODEL = 4096
BATCH = 2
SEQ = 8192
DEPTH = 1

GRID_W = 64
CTX_LEN = 256

DIFF_HEADS = 8
DIFF_QK_DIM = 128
DIFF_V_DIM = 2 * DIFF_QK_DIM
DIFF_QK_W = DIFF_HEADS * 2 * DIFF_QK_DIM
DIFF_WIDTH = DIFF_HEADS * DIFF_V_DIM

RET_HEADS = 8
RET_K_DIM = 128
RET_V_DIM = 256
RET_QK_W = RET_HEADS * RET_K_DIM
RET_WIDTH = RET_HEADS * RET_V_DIM
RET_CHUNK = 128
RET_DECAY_BASE = 5.0

D_MIX = DIFF_WIDTH + RET_WIDTH

O_DQ = 0
O_DK = O_DQ + DIFF_QK_W
O_DV = O_DK + DIFF_QK_W
O_DG = O_DV + DIFF_WIDTH
O_RQ = O_DG + DIFF_WIDTH
O_RK = O_RQ + RET_QK_W
O_RV = O_RK + RET_QK_W
O_RG = O_RV + RET_WIDTH
D_IN_PROJ = O_RG + RET_WIDTH

ROPE_DIM = 128
ROPE_BASE = 10000.0
Q_BLOCK = 128
EPS = 1e-6

kernel_name = "hybrid_diffattn_retention_dit_layer"


def rms_norm(x, w):
    xf = x.astype(jnp.float32)
    y = xf * lax.rsqrt(jnp.mean(xf * xf, axis=-1, keepdims=True) + EPS)
    return (y * w.astype(jnp.float32)).astype(x.dtype)


def axial_rope_tables(n_tokens):
    rows = n_tokens // GRID_W
    row, col = jnp.meshgrid(jnp.arange(rows), jnp.arange(GRID_W), indexing="ij")
    row = row.reshape(-1).astype(jnp.float32)
    col = col.reshape(-1).astype(jnp.float32)
    half = ROPE_DIM // 2
    inv_freq = ROPE_BASE ** (-jnp.arange(0, half, 2, dtype=jnp.float32) / half)
    ang_r = row[:, None] * inv_freq
    ang_c = col[:, None] * inv_freq
    ang = jnp.concatenate([ang_r, ang_r, ang_c, ang_c], axis=-1)
    return jnp.cos(ang), jnp.sin(ang)


def apply_axial_rope(x, cos, sin):
    a, b, c_, d = jnp.split(x, 4, axis=-1)
    rot = jnp.concatenate([-b, a, -d, c_], axis=-1)
    cos = cos[None, :, None, :].astype(x.dtype)
    sin = sin[None, :, None, :].astype(x.dtype)
    return x * cos + rot * sin


def diff_attention(q, k, v, lam):
    B, L, H2, d = q.shape
    H = H2 // 2
    Lk = k.shape[1]
    nb = L // Q_BLOCK
    scale = d ** -0.5
    qb = q.reshape(B, nb, Q_BLOCK, H, 2, d).transpose(1, 0, 3, 4, 2, 5)
    kt = k.reshape(B, Lk, H, 2, d).transpose(0, 2, 3, 1, 4)
    vt = v.transpose(0, 2, 1, 3)

    def block(q_blk):
        s = jnp.einsum("bhiqd,bhikd->bhiqk", q_blk, kt).astype(jnp.float32) * scale
        p = jax.nn.softmax(s, axis=-1)
        a = p[:, :, 0] - lam * p[:, :, 1]
        return jnp.einsum("bhqk,bhkv->bqhv", a.astype(vt.dtype), vt)

    o = lax.map(block, qb)
    return o.transpose(1, 0, 2, 3, 4).reshape(B, L, H, v.shape[-1])


def log_decay(p):
    return jnp.log1p(-jnp.exp2(-p.astype(jnp.float32)))


def retention_chunked(q, k, v, log_gamma, r0, include_diag):
    B, H, L, dk = q.shape
    dv = v.shape[-1]
    C = RET_CHUNK
    nc = L // C
    qc = q.reshape(B, H, nc, C, dk)
    kc = k.reshape(B, H, nc, C, dk)
    vc = v.reshape(B, H, nc, C, dv)
    n = jnp.arange(C, dtype=jnp.float32)
    dist = n[:, None] - n[None, :]
    mask = dist >= 0 if include_diag else dist > 0
    decay = jnp.where(mask[None], jnp.exp(log_gamma[:, None, None] * jnp.where(mask, dist, 0.0)[None]), 0.0)
    scores = jnp.einsum("bhnqd,bhnkd->bhnqk", qc, kc) * decay[None, :, None]
    o_intra = jnp.einsum("bhnqk,bhnkv->bhnqv", scores, vc)
    xi = jnp.exp(log_gamma[:, None] * (n + 1.0))
    zeta = jnp.exp(log_gamma[:, None] * (C - 1.0 - n))
    chunk_kv = jnp.einsum("bhnkd,bhnkv->bhndv", kc * zeta[None, :, None, :, None], vc)
    gamma_c = jnp.exp(log_gamma * C)[None, :, None, None]

    def step(r, kv):
        return gamma_c * r + kv, r

    _, r_prev = lax.scan(step, r0, jnp.moveaxis(chunk_kv, 2, 0))
    r_prev = jnp.moveaxis(r_prev, 0, 2)
    o_cross = jnp.einsum("bhnqd,bhndv->bhnqv", qc, r_prev) * xi[None, :, None, :, None]
    return (o_intra + o_cross).reshape(B, H, L, dv)


def hybrid_layer(x, ctx, c, c_ctx, cos, sin, lam_init, norm_w, ada_w, ada_b, w_in,
                 q_norm_w, k_norm_w, lq1, lk1, lq2, lk2, subln_w,
                 dec_f, dec_b, ret_norm_w, w_out):
    B, L, D = x.shape
    Lc = ctx.shape[1]
    f32 = jnp.float32

    mod = jax.nn.silu(c) @ ada_w + ada_b
    shift, scale, gate = jnp.split(mod, 3, axis=-1)
    mod_c = jax.nn.silu(c_ctx) @ ada_w[:, :2 * D] + ada_b[:2 * D]
    shift_c, scale_c = jnp.split(mod_c, 2, axis=-1)
    h = rms_norm(x, norm_w) * (1.0 + scale[:, None]) + shift[:, None]
    hc = rms_norm(ctx, norm_w) * (1.0 + scale_c) + shift_c

    proj = h @ w_in
    dq, dk, dv, dg, rq, rk, rv, rg = jnp.split(proj, [O_DK, O_DV, O_DG, O_RQ, O_RK, O_RV, O_RG], axis=-1)
    w_ctx = jnp.concatenate([w_in[:, O_DK:O_DG], w_in[:, O_RK:O_RG]], axis=-1)
    pc = hc @ w_ctx
    c_dk, c_dv, c_rk, c_rv = jnp.split(pc, [DIFF_QK_W, DIFF_QK_W + DIFF_WIDTH, DIFF_QK_W + DIFF_WIDTH + RET_QK_W], axis=-1)

    dq = apply_axial_rope(rms_norm(dq.reshape(B, L, 2 * DIFF_HEADS, DIFF_QK_DIM), q_norm_w), cos, sin)
    dk = apply_axial_rope(rms_norm(dk.reshape(B, L, 2 * DIFF_HEADS, DIFF_QK_DIM), k_norm_w), cos, sin)
    c_dk = rms_norm(c_dk.reshape(B, Lc, 2 * DIFF_HEADS, DIFF_QK_DIM), k_norm_w)
    k_all = jnp.concatenate([dk, c_dk], axis=1)
    v_all = jnp.concatenate([dv.reshape(B, L, DIFF_HEADS, DIFF_V_DIM),
                             c_dv.reshape(B, Lc, DIFF_HEADS, DIFF_V_DIM)], axis=1)
    lam = (jnp.exp(jnp.sum(lq1.astype(f32) * lk1.astype(f32)))
           - jnp.exp(jnp.sum(lq2.astype(f32) * lk2.astype(f32))) + lam_init)
    o_d = diff_attention(dq, k_all, v_all, lam)
    o_d = rms_norm(o_d, subln_w) * (1.0 - lam_init)
    o_d = o_d.reshape(B, L, DIFF_WIDTH) * jax.nn.silu(dg)

    k_scale = RET_K_DIM ** -0.5
    rq = apply_axial_rope(rq.reshape(B, L, RET_HEADS, RET_K_DIM), cos, sin)
    rk = apply_axial_rope(rk.reshape(B, L, RET_HEADS, RET_K_DIM), cos, sin) * k_scale
    c_rk = c_rk.reshape(B, Lc, RET_HEADS, RET_K_DIM) * k_scale
    rv = rv.reshape(B, L, RET_HEADS, RET_V_DIM)
    c_rv = c_rv.reshape(B, Lc, RET_HEADS, RET_V_DIM)
    to_bhld = lambda t: t.astype(f32).transpose(0, 2, 1, 3)
    q_, k_, v_ = to_bhld(rq), to_bhld(rk), to_bhld(rv)
    kc_, vc_ = to_bhld(c_rk), to_bhld(c_rv)
    lg_f = log_decay(dec_f)
    lg_b = log_decay(dec_b)
    m = jnp.arange(Lc, dtype=f32)
    r_ctx_f = jnp.einsum("bhmd,bhmv->bhdv", kc_ * jnp.exp(lg_f[:, None] * (Lc - 1.0 - m))[None, :, :, None], vc_)
    r_ctx_b = jnp.einsum("bhmd,bhmv->bhdv", kc_ * jnp.exp(lg_b[:, None] * m)[None, :, :, None], vc_)
    o_f = retention_chunked(q_, k_, v_, lg_f, r_ctx_f, True)
    o_b = retention_chunked(q_[:, :, ::-1], k_[:, :, ::-1], v_[:, :, ::-1], lg_b, r_ctx_b, False)[:, :, ::-1]
    o_r = (o_f + o_b).transpose(0, 2, 1, 3)
    o_r = rms_norm(o_r, ret_norm_w).astype(x.dtype).reshape(B, L, RET_WIDTH) * jax.nn.silu(rg)

    y = jnp.concatenate([o_d, o_r], axis=-1) @ w_out
    return x + gate[:, None] * y


def setup_inputs(seed: int = 0) -> dict:
    key = jax.random.key(seed)
    ks = jax.random.split(key, 20)
    f32 = jnp.float32
    D = D_MODEL

    def nrm(k, shape, s):
        return jax.random.normal(k, shape, f32) * s

    decay_sched = RET_DECAY_BASE + jnp.arange(RET_HEADS, dtype=f32)
    return {
        "x": nrm(ks[0], (BATCH, SEQ, D), 1.0),
        "c": nrm(ks[1], (BATCH, D), 1.0),
        "ctx": nrm(ks[2], (BATCH, CTX_LEN, D), 1.0),
        "c_ctx": nrm(ks[3], (D,), 1.0),
        "norm_w": 1.0 + nrm(ks[4], (DEPTH, D), 0.02),
        "ada_w": nrm(ks[5], (DEPTH, D, 3 * D), D ** -0.5),
        "ada_b": nrm(ks[6], (DEPTH, 3 * D), 0.02),
        "w_in": nrm(ks[7], (DEPTH, D, D_IN_PROJ), D ** -0.5),
        "diff_q_norm_w": 1.0 + nrm(ks[8], (DEPTH, DIFF_QK_DIM), 0.02),
        "diff_k_norm_w": 1.0 + nrm(ks[9], (DEPTH, DIFF_QK_DIM), 0.02),
        "diff_lambda_q1": nrm(ks[10], (DEPTH, DIFF_QK_DIM), 0.1),
        "diff_lambda_k1": nrm(ks[11], (DEPTH, DIFF_QK_DIM), 0.1),
        "diff_lambda_q2": nrm(ks[12], (DEPTH, DIFF_QK_DIM), 0.1),
        "diff_lambda_k2": nrm(ks[13], (DEPTH, DIFF_QK_DIM), 0.1),
        "diff_subln_w": 1.0 + nrm(ks[14], (DEPTH, DIFF_V_DIM), 0.02),
        "ret_decay_fwd": decay_sched[None] + nrm(ks[15], (DEPTH, RET_HEADS), 0.1),
        "ret_decay_bwd": decay_sched[None] + nrm(ks[16], (DEPTH, RET_HEADS), 0.1),
        "ret_norm_w": 1.0 + nrm(ks[17], (DEPTH, RET_V_DIM), 0.02),
        "w_out": nrm(ks[18], (DEPTH, D_MIX, D), D_MIX ** -0.5),
    }


def reference(x, c, ctx, c_ctx, norm_w, ada_w, ada_b, w_in, diff_q_norm_w, diff_k_norm_w,
              diff_lambda_q1, diff_lambda_k1, diff_lambda_q2, diff_lambda_k2, diff_subln_w,
              ret_decay_fwd, ret_decay_bwd, ret_norm_w, w_out):
    L = x.shape[1]
    cos, sin = axial_rope_tables(L)
    for layer in range(DEPTH):
        lam_init = 0.8 - 0.6 * math.exp(-0.3 * layer)
        x = hybrid_layer(x, ctx, c, c_ctx, cos, sin, lam_init,
                         norm_w[layer], ada_w[layer], ada_b[layer], w_in[layer],
                         diff_q_norm_w[layer], diff_k_norm_w[layer],
                         diff_lambda_q1[layer], diff_lambda_k1[layer],
                         diff_lambda_q2[layer], diff_lambda_k2[layer], diff_subln_w[layer],
                         ret_decay_fwd[layer], ret_decay_bwd[layer], ret_norm_w[layer], w_out[layer])
    return x
```

```python
import functools
import math

import jax
import jax.numpy as jnp
from jax import lax
from jax.experimental import pallas as pl
from jax.experimental.pallas import tpu as pltpu

F32 = jnp.float32
BF16 = jnp.bfloat16

GRID_W = 64
DIFF_HEADS = 8
QK_DIM = 128
V_DIM = 256
RET_HEADS = 8
ROPE_BASE = 10000.0
EPS = 1e-6
LAM_INIT = 0.8 - 0.6 * math.exp(-0.3 * 0)

DIFF_QK_W = DIFF_HEADS * 2 * QK_DIM
DIFF_WIDTH = DIFF_HEADS * V_DIM
RET_QK_W = RET_HEADS * QK_DIM
RET_WIDTH = RET_HEADS * V_DIM
O_DQ = 0
O_DK = O_DQ + DIFF_QK_W
O_DV = O_DK + DIFF_QK_W
O_DG = O_DV + DIFF_WIDTH
O_RQ = O_DG + DIFF_WIDTH
O_RK = O_RQ + RET_QK_W
O_RV = O_RK + RET_QK_W
O_RG = O_RV + RET_WIDTH

V7X_VMEM_LIMIT_BYTES = 56 * 1024 * 1024
ARB = "arbitrary"


def _pick(n, candidates):
    for c in candidates:
        if n % c == 0:
            return c
    raise ValueError(f"no tile in {candidates} divides {n}")


def _params(*sem):
    return pltpu.CompilerParams(dimension_semantics=sem, vmem_limit_bytes=V7X_VMEM_LIMIT_BYTES)


def _ada_kernel(c_ref, w_ref, b_ref, o_ref):
    c = c_ref[...]
    a = (c * jax.nn.sigmoid(c)).astype(BF16)
    o_ref[...] = jnp.dot(a, w_ref[...].astype(BF16), preferred_element_type=F32) + b_ref[...]


def _ada_call(cc, ada_w, ada_b):
    rows, d = cc.shape
    n = ada_w.shape[1]
    tn = _pick(n, (512, 256, 128))
    return pl.pallas_call(
        _ada_kernel,
        grid=(n // tn,),
        in_specs=[pl.BlockSpec((rows, d), lambda j: (0, 0)),
                  pl.BlockSpec((d, tn), lambda j: (0, j)),
                  pl.BlockSpec((1, tn), lambda j: (0, j))],
        out_specs=pl.BlockSpec((rows, tn), lambda j: (0, j)),
        out_shape=jax.ShapeDtypeStruct((rows, n), F32),
        compiler_params=_params(ARB),
        name="ada_mod",
    )(cc, ada_w, ada_b.reshape(1, n))


def _norm_kernel(x_ref, ctx_ref, nw_ref, mod_ref, o_ref, *, n_lat):
    i = pl.program_id(1)

    def norm_mod(xv, scale, shift):
        ms = jnp.mean(xv * xv, axis=-1, keepdims=True)
        y = xv * lax.rsqrt(ms + EPS) * nw_ref[...]
        return (y * (1.0 + scale) + shift).astype(o_ref.dtype)

    @pl.when(i < n_lat)
    def _():
        o_ref[...] = norm_mod(x_ref[...], mod_ref[0:1, :], mod_ref[1:2, :])

    @pl.when(i >= n_lat)
    def _():
        o_ref[...] = norm_mod(ctx_ref[...], mod_ref[2:3, :], mod_ref[3:4, :])


def _norm_call(x, ctx, norm_w, mods):
    b, l, d = x.shape
    lc = ctx.shape[1]
    tr = _pick(math.gcd(l, lc), (256, 128, 64, 32, 16))
    n_lat, n_ctx = l // tr, lc // tr
    return pl.pallas_call(
        functools.partial(_norm_kernel, n_lat=n_lat),
        grid=(b, n_lat + n_ctx),
        in_specs=[pl.BlockSpec((None, tr, d), lambda bb, i: (bb, jnp.minimum(i, n_lat - 1), 0)),
                  pl.BlockSpec((None, tr, d), lambda bb, i: (bb, jnp.maximum(i - n_lat, 0), 0)),
                  pl.BlockSpec((1, d), lambda bb, i: (0, 0)),
                  pl.BlockSpec((None, 4, d), lambda bb, i: (bb, 0, 0))],
        out_specs=pl.BlockSpec((None, tr, d), lambda bb, i: (bb, i, 0)),
        out_shape=jax.ShapeDtypeStruct((b, l + lc, d), BF16),
        compiler_params=_params(ARB, ARB),
        name="norm_mod",
    )(x, ctx, norm_w.reshape(1, d), mods)


def _rope(xh, cos, sin_lo, sin_hi):
    return xh * cos + pltpu.roll(xh, 96, 1) * sin_lo + pltpu.roll(xh, 32, 1) * sin_hi


def _inproj_kernel(cb_ref, a_ref, w_ref, *rest, mode, out_w):
    del cb_ref
    acc = jnp.dot(a_ref[...], w_ref[...], preferred_element_type=F32)
    tn = acc.shape[1]
    if mode == "plain":
        (o_ref,) = rest
        for c in range(tn // out_w):
            o_ref[c] = acc[:, c * out_w:(c + 1) * out_w].astype(o_ref.dtype)
        return
    cos_ref, slo_ref, shi_ref, vec_ref, o_ref = rest
    cos, slo, shi, vec = cos_ref[...], slo_ref[...], shi_ref[...], vec_ref[...]
    per = out_w // QK_DIM
    for t in range(tn // QK_DIM):
        xh = acc[:, t * QK_DIM:(t + 1) * QK_DIM]
        if mode == "normrope":
            xh = xh * lax.rsqrt(jnp.mean(xh * xh, axis=-1, keepdims=True) + EPS) * vec
            r = _rope(xh, cos, slo, shi)
        else:
            r = _rope(xh, cos, slo, shi) * vec
        lane0 = (t % per) * QK_DIM
        o_ref[t // per, :, lane0:lane0 + QK_DIM] = r.astype(o_ref.dtype)


def _inproj_call(a2d, w, tables, vecs, *, mode, col_blocks, out_w, rows_per_batch):
    m, d = a2d.shape
    tn = 512
    tm = _pick(rows_per_batch, (768, 512, 384, 256, 128))
    n_j = len(col_blocks)
    n_out = n_j * tn // out_w
    col_tab = jnp.asarray(col_blocks, jnp.int32)
    t_blocks = rows_per_batch // tm
    grid = (m // tm, n_j)
    in_specs = [pl.BlockSpec((tm, d), lambda i, j, cb: (i, 0)),
                pl.BlockSpec((d, tn), lambda i, j, cb: (0, cb[j]))]
    args = [a2d, w]
    if mode != "plain":
        n_vec = vecs.shape[0]
        j_per_vec = n_j // n_vec
        tab_spec = pl.BlockSpec((tm, QK_DIM), lambda i, j, cb: (i % t_blocks, 0))
        in_specs += [tab_spec, tab_spec, tab_spec,
                     pl.BlockSpec((None, 1, QK_DIM), lambda i, j, cb: (j // j_per_vec, 0, 0))]
        args += [*tables, vecs]
    return pl.pallas_call(
        functools.partial(_inproj_kernel, mode=mode, out_w=out_w),
        grid_spec=pltpu.PrefetchScalarGridSpec(
            num_scalar_prefetch=1,
            grid=grid,
            in_specs=in_specs,
            out_specs=pl.BlockSpec((tn // out_w, tm, out_w), lambda i, j, cb: (j, i, 0)),
        ),
        out_shape=jax.ShapeDtypeStruct((n_out, m, out_w), BF16),
        compiler_params=_params(ARB, ARB),
        name="inproj_" + mode,
    )(col_tab, *args)


def _attn_kernel(lamp_ref, q_ref, k_ref, v_ref, g_ref, sw_ref, o_ref, acc1, acc2, *, tk):
    t_all = k_ref.shape[0]
    tq = q_ref.shape[0]
    c_exp = (QK_DIM ** -0.5) * math.log2(math.e)
    q = q_ref[...]
    qs = (q[:, :QK_DIM], q[:, QK_DIM:])
    accs = (acc1, acc2)
    acc1[...] = jnp.zeros_like(acc1)
    acc2[...] = jnp.zeros_like(acc2)

    def body(c, carry):
        start = pl.multiple_of(c * tk, tk)
        kc = k_ref[pl.ds(start, tk), :]
        vc = v_ref[pl.ds(start, tk), :]
        new = []
        for sub in range(2):
            m, l = carry[2 * sub], carry[2 * sub + 1]
            ks = kc[:, sub * QK_DIM:(sub + 1) * QK_DIM]
            s = lax.dot_general(qs[sub], ks, (((1,), (1,)), ((), ())), preferred_element_type=F32)
            m_new = jnp.maximum(m, jnp.max(s, axis=1, keepdims=True))
            alpha = jnp.exp2((m - m_new) * c_exp)
            p = jnp.exp2((s - m_new) * c_exp)
            l_new = alpha * l + jnp.sum(p, axis=1, keepdims=True)
            accs[sub][...] = alpha * accs[sub][...] + jnp.dot(p.astype(BF16), vc, preferred_element_type=F32)
            new += [m_new, l_new]
        return tuple(new)

    neg = jnp.full((tq, 1), -jnp.inf, F32)
    zero = jnp.zeros((tq, 1), F32)
    m1, l1, m2, l2 = lax.fori_loop(0, t_all // tk, body, (neg, zero, neg, zero))

    lp = lamp_ref[...]
    lam = (jnp.exp(jnp.sum(lp[0:1] * lp[1:2], axis=1, keepdims=True))
           - jnp.exp(jnp.sum(lp[2:3] * lp[3:4], axis=1, keepdims=True)) + LAM_INIT)
    o = acc1[...] / l1 - lam * (acc2[...] / l2)
    y = o * lax.rsqrt(jnp.mean(o * o, axis=-1, keepdims=True) + EPS) * sw_ref[...] * (1.0 - LAM_INIT)
    g = g_ref[...].astype(F32)
    o_ref[...] = (y * (g * jax.nn.sigmoid(g))).astype(o_ref.dtype)


def _attn_call(p_nr, p_plain, lam_params, subln_w, l):
    _, b, t_all, _ = p_nr.shape
    tq = _pick(l, (256, 128))
    tk = _pick(t_all, (768, 512, 384, 256, 128))
    h = DIFF_HEADS
    return pl.pallas_call(
        functools.partial(_attn_kernel, tk=tk),
        grid=(b, h, l // tq),
        in_specs=[pl.BlockSpec((4, QK_DIM), lambda bb, hh, i: (0, 0)),
                  pl.BlockSpec((None, None, tq, V_DIM), lambda bb, hh, i: (hh, bb, i, 0)),
                  pl.BlockSpec((None, None, t_all, V_DIM), lambda bb, hh, i: (h + hh, bb, 0, 0)),
                  pl.BlockSpec((None, None, t_all, V_DIM), lambda bb, hh, i: (hh, bb, 0, 0)),
                  pl.BlockSpec((None, None, tq, V_DIM), lambda bb, hh, i: (h + hh, bb, i, 0)),
                  pl.BlockSpec((1, V_DIM), lambda bb, hh, i: (0, 0))],
        out_specs=pl.BlockSpec((None, tq, V_DIM), lambda bb, hh, i: (bb, i, hh)),
        out_shape=jax.ShapeDtypeStruct((b, l, DIFF_WIDTH), BF16),
        scratch_shapes=[pltpu.VMEM((tq, V_DIM), F32), pltpu.VMEM((tq, V_DIM), F32)],
        compiler_params=_params(ARB, ARB, ARB),
        name="diff_attn",
    )(lam_params, p_nr, p_nr, p_plain, p_plain, subln_w.reshape(1, V_DIM))


def _ret_kernel(dec_ref, q_ref, k_ref, v_ref, g_ref, nw_ref, o_ref, ob_sc, *, chunk):
    l = q_ref.shape[0]
    lc = k_ref.shape[0] - l
    nc = l // chunk
    cdim = (((1,), (1,)), ((), ()))
    rdim = (((0,), (0,)), ((), ()))

    dec = dec_ref[...]
    lg = jnp.log1p(-jnp.exp2(-dec))
    lg_f, lg_b = lg[0:1, 0:1], lg[1:2, 0:1]

    def col_iota(n):
        return lax.broadcasted_iota(jnp.int32, (n, 1), 0).astype(F32)

    ic = col_iota(chunk)
    xi_f = jnp.exp(lg_f * (ic + 1.0))
    xi_b = jnp.exp(lg_b * (chunk - ic))
    zeta_f = jnp.exp(lg_f * (chunk - 1.0 - ic))
    zeta_b = jnp.exp(lg_b * ic)
    gc_f = jnp.exp(lg_f * chunk)
    gc_b = jnp.exp(lg_b * chunk)
    ri = lax.broadcasted_iota(jnp.int32, (chunk, chunk), 0)
    ci = lax.broadcasted_iota(jnp.int32, (chunk, chunk), 1)
    dist = (ri - ci).astype(F32)
    dmat = jnp.where(ri >= ci, jnp.exp(lg_f * jnp.maximum(dist, 0.0)), jnp.exp(lg_b * jnp.maximum(-dist, 0.0)))

    im = col_iota(lc)
    kctx = k_ref[pl.ds(l, lc), :].astype(F32)
    vctx = v_ref[pl.ds(l, lc), :]
    r0_f = lax.dot_general((kctx * jnp.exp(lg_f * (lc - 1.0 - im))).astype(BF16), vctx, rdim,
                           preferred_element_type=F32)
    r0_b = lax.dot_general((kctx * jnp.exp(lg_b * im)).astype(BF16), vctx, rdim, preferred_element_type=F32)

    def chunk_kv(rows, zeta):
        kz = (k_ref[rows, :].astype(F32) * zeta).astype(BF16)
        return lax.dot_general(kz, v_ref[rows, :], rdim, preferred_element_type=F32)

    def bwd_body(j, s_b):
        c = nc - 1 - j
        rows = pl.ds(pl.multiple_of(c * chunk, chunk), chunk)
        cross = jnp.dot(q_ref[rows, :], s_b.astype(BF16), preferred_element_type=F32)
        ob_sc[rows, :] = cross * xi_b
        return gc_b * s_b + chunk_kv(rows, zeta_b)

    lax.fori_loop(0, nc, bwd_body, r0_b)

    def fwd_body(c, s_f):
        rows = pl.ds(pl.multiple_of(c * chunk, chunk), chunk)
        qc = q_ref[rows, :]
        scores = lax.dot_general(qc, k_ref[rows, :], cdim, preferred_element_type=F32) * dmat
        o = jnp.dot(scores.astype(BF16), v_ref[rows, :], preferred_element_type=F32)
        o = o + jnp.dot(qc, s_f.astype(BF16), preferred_element_type=F32) * xi_f + ob_sc[rows, :]
        y = o * lax.rsqrt(jnp.mean(o * o, axis=-1, keepdims=True) + EPS) * nw_ref[...]
        g = g_ref[rows, :].astype(F32)
        o_ref[rows, :] = (y * (g * jax.nn.sigmoid(g))).astype(o_ref.dtype)
        return gc_f * s_f + chunk_kv(rows, zeta_f)

    lax.fori_loop(0, nc, fwd_body, r0_f)


def _ret_call(p_rope, p_plain, decs, ret_norm_w, l):
    _, b, t_all, _ = p_rope.shape
    h = RET_HEADS
    chunk = 128
    return pl.pallas_call(
        functools.partial(_ret_kernel, chunk=chunk),
        grid=(b, h),
        in_specs=[pl.BlockSpec((None, 2, QK_DIM), lambda bb, hh: (hh, 0, 0)),
                  pl.BlockSpec((None, None, l, QK_DIM), lambda bb, hh: (hh, bb, 0, 0)),
                  pl.BlockSpec((None, None, t_all, QK_DIM), lambda bb, hh: (h + hh, bb, 0, 0)),
                  pl.BlockSpec((None, None, t_all, V_DIM), lambda bb, hh: (2 * DIFF_HEADS + hh, bb, 0, 0)),
                  pl.BlockSpec((None, None, l, V_DIM), lambda bb, hh: (2 * DIFF_HEADS + h + hh, bb, 0, 0)),
                  pl.BlockSpec((1, V_DIM), lambda bb, hh: (0, 0))],
        out_specs=pl.BlockSpec((None, l, V_DIM), lambda bb, hh: (bb, 0, hh)),
        out_shape=jax.ShapeDtypeStruct((b, l, RET_WIDTH), BF16),
        scratch_shapes=[pltpu.VMEM((l, V_DIM), F32)],
        compiler_params=_params(ARB, ARB),
        name="retention",
    )(decs, p_rope, p_rope, p_plain, p_plain, ret_norm_w.reshape(1, V_DIM))


def _outproj_kernel(ad_ref, ar_ref, wd_ref, wr_ref, x_ref, gate_ref, o_ref):
    y = jnp.dot(ad_ref[...], wd_ref[...], preferred_element_type=F32)
    y = y + jnp.dot(ar_ref[...], wr_ref[...], preferred_element_type=F32)
    o_ref[...] = x_ref[...] + gate_ref[...] * y


def _outproj_call(o_d, o_r, w_out, x, gate):
    b, l, d = x.shape
    kd, kr = o_d.shape[-1], o_r.shape[-1]
    tm = _pick(l, (1024, 512, 256, 128))
    tn = _pick(d, (512, 256, 128))
    per_b = l // tm
    return pl.pallas_call(
        _outproj_kernel,
        grid=(b * per_b, d // tn),
        in_specs=[pl.BlockSpec((tm, kd), lambda i, j: (i, 0)),
                  pl.BlockSpec((tm, kr), lambda i, j: (i, 0)),
                  pl.BlockSpec((kd, tn), lambda i, j: (0, j)),
                  pl.BlockSpec((kr, tn), lambda i, j: (kd // kr, j)),
                  pl.BlockSpec((tm, tn), lambda i, j: (i, j)),
                  pl.BlockSpec((None, 1, tn), lambda i, j: (i // per_b, 0, j))],
        out_specs=pl.BlockSpec((tm, tn), lambda i, j: (i, j)),
        out_shape=jax.ShapeDtypeStruct((b * l, d), F32),
        compiler_params=_params(ARB, ARB),
        name="outproj",
    )(o_d.reshape(b * l, kd), o_r.reshape(b * l, kr), w_out, w_out, x.reshape(b * l, d),
      gate.reshape(b, 1, d)).reshape(b, l, d)


def _rope_tables(l, lc):
    row, col = jnp.meshgrid(jnp.arange(l // GRID_W), jnp.arange(GRID_W), indexing="ij")
    row = row.reshape(-1).astype(F32)
    col = col.reshape(-1).astype(F32)
    half = QK_DIM // 2
    inv_freq = ROPE_BASE ** (-jnp.arange(0, half, 2, dtype=F32) / half)
    ang_r = row[:, None] * inv_freq
    ang_c = col[:, None] * inv_freq
    ang = jnp.concatenate([ang_r, ang_r, ang_c, ang_c], axis=-1)
    cos, sin = jnp.cos(ang), jnp.sin(ang)
    lo = (jnp.arange(QK_DIM) % half) < (half // 2)
    cos = jnp.concatenate([cos, jnp.ones((lc, QK_DIM), F32)], axis=0)
    sin_lo = jnp.concatenate([jnp.where(lo, -sin, 0.0), jnp.zeros((lc, QK_DIM), F32)], axis=0)
    sin_hi = jnp.concatenate([jnp.where(lo, 0.0, sin), jnp.zeros((lc, QK_DIM), F32)], axis=0)
    return cos, sin_lo, sin_hi


def _layer(x, ctx, c, c_ctx, norm_w, ada_w, ada_b, w_in, q_norm_w, k_norm_w, lam_params, subln_w,
           dec_f, dec_b, ret_norm_w, w_out):
    b, l, d = x.shape
    lc = ctx.shape[1]
    t_all = l + lc

    rows = -(-(b + 1) // 8) * 8
    cc = jnp.zeros((rows, d), F32).at[:b].set(c).at[b].set(c_ctx)
    mod = _ada_call(cc, ada_w, ada_b)
    shift, scale, gate = mod[:b, :d], mod[:b, d:2 * d], mod[:b, 2 * d:]
    shift_c = jnp.broadcast_to(mod[b, :d], (b, d))
    scale_c = jnp.broadcast_to(mod[b, d:2 * d], (b, d))
    mods = jnp.stack([scale, shift, scale_c, shift_c], axis=1)

    h_all = _norm_call(x, ctx, norm_w, mods).reshape(b * t_all, d)

    w_in_b = w_in.astype(BF16)
    tables = _rope_tables(l, lc)
    tn = 512
    blocks = lambda lo, hi: list(range(lo // tn, hi // tn))
    qk_vecs = jnp.stack([q_norm_w, k_norm_w]).reshape(2, 1, QK_DIM)
    ret_vecs = jnp.stack([jnp.ones((QK_DIM,), F32), jnp.full((QK_DIM,), QK_DIM ** -0.5, F32)]).reshape(2, 1, QK_DIM)
    call = functools.partial(_inproj_call, h_all, w_in_b, tables, rows_per_batch=t_all)
    p_nr = call(qk_vecs, mode="normrope", col_blocks=blocks(O_DQ, O_DV), out_w=V_DIM)
    p_rope = call(ret_vecs, mode="rope", col_blocks=blocks(O_RQ, O_RV), out_w=QK_DIM)
    p_plain = call(None, mode="plain", col_blocks=blocks(O_DV, O_RQ) + blocks(O_RV, O_RG + RET_WIDTH), out_w=V_DIM)
    p_nr = p_nr.reshape(-1, b, t_all, V_DIM)
    p_rope = p_rope.reshape(-1, b, t_all, QK_DIM)
    p_plain = p_plain.reshape(-1, b, t_all, V_DIM)

    o_d = _attn_call(p_nr, p_plain, lam_params, subln_w, l)
    decs = jnp.broadcast_to(jnp.stack([dec_f, dec_b], axis=1)[:, :, None], (RET_HEADS, 2, QK_DIM)).astype(F32)
    o_r = _ret_call(p_rope, p_plain, decs, ret_norm_w, l)

    return _outproj_call(o_d, o_r, w_out.astype(BF16), x, gate)


def kernel(x, c, ctx, c_ctx, norm_w, ada_w, ada_b, w_in, diff_q_norm_w, diff_k_norm_w, diff_lambda_q1,
           diff_lambda_k1, diff_lambda_q2, diff_lambda_k2, diff_subln_w, ret_decay_fwd, ret_decay_bwd,
           ret_norm_w, w_out):
    depth = norm_w.shape[0]
    assert depth == 1, "LAM_INIT is the layer-0 value"
    for layer in range(depth):
        lam_params = jnp.stack([diff_lambda_q1[layer], diff_lambda_k1[layer],
                                diff_lambda_q2[layer], diff_lambda_k2[layer]]).astype(F32)
        x = _layer(x, ctx, c, c_ctx, norm_w[layer], ada_w[layer], ada_b[layer], w_in[layer],
                   diff_q_norm_w[layer], diff_k_norm_w[layer], lam_params, diff_subln_w[layer],
                   ret_decay_fwd[layer], ret_decay_bwd[layer], ret_norm_w[layer], w_out[layer])
    return x
```

```python
import functools
import math

import jax
import jax.numpy as jnp
from jax import lax
from jax.experimental import pallas as pl
from jax.experimental.pallas import tpu as pltpu

F32 = jnp.float32
BF16 = jnp.bfloat16

GRID_W = 64
DIFF_HEADS = 8
QK_DIM = 128
V_DIM = 256
RET_HEADS = 8
ROPE_BASE = 10000.0
EPS = 1e-6
LAM_INIT = 0.8 - 0.6 * math.exp(-0.3 * 0)

DIFF_QK_W = DIFF_HEADS * 2 * QK_DIM
DIFF_WIDTH = DIFF_HEADS * V_DIM
RET_QK_W = RET_HEADS * QK_DIM
RET_WIDTH = RET_HEADS * V_DIM
O_DQ = 0
O_DK = O_DQ + DIFF_QK_W
O_DV = O_DK + DIFF_QK_W
O_DG = O_DV + DIFF_WIDTH
O_RQ = O_DG + DIFF_WIDTH
O_RK = O_RQ + RET_QK_W
O_RV = O_RK + RET_QK_W
O_RG = O_RV + RET_WIDTH

V7X_VMEM_LIMIT_BYTES = 56 * 1024 * 1024
ARB = "arbitrary"


def _pick(n, candidates):
    for c in candidates:
        if n % c == 0:
            return c
    raise ValueError(f"no tile in {candidates} divides {n}")


def _params(*sem):
    return pltpu.CompilerParams(dimension_semantics=sem, vmem_limit_bytes=V7X_VMEM_LIMIT_BYTES)


def _ada_kernel(c_ref, w_ref, b_ref, o_ref):
    c = c_ref[...]
    a = (c * jax.nn.sigmoid(c)).astype(BF16)
    o_ref[...] = jnp.dot(a, w_ref[...].astype(BF16), preferred_element_type=F32) + b_ref[...]


def _ada_call(cc, ada_w, ada_b):
    rows, d = cc.shape
    n = ada_w.shape[1]
    tn = _pick(n, (512, 256, 128))
    return pl.pallas_call(
        _ada_kernel,
        grid=(n // tn,),
        in_specs=[pl.BlockSpec((rows, d), lambda j: (0, 0)),
                  pl.BlockSpec((d, tn), lambda j: (0, j)),
                  pl.BlockSpec((1, tn), lambda j: (0, j))],
        out_specs=pl.BlockSpec((rows, tn), lambda j: (0, j)),
        out_shape=jax.ShapeDtypeStruct((rows, n), F32),
        compiler_params=_params(ARB),
        name="ada_mod",
    )(cc, ada_w, ada_b.reshape(1, n))


def _norm_kernel(x_ref, ctx_ref, nw_ref, mod_ref, o_ref, *, n_lat):
    i = pl.program_id(1)

    def norm_mod(xv, scale, shift):
        ms = jnp.mean(xv * xv, axis=-1, keepdims=True)
        y = xv * lax.rsqrt(ms + EPS) * nw_ref[...]
        return (y * (1.0 + scale) + shift).astype(o_ref.dtype)

    @pl.when(i < n_lat)
    def _():
        o_ref[...] = norm_mod(x_ref[...], mod_ref[0:1, :], mod_ref[1:2, :])

    @pl.when(i >= n_lat)
    def _():
        o_ref[...] = norm_mod(ctx_ref[...], mod_ref[2:3, :], mod_ref[3:4, :])


def _norm_call(x, ctx, norm_w, mods):
    b, l, d = x.shape
    lc = ctx.shape[1]
    tr = _pick(math.gcd(l, lc), (256, 128, 64, 32, 16))
    n_lat, n_ctx = l // tr, lc // tr
    return pl.pallas_call(
        functools.partial(_norm_kernel, n_lat=n_lat),
        grid=(b, n_lat + n_ctx),
        in_specs=[pl.BlockSpec((None, tr, d), lambda bb, i: (bb, jnp.minimum(i, n_lat - 1), 0)),
                  pl.BlockSpec((None, tr, d), lambda bb, i: (bb, jnp.maximum(i - n_lat, 0), 0)),
                  pl.BlockSpec((1, d), lambda bb, i: (0, 0)),
                  pl.BlockSpec((None, 4, d), lambda bb, i: (bb, 0, 0))],
        out_specs=pl.BlockSpec((None, tr, d), lambda bb, i: (bb, i, 0)),
        out_shape=jax.ShapeDtypeStruct((b, l + lc, d), BF16),
        compiler_params=_params(ARB, ARB),
        name="norm_mod",
    )(x, ctx, norm_w.reshape(1, d), mods)


def _rope(xh, cos, sin_lo, sin_hi):
    return xh * cos + pltpu.roll(xh, 96, 1) * sin_lo + pltpu.roll(xh, 32, 1) * sin_hi


def _inproj_kernel(cb_ref, a_ref, w_ref, *rest, mode, out_w):
    del cb_ref
    acc = jnp.dot(a_ref[...], w_ref[...], preferred_element_type=F32)
    tn = acc.shape[1]
    if mode == "plain":
        (o_ref,) = rest
        for c in range(tn // out_w):
            o_ref[c] = acc[:, c * out_w:(c + 1) * out_w].astype(o_ref.dtype)
        return
    cos_ref, slo_ref, shi_ref, vec_ref, o_ref = rest
    cos, slo, shi, vec = cos_ref[...], slo_ref[...], shi_ref[...], vec_ref[...]
    per = out_w // QK_DIM
    for t in range(tn // QK_DIM):
        xh = acc[:, t * QK_DIM:(t + 1) * QK_DIM]
        if mode == "normrope":
            xh = xh * lax.rsqrt(jnp.mean(xh * xh, axis=-1, keepdims=True) + EPS) * vec
            r = _rope(xh, cos, slo, shi)
        else:
            r = _rope(xh, cos, slo, shi) * vec
        lane0 = (t % per) * QK_DIM
        o_ref[t // per, :, lane0:lane0 + QK_DIM] = r.astype(o_ref.dtype)


def _inproj_call(a2d, w, tables, vecs, *, mode, col_blocks, out_w, rows_per_batch):
    m, d = a2d.shape
    tn = 512
    tm = _pick(rows_per_batch, (768, 512, 384, 256, 128))
    n_j = len(col_blocks)
    n_out = n_j * tn // out_w
    col_tab = jnp.asarray(col_blocks, jnp.int32)
    t_blocks = rows_per_batch // tm
    grid = (m // tm, n_j)
    in_specs = [pl.BlockSpec((tm, d), lambda i, j, cb: (i, 0)),
                pl.BlockSpec((d, tn), lambda i, j, cb: (0, cb[j]))]
    args = [a2d, w]
    if mode != "plain":
        n_vec = vecs.shape[0]
        j_per_vec = n_j // n_vec
        tab_spec = pl.BlockSpec((tm, QK_DIM), lambda i, j, cb: (i % t_blocks, 0))
        in_specs += [tab_spec, tab_spec, tab_spec,
                     pl.BlockSpec((None, 1, QK_DIM), lambda i, j, cb: (j // j_per_vec, 0, 0))]
        args += [*tables, vecs]
    return pl.pallas_call(
        functools.partial(_inproj_kernel, mode=mode, out_w=out_w),
        grid_spec=pltpu.PrefetchScalarGridSpec(
            num_scalar_prefetch=1,
            grid=grid,
            in_specs=in_specs,
            out_specs=pl.BlockSpec((tn // out_w, tm, out_w), lambda i, j, cb: (j, i, 0)),
        ),
        out_shape=jax.ShapeDtypeStruct((n_out, m, out_w), BF16),
        compiler_params=_params(ARB, ARB),
        name="inproj_" + mode,
    )(col_tab, *args)


def _attn_kernel(lamp_ref, q_ref, k_ref, v_ref, g_ref, sw_ref, o_ref, vt_sc, s_sc, p_sc, acc_sc, *, tk):
    t_all = k_ref.shape[0]
    tq = q_ref.shape[0]
    n = t_all // tk
    c_exp = (QK_DIM ** -0.5) * math.log2(math.e)
    nt = (((1,), (1,)), ((), ()))

    @pl.when(pl.program_id(2) == 0)
    def _():
        def transpose_chunk(c, carry):
            rows = pl.ds(pl.multiple_of(c * tk, tk), tk)
            vt_sc[c] = v_ref[rows, :].astype(F32).T.astype(BF16)
            return carry
        lax.fori_loop(0, n, transpose_chunk, 0)

    q = q_ref[...]
    qs = (q[:, :QK_DIM], q[:, QK_DIM:])
    acc_sc[...] = jnp.zeros_like(acc_sc)

    def stage_qk(c):
        rows = pl.ds(pl.multiple_of(c * tk, tk), tk)
        kc = k_ref[rows, :]
        cms = []
        for sub in range(2):
            s = lax.dot_general(kc[:, sub * QK_DIM:(sub + 1) * QK_DIM], qs[sub], nt,
                                preferred_element_type=F32)
            s_sc[sub] = s
            cms.append(jnp.max(s, axis=0, keepdims=True))
        return tuple(cms)

    def stage_softmax(cms, ms, ls):
        new_m, new_l, alphas = [], [], []
        for sub in range(2):
            m_new = jnp.maximum(ms[sub], cms[sub])
            alpha = jnp.exp2((ms[sub] - m_new) * c_exp)
            p = jnp.exp2(s_sc[sub] * c_exp - m_new * c_exp)
            new_l.append(alpha * ls[sub] + jnp.sum(p.reshape(tk // 8, 8, tq), axis=0))
            p_sc[sub] = p.astype(BF16)
            new_m.append(m_new)
            alphas.append(alpha)
        return tuple(new_m), tuple(new_l), tuple(alphas)

    def stage_pv(c, alphas):
        vt = vt_sc[c]
        for sub in range(2):
            acc_sc[sub] = alphas[sub] * acc_sc[sub] + jnp.dot(vt, p_sc[sub], preferred_element_type=F32)

    neg = jnp.full((1, tq), -jnp.inf, F32)
    ms = (neg, neg)
    ls = (jnp.zeros((8, tq), F32),) * 2
    cms = stage_qk(0)
    ms, ls, alphas = stage_softmax(cms, ms, ls)
    cms = stage_qk(1)

    def tick(t, carry):
        cms, ms, ls, alphas = carry
        stage_pv(t - 2, alphas)
        ms, ls, alphas = stage_softmax(cms, ms, ls)
        cms = stage_qk(t)
        return cms, ms, ls, alphas

    cms, ms, ls, alphas = lax.fori_loop(2, n, tick, (cms, ms, ls, alphas))
    stage_pv(n - 2, alphas)
    ms, ls, alphas = stage_softmax(cms, ms, ls)
    stage_pv(n - 1, alphas)

    lp = lamp_ref[...]
    lam = (jnp.exp(jnp.sum(lp[0:1] * lp[1:2], axis=1, keepdims=True))
           - jnp.exp(jnp.sum(lp[2:3] * lp[3:4], axis=1, keepdims=True)) + LAM_INIT)
    inv1 = 1.0 / jnp.sum(ls[0], axis=0, keepdims=True)
    inv2 = lam / jnp.sum(ls[1], axis=0, keepdims=True)
    o = (acc_sc[0] * inv1 - acc_sc[1] * inv2).T
    y = o * lax.rsqrt(jnp.mean(o * o, axis=-1, keepdims=True) + EPS) * sw_ref[...] * (1.0 - LAM_INIT)
    g = g_ref[...].astype(F32)
    o_ref[...] = (y * (g * jax.nn.sigmoid(g))).astype(o_ref.dtype)


def _attn_call(p_nr, p_plain, lam_params, subln_w, l):
    _, b, t_all, _ = p_nr.shape
    tq = _pick(l, (256, 128))
    tk = _pick(t_all, (768, 256))
    n = t_all // tk
    assert n >= 2, "the three-stage skew needs at least two key chunks"
    h = DIFF_HEADS
    return pl.pallas_call(
        functools.partial(_attn_kernel, tk=tk),
        grid=(b, h, l // tq),
        in_specs=[pl.BlockSpec((4, QK_DIM), lambda bb, hh, i: (0, 0)),
                  pl.BlockSpec((None, None, tq, V_DIM), lambda bb, hh, i: (hh, bb, i, 0)),
                  pl.BlockSpec((None, None, t_all, V_DIM), lambda bb, hh, i: (h + hh, bb, 0, 0)),
                  pl.BlockSpec((None, None, t_all, V_DIM), lambda bb, hh, i: (hh, bb, 0, 0)),
                  pl.BlockSpec((None, None, tq, V_DIM), lambda bb, hh, i: (h + hh, bb, i, 0)),
                  pl.BlockSpec((1, V_DIM), lambda bb, hh, i: (0, 0))],
        out_specs=pl.BlockSpec((None, tq, V_DIM), lambda bb, hh, i: (bb, i, hh)),
        out_shape=jax.ShapeDtypeStruct((b, l, DIFF_WIDTH), BF16),
        scratch_shapes=[pltpu.VMEM((n, V_DIM, tk), BF16),
                        pltpu.VMEM((2, tk, tq), F32),
                        pltpu.VMEM((2, tk, tq), BF16),
                        pltpu.VMEM((2, V_DIM, tq), F32)],
        compiler_params=_params(ARB, ARB, ARB),
        name="diff_attn",
    )(lam_params, p_nr, p_nr, p_plain, p_plain, subln_w.reshape(1, V_DIM))


def _ret_kernel(dec_ref, q_ref, k_ref, v_ref, g_ref, nw_ref, o_ref, ob_sc, *, chunk):
    l = q_ref.shape[0]
    lc = k_ref.shape[0] - l
    nc = l // chunk
    cdim = (((1,), (1,)), ((), ()))
    rdim = (((0,), (0,)), ((), ()))

    dec = dec_ref[...]
    lg = jnp.log1p(-jnp.exp2(-dec))
    lg_f, lg_b = lg[0:1, 0:1], lg[1:2, 0:1]

    def col_iota(n):
        return lax.broadcasted_iota(jnp.int32, (n, 1), 0).astype(F32)

    ic = col_iota(chunk)
    xi_f = jnp.exp(lg_f * (ic + 1.0))
    xi_b = jnp.exp(lg_b * (chunk - ic))
    zeta_f = jnp.exp(lg_f * (chunk - 1.0 - ic))
    zeta_b = jnp.exp(lg_b * ic)
    gc_f = jnp.exp(lg_f * chunk)
    gc_b = jnp.exp(lg_b * chunk)
    ri = lax.broadcasted_iota(jnp.int32, (chunk, chunk), 0)
    ci = lax.broadcasted_iota(jnp.int32, (chunk, chunk), 1)
    dist = (ri - ci).astype(F32)
    dmat = jnp.where(ri >= ci, jnp.exp(lg_f * jnp.maximum(dist, 0.0)), jnp.exp(lg_b * jnp.maximum(-dist, 0.0)))

    im = col_iota(lc)
    kctx = k_ref[pl.ds(l, lc), :].astype(F32)
    vctx = v_ref[pl.ds(l, lc), :]
    r0_f = lax.dot_general((kctx * jnp.exp(lg_f * (lc - 1.0 - im))).astype(BF16), vctx, rdim,
                           preferred_element_type=F32)
    r0_b = lax.dot_general((kctx * jnp.exp(lg_b * im)).astype(BF16), vctx, rdim, preferred_element_type=F32)

    def chunk_kv(rows, zeta):
        kz = (k_ref[rows, :].astype(F32) * zeta).astype(BF16)
        return lax.dot_general(kz, v_ref[rows, :], rdim, preferred_element_type=F32)

    def bwd_body(j, s_b):
        c = nc - 1 - j
        rows = pl.ds(pl.multiple_of(c * chunk, chunk), chunk)
        cross = jnp.dot(q_ref[rows, :], s_b.astype(BF16), preferred_element_type=F32)
        ob_sc[rows, :] = cross * xi_b
        return gc_b * s_b + chunk_kv(rows, zeta_b)

    lax.fori_loop(0, nc, bwd_body, r0_b)

    def fwd_body(c, s_f):
        rows = pl.ds(pl.multiple_of(c * chunk, chunk), chunk)
        qc = q_ref[rows, :]
        scores = lax.dot_general(qc, k_ref[rows, :], cdim, preferred_element_type=F32) * dmat
        o = jnp.dot(scores.astype(BF16), v_ref[rows, :], preferred_element_type=F32)
        o = o + jnp.dot(qc, s_f.astype(BF16), preferred_element_type=F32) * xi_f + ob_sc[rows, :]
        y = o * lax.rsqrt(jnp.mean(o * o, axis=-1, keepdims=True) + EPS) * nw_ref[...]
        g = g_ref[rows, :].astype(F32)
        o_ref[rows, :] = (y * (g * jax.nn.sigmoid(g))).astype(o_ref.dtype)
        return gc_f * s_f + chunk_kv(rows, zeta_f)

    lax.fori_loop(0, nc, fwd_body, r0_f)


def _ret_call(p_rope, p_plain, decs, ret_norm_w, l):
    _, b, t_all, _ = p_rope.shape
    h = RET_HEADS
    chunk = 128
    return pl.pallas_call(
        functools.partial(_ret_kernel, chunk=chunk),
        grid=(b, h),
        in_specs=[pl.BlockSpec((None, 2, QK_DIM), lambda bb, hh: (hh, 0, 0)),
                  pl.BlockSpec((None, None, l, QK_DIM), lambda bb, hh: (hh, bb, 0, 0)),
                  pl.BlockSpec((None, None, t_all, QK_DIM), lambda bb, hh: (h + hh, bb, 0, 0)),
                  pl.BlockSpec((None, None, t_all, V_DIM), lambda bb, hh: (2 * DIFF_HEADS + hh, bb, 0, 0)),
                  pl.BlockSpec((None, None, l, V_DIM), lambda bb, hh: (2 * DIFF_HEADS + h + hh, bb, 0, 0)),
                  pl.BlockSpec((1, V_DIM), lambda bb, hh: (0, 0))],
        out_specs=pl.BlockSpec((None, l, V_DIM), lambda bb, hh: (bb, 0, hh)),
        out_shape=jax.ShapeDtypeStruct((b, l, RET_WIDTH), BF16),
        scratch_shapes=[pltpu.VMEM((l, V_DIM), F32)],
        compiler_params=_params(ARB, ARB),
        name="retention",
    )(decs, p_rope, p_rope, p_plain, p_plain, ret_norm_w.reshape(1, V_DIM))


def _outproj_kernel(ad_ref, ar_ref, wd_ref, wr_ref, x_ref, gate_ref, o_ref):
    y = jnp.dot(ad_ref[...], wd_ref[...], preferred_element_type=F32)
    y = y + jnp.dot(ar_ref[...], wr_ref[...], preferred_element_type=F32)
    o_ref[...] = x_ref[...] + gate_ref[...] * y


def _outproj_call(o_d, o_r, w_out, x, gate):
    b, l, d = x.shape
    kd, kr = o_d.shape[-1], o_r.shape[-1]
    tm = _pick(l, (1024, 512, 256, 128))
    tn = _pick(d, (512, 256, 128))
    per_b = l // tm
    return pl.pallas_call(
        _outproj_kernel,
        grid=(b * per_b, d // tn),
        in_specs=[pl.BlockSpec((tm, kd), lambda i, j: (i, 0)),
                  pl.BlockSpec((tm, kr), lambda i, j: (i, 0)),
                  pl.BlockSpec((kd, tn), lambda i, j: (0, j)),
                  pl.BlockSpec((kr, tn), lambda i, j: (kd // kr, j)),
                  pl.BlockSpec((tm, tn), lambda i, j: (i, j)),
                  pl.BlockSpec((None, 1, tn), lambda i, j: (i // per_b, 0, j))],
        out_specs=pl.BlockSpec((tm, tn), lambda i, j: (i, j)),
        out_shape=jax.ShapeDtypeStruct((b * l, d), F32),
        compiler_params=_params(ARB, ARB),
        name="outproj",
    )(o_d.reshape(b * l, kd), o_r.reshape(b * l, kr), w_out, w_out, x.reshape(b * l, d),
      gate.reshape(b, 1, d)).reshape(b, l, d)


def _rope_tables(l, lc):
    row, col = jnp.meshgrid(jnp.arange(l // GRID_W), jnp.arange(GRID_W), indexing="ij")
    row = row.reshape(-1).astype(F32)
    col = col.reshape(-1).astype(F32)
    half = QK_DIM // 2
    inv_freq = ROPE_BASE ** (-jnp.arange(0, half, 2, dtype=F32) / half)
    ang_r = row[:, None] * inv_freq
    ang_c = col[:, None] * inv_freq
    ang = jnp.concatenate([ang_r, ang_r, ang_c, ang_c], axis=-1)
    cos, sin = jnp.cos(ang), jnp.sin(ang)
    lo = (jnp.arange(QK_DIM) % half) < (half // 2)
    cos = jnp.concatenate([cos, jnp.ones((lc, QK_DIM), F32)], axis=0)
    sin_lo = jnp.concatenate([jnp.where(lo, -sin, 0.0), jnp.zeros((lc, QK_DIM), F32)], axis=0)
    sin_hi = jnp.concatenate([jnp.where(lo, 0.0, sin), jnp.zeros((lc, QK_DIM), F32)], axis=0)
    return cos, sin_lo, sin_hi


def _layer(x, ctx, c, c_ctx, norm_w, ada_w, ada_b, w_in, q_norm_w, k_norm_w, lam_params, subln_w,
           dec_f, dec_b, ret_norm_w, w_out):
    b, l, d = x.shape
    lc = ctx.shape[1]
    t_all = l + lc

    rows = -(-(b + 1) // 8) * 8
    cc = jnp.zeros((rows, d), F32).at[:b].set(c).at[b].set(c_ctx)
    mod = _ada_call(cc, ada_w, ada_b)
    shift, scale, gate = mod[:b, :d], mod[:b, d:2 * d], mod[:b, 2 * d:]
    shift_c = jnp.broadcast_to(mod[b, :d], (b, d))
    scale_c = jnp.broadcast_to(mod[b, d:2 * d], (b, d))
    mods = jnp.stack([scale, shift, scale_c, shift_c], axis=1)

    h_all = _norm_call(x, ctx, norm_w, mods).reshape(b * t_all, d)

    w_in_b = w_in.astype(BF16)
    tables = _rope_tables(l, lc)
    tn = 512
    blocks = lambda lo, hi: list(range(lo // tn, hi // tn))
    qk_vecs = jnp.stack([q_norm_w, k_norm_w]).reshape(2, 1, QK_DIM)
    ret_vecs = jnp.stack([jnp.ones((QK_DIM,), F32), jnp.full((QK_DIM,), QK_DIM ** -0.5, F32)]).reshape(2, 1, QK_DIM)
    call = functools.partial(_inproj_call, h_all, w_in_b, tables, rows_per_batch=t_all)
    p_nr = call(qk_vecs, mode="normrope", col_blocks=blocks(O_DQ, O_DV), out_w=V_DIM)
    p_rope = call(ret_vecs, mode="rope", col_blocks=blocks(O_RQ, O_RV), out_w=QK_DIM)
    p_plain = call(None, mode="plain", col_blocks=blocks(O_DV, O_RQ) + blocks(O_RV, O_RG + RET_WIDTH), out_w=V_DIM)
    p_nr = p_nr.reshape(-1, b, t_all, V_DIM)
    p_rope = p_rope.reshape(-1, b, t_all, QK_DIM)
    p_plain = p_plain.reshape(-1, b, t_all, V_DIM)

    o_d = _attn_call(p_nr, p_plain, lam_params, subln_w, l)
    decs = jnp.broadcast_to(jnp.stack([dec_f, dec_b], axis=1)[:, :, None], (RET_HEADS, 2, QK_DIM)).astype(F32)
    o_r = _ret_call(p_rope, p_plain, decs, ret_norm_w, l)

    return _outproj_call(o_d, o_r, w_out.astype(BF16), x, gate)


def kernel(x, c, ctx, c_ctx, norm_w, ada_w, ada_b, w_in, diff_q_norm_w, diff_k_norm_w, diff_lambda_q1,
           diff_lambda_k1, diff_lambda_q2, diff_lambda_k2, diff_subln_w, ret_decay_fwd, ret_decay_bwd,
           ret_norm_w, w_out):
    depth = norm_w.shape[0]
    assert depth == 1, "LAM_INIT is the layer-0 value"
    for layer in range(depth):
        lam_params = jnp.stack([diff_lambda_q1[layer], diff_lambda_k1[layer],
                                diff_lambda_q2[layer], diff_lambda_k2[layer]]).astype(F32)
        x = _layer(x, ctx, c, c_ctx, norm_w[layer], ada_w[layer], ada_b[layer], w_in[layer],
                   diff_q_norm_w[layer], diff_k_norm_w[layer], lam_params, diff_subln_w[layer],
                   ret_decay_fwd[layer], ret_decay_bwd[layer], ret_norm_w[layer], w_out[layer])
    return x
```

```python
import functools
import math

import jax
import jax.numpy as jnp
from jax import lax
from jax.experimental import pallas as pl
from jax.experimental.pallas import tpu as pltpu

F32 = jnp.float32
BF16 = jnp.bfloat16

GRID_W = 64
DIFF_HEADS = 8
QK_DIM = 128
V_DIM = 256
RET_HEADS = 8
ROPE_BASE = 10000.0
EPS = 1e-6
LAM_INIT = 0.8 - 0.6 * math.exp(-0.3 * 0)

DIFF_QK_W = DIFF_HEADS * 2 * QK_DIM
DIFF_WIDTH = DIFF_HEADS * V_DIM
RET_QK_W = RET_HEADS * QK_DIM
RET_WIDTH = RET_HEADS * V_DIM
O_DQ = 0
O_DK = O_DQ + DIFF_QK_W
O_DV = O_DK + DIFF_QK_W
O_DG = O_DV + DIFF_WIDTH
O_RQ = O_DG + DIFF_WIDTH
O_RK = O_RQ + RET_QK_W
O_RV = O_RK + RET_QK_W
O_RG = O_RV + RET_WIDTH

V7X_VMEM_LIMIT_BYTES = 56 * 1024 * 1024
ARB = "arbitrary"
RET_UNROLL = 8


def _pick(n, candidates):
    for c in candidates:
        if n % c == 0:
            return c
    raise ValueError(f"no tile in {candidates} divides {n}")


def _params(*sem):
    return pltpu.CompilerParams(dimension_semantics=sem, vmem_limit_bytes=V7X_VMEM_LIMIT_BYTES)


def _ada_kernel(c_ref, w_ref, b_ref, o_ref):
    c = c_ref[...]
    a = (c * jax.nn.sigmoid(c)).astype(BF16)
    o_ref[...] = jnp.dot(a, w_ref[...].astype(BF16), preferred_element_type=F32) + b_ref[...]


def _ada_call(cc, ada_w, ada_b):
    rows, d = cc.shape
    n = ada_w.shape[1]
    tn = _pick(n, (512, 256, 128))
    return pl.pallas_call(
        _ada_kernel,
        grid=(n // tn,),
        in_specs=[pl.BlockSpec((rows, d), lambda j: (0, 0)),
                  pl.BlockSpec((d, tn), lambda j: (0, j)),
                  pl.BlockSpec((1, tn), lambda j: (0, j))],
        out_specs=pl.BlockSpec((rows, tn), lambda j: (0, j)),
        out_shape=jax.ShapeDtypeStruct((rows, n), F32),
        compiler_params=_params(ARB),
        name="ada_mod",
    )(cc, ada_w, ada_b.reshape(1, n))


def _norm_kernel(x_ref, ctx_ref, nw_ref, mod_ref, o_ref, *, n_lat):
    i = pl.program_id(1)

    def norm_mod(xv, scale, shift):
        ms = jnp.mean(xv * xv, axis=-1, keepdims=True)
        y = xv * lax.rsqrt(ms + EPS) * nw_ref[...]
        return (y * (1.0 + scale) + shift).astype(o_ref.dtype)

    @pl.when(i < n_lat)
    def _():
        o_ref[...] = norm_mod(x_ref[...], mod_ref[0:1, :], mod_ref[1:2, :])

    @pl.when(i >= n_lat)
    def _():
        o_ref[...] = norm_mod(ctx_ref[...], mod_ref[2:3, :], mod_ref[3:4, :])


def _norm_call(x, ctx, norm_w, mods):
    b, l, d = x.shape
    lc = ctx.shape[1]
    tr = _pick(math.gcd(l, lc), (256, 128, 64, 32, 16))
    n_lat, n_ctx = l // tr, lc // tr
    return pl.pallas_call(
        functools.partial(_norm_kernel, n_lat=n_lat),
        grid=(b, n_lat + n_ctx),
        in_specs=[pl.BlockSpec((None, tr, d), lambda bb, i: (bb, jnp.minimum(i, n_lat - 1), 0)),
                  pl.BlockSpec((None, tr, d), lambda bb, i: (bb, jnp.maximum(i - n_lat, 0), 0)),
                  pl.BlockSpec((1, d), lambda bb, i: (0, 0)),
                  pl.BlockSpec((None, 4, d), lambda bb, i: (bb, 0, 0))],
        out_specs=pl.BlockSpec((None, tr, d), lambda bb, i: (bb, i, 0)),
        out_shape=jax.ShapeDtypeStruct((b, l + lc, d), BF16),
        compiler_params=_params(ARB, ARB),
        name="norm_mod",
    )(x, ctx, norm_w.reshape(1, d), mods)


def _rope(xh, cos, sin_lo, sin_hi):
    return xh * cos + pltpu.roll(xh, 96, 1) * sin_lo + pltpu.roll(xh, 32, 1) * sin_hi


def _inproj_plain_kernel(cb_ref, a_ref, w_ref, o_ref, *, out_w):
    del cb_ref
    acc = jnp.dot(a_ref[...], w_ref[...], preferred_element_type=F32)
    for c in range(acc.shape[1] // out_w):
        o_ref[c] = acc[:, c * out_w:(c + 1) * out_w].astype(o_ref.dtype)


def _inproj_rope_kernel(cb_ref, a_ref, w_ref, cos_ref, slo_ref, shi_ref, vec_ref, o_ref, acc_sc, *, norm, out_w):
    del cb_ref

    @pl.when(pl.program_id(0) == 0)
    def _():
        acc_sc[...] = jnp.zeros_like(acc_sc)

    cos, slo, shi, vec = cos_ref[...], slo_ref[...], shi_ref[...], vec_ref[...]
    per = out_w // QK_DIM
    for t in range(acc_sc.shape[1] // QK_DIM):
        xh = acc_sc[:, t * QK_DIM:(t + 1) * QK_DIM]
        if norm:
            xh = xh * lax.rsqrt(jnp.mean(xh * xh, axis=-1, keepdims=True) + EPS) * vec
            r = _rope(xh, cos, slo, shi)
        else:
            r = _rope(xh, cos, slo, shi) * vec
        lane0 = (t % per) * QK_DIM
        o_ref[t // per, :, lane0:lane0 + QK_DIM] = r.astype(o_ref.dtype)
    acc_sc[...] = jnp.dot(a_ref[...], w_ref[...], preferred_element_type=F32)


def _inproj_call(a2d, w, tables, vecs, *, mode, col_blocks, out_w, rows_per_batch):
    m, d = a2d.shape
    tn = 512
    tm = _pick(rows_per_batch, (768, 512, 384, 256, 128))
    n_i, n_j = m // tm, len(col_blocks)
    n_out = n_j * tn // out_w
    col_tab = jnp.asarray(col_blocks, jnp.int32)
    out_shape = jax.ShapeDtypeStruct((n_out, m, out_w), BF16)
    if mode == "plain":
        return pl.pallas_call(
            functools.partial(_inproj_plain_kernel, out_w=out_w),
            grid_spec=pltpu.PrefetchScalarGridSpec(
                num_scalar_prefetch=1,
                grid=(n_i, n_j),
                in_specs=[pl.BlockSpec((tm, d), lambda i, j, cb: (i, 0)),
                          pl.BlockSpec((d, tn), lambda i, j, cb: (0, cb[j]))],
                out_specs=pl.BlockSpec((tn // out_w, tm, out_w), lambda i, j, cb: (j, i, 0)),
            ),
            out_shape=out_shape,
            compiler_params=_params(ARB, ARB),
            name="inproj_plain",
        )(col_tab, a2d, w)

    steps = n_i * n_j
    t_blocks = rows_per_batch // tm
    j_per_vec = n_j // vecs.shape[0]
    cur = lambda s: jnp.minimum(s, steps - 1)
    prev = lambda s: jnp.maximum(s - 1, 0)
    tab_spec = pl.BlockSpec((tm, QK_DIM), lambda s, cb: ((prev(s) // n_j) % t_blocks, 0))
    return pl.pallas_call(
        functools.partial(_inproj_rope_kernel, norm=(mode == "normrope"), out_w=out_w),
        grid_spec=pltpu.PrefetchScalarGridSpec(
            num_scalar_prefetch=1,
            grid=(steps + 1,),
            in_specs=[pl.BlockSpec((tm, d), lambda s, cb: (cur(s) // n_j, 0)),
                      pl.BlockSpec((d, tn), lambda s, cb: (0, cb[cur(s) % n_j])),
                      tab_spec, tab_spec, tab_spec,
                      pl.BlockSpec((None, 1, QK_DIM), lambda s, cb: ((prev(s) % n_j) // j_per_vec, 0, 0))],
            out_specs=pl.BlockSpec((tn // out_w, tm, out_w), lambda s, cb: (prev(s) % n_j, prev(s) // n_j, 0)),
            scratch_shapes=[pltpu.VMEM((tm, tn), F32)],
        ),
        out_shape=out_shape,
        compiler_params=_params(ARB),
        name="inproj_" + mode,
    )(col_tab, a2d, w, *tables, vecs)


def _attn_kernel(lamp_ref, q_ref, k_ref, v_ref, g_ref, sw_ref, o_ref, vt_sc, s_sc, p_sc, acc_sc, *, tk):
    t_all = k_ref.shape[0]
    tq = q_ref.shape[0]
    n = t_all // tk
    c_exp = (QK_DIM ** -0.5) * math.log2(math.e)
    nt = (((1,), (1,)), ((), ()))

    @pl.when(pl.program_id(2) == 0)
    def _():
        def transpose_chunk(c, carry):
            rows = pl.ds(pl.multiple_of(c * tk, tk), tk)
            vt_sc[c] = v_ref[rows, :].astype(F32).T.astype(BF16)
            return carry
        lax.fori_loop(0, n, transpose_chunk, 0)

    q = q_ref[...]
    qs = (q[:, :QK_DIM], q[:, QK_DIM:])
    acc_sc[...] = jnp.zeros_like(acc_sc)

    def stage_qk(c):
        rows = pl.ds(pl.multiple_of(c * tk, tk), tk)
        kc = k_ref[rows, :]
        cms = []
        for sub in range(2):
            s = lax.dot_general(kc[:, sub * QK_DIM:(sub + 1) * QK_DIM], qs[sub], nt,
                                preferred_element_type=F32)
            s_sc[sub] = s
            cms.append(jnp.max(s, axis=0, keepdims=True))
        return tuple(cms)

    def stage_softmax(cms, ms, ls):
        new_m, new_l, alphas = [], [], []
        for sub in range(2):
            m_new = jnp.maximum(ms[sub], cms[sub])
            alpha = jnp.exp2((ms[sub] - m_new) * c_exp)
            p = jnp.exp2(s_sc[sub] * c_exp - m_new * c_exp)
            new_l.append(alpha * ls[sub] + jnp.sum(p.reshape(tk // 8, 8, tq), axis=0))
            p_sc[sub] = p.astype(BF16)
            new_m.append(m_new)
            alphas.append(alpha)
        return tuple(new_m), tuple(new_l), tuple(alphas)

    def stage_pv(c, alphas):
        vt = vt_sc[c]
        for sub in range(2):
            acc_sc[sub] = alphas[sub] * acc_sc[sub] + jnp.dot(vt, p_sc[sub], preferred_element_type=F32)

    neg = jnp.full((1, tq), -jnp.inf, F32)
    ms = (neg, neg)
    ls = (jnp.zeros((8, tq), F32),) * 2
    cms = stage_qk(0)
    ms, ls, alphas = stage_softmax(cms, ms, ls)
    cms = stage_qk(1)

    def tick(t, carry):
        cms, ms, ls, alphas = carry
        stage_pv(t - 2, alphas)
        ms, ls, alphas = stage_softmax(cms, ms, ls)
        cms = stage_qk(t)
        return cms, ms, ls, alphas

    cms, ms, ls, alphas = lax.fori_loop(2, n, tick, (cms, ms, ls, alphas))
    stage_pv(n - 2, alphas)
    ms, ls, alphas = stage_softmax(cms, ms, ls)
    stage_pv(n - 1, alphas)

    lp = lamp_ref[...]
    lam = (jnp.exp(jnp.sum(lp[0:1] * lp[1:2], axis=1, keepdims=True))
           - jnp.exp(jnp.sum(lp[2:3] * lp[3:4], axis=1, keepdims=True)) + LAM_INIT)
    inv1 = 1.0 / jnp.sum(ls[0], axis=0, keepdims=True)
    inv2 = lam / jnp.sum(ls[1], axis=0, keepdims=True)
    o = (acc_sc[0] * inv1 - acc_sc[1] * inv2).T
    y = o * lax.rsqrt(jnp.mean(o * o, axis=-1, keepdims=True) + EPS) * sw_ref[...] * (1.0 - LAM_INIT)
    g = g_ref[...].astype(F32)
    o_ref[...] = (y * (g * jax.nn.sigmoid(g))).astype(o_ref.dtype)


def _attn_call(p_nr, p_plain, lam_params, subln_w, l):
    _, b, t_all, _ = p_nr.shape
    tq = _pick(l, (512, 256, 128))
    tk = _pick(t_all, (768, 256))
    n = t_all // tk
    assert n >= 2, "the three-stage skew needs at least two key chunks"
    h = DIFF_HEADS
    return pl.pallas_call(
        functools.partial(_attn_kernel, tk=tk),
        grid=(b, h, l // tq),
        in_specs=[pl.BlockSpec((4, QK_DIM), lambda bb, hh, i: (0, 0)),
                  pl.BlockSpec((None, None, tq, V_DIM), lambda bb, hh, i: (hh, bb, i, 0)),
                  pl.BlockSpec((None, None, t_all, V_DIM), lambda bb, hh, i: (h + hh, bb, 0, 0)),
                  pl.BlockSpec((None, None, t_all, V_DIM), lambda bb, hh, i: (hh, bb, 0, 0)),
                  pl.BlockSpec((None, None, tq, V_DIM), lambda bb, hh, i: (h + hh, bb, i, 0)),
                  pl.BlockSpec((1, V_DIM), lambda bb, hh, i: (0, 0))],
        out_specs=pl.BlockSpec((None, tq, V_DIM), lambda bb, hh, i: (bb, i, hh)),
        out_shape=jax.ShapeDtypeStruct((b, l, DIFF_WIDTH), BF16),
        scratch_shapes=[pltpu.VMEM((n, V_DIM, tk), BF16),
                        pltpu.VMEM((2, tk, tq), F32),
                        pltpu.VMEM((2, tk, tq), BF16),
                        pltpu.VMEM((2, V_DIM, tq), F32)],
        compiler_params=_params(ARB, ARB, ARB),
        name="diff_attn",
    )(lam_params, p_nr, p_nr, p_plain, p_plain, subln_w.reshape(1, V_DIM))


def _ret_kernel(dec_ref, q_ref, k_ref, v_ref, g_ref, nw_ref, o_ref, ob_sc, *, chunk):
    l = q_ref.shape[0]
    lc = k_ref.shape[0] - l
    nc = l // chunk
    cdim = (((1,), (1,)), ((), ()))
    rdim = (((0,), (0,)), ((), ()))

    dec = dec_ref[...]
    lg = jnp.log1p(-jnp.exp2(-dec))
    lg_f, lg_b = lg[0:1, 0:1], lg[1:2, 0:1]

    def col_iota(n):
        return lax.broadcasted_iota(jnp.int32, (n, 1), 0).astype(F32)

    ic = col_iota(chunk)
    xi_f = jnp.exp(lg_f * (ic + 1.0))
    xi_b = jnp.exp(lg_b * (chunk - ic))
    zeta_f = jnp.exp(lg_f * (chunk - 1.0 - ic))
    zeta_b = jnp.exp(lg_b * ic)
    gc_f = jnp.exp(lg_f * chunk)
    gc_b = jnp.exp(lg_b * chunk)
    ri = lax.broadcasted_iota(jnp.int32, (chunk, chunk), 0)
    ci = lax.broadcasted_iota(jnp.int32, (chunk, chunk), 1)
    dist = (ri - ci).astype(F32)
    dmat = jnp.where(ri >= ci, jnp.exp(lg_f * jnp.maximum(dist, 0.0)), jnp.exp(lg_b * jnp.maximum(-dist, 0.0)))

    im = col_iota(lc)
    kctx = k_ref[pl.ds(l, lc), :].astype(F32)
    vctx = v_ref[pl.ds(l, lc), :]
    r0_f = lax.dot_general((kctx * jnp.exp(lg_f * (lc - 1.0 - im))).astype(BF16), vctx, rdim,
                           preferred_element_type=F32)
    r0_b = lax.dot_general((kctx * jnp.exp(lg_b * im)).astype(BF16), vctx, rdim, preferred_element_type=F32)

    def chunk_kv(rows, zeta):
        kz = (k_ref[rows, :].astype(F32) * zeta).astype(BF16)
        return lax.dot_general(kz, v_ref[rows, :], rdim, preferred_element_type=F32)

    def bwd_body(j, s_b):
        c = nc - 1 - j
        rows = pl.ds(pl.multiple_of(c * chunk, chunk), chunk)
        cross = jnp.dot(q_ref[rows, :], s_b.astype(BF16), preferred_element_type=F32)
        ob_sc[rows, :] = cross * xi_b
        return gc_b * s_b + chunk_kv(rows, zeta_b)

    lax.fori_loop(0, nc, bwd_body, r0_b, unroll=RET_UNROLL)

    def fwd_body(c, s_f):
        rows = pl.ds(pl.multiple_of(c * chunk, chunk), chunk)
        qc = q_ref[rows, :]
        scores = lax.dot_general(qc, k_ref[rows, :], cdim, preferred_element_type=F32) * dmat
        o = jnp.dot(scores.astype(BF16), v_ref[rows, :], preferred_element_type=F32)
        o = o + jnp.dot(qc, s_f.astype(BF16), preferred_element_type=F32) * xi_f + ob_sc[rows, :]
        y = o * lax.rsqrt(jnp.mean(o * o, axis=-1, keepdims=True) + EPS) * nw_ref[...]
        g = g_ref[rows, :].astype(F32)
        o_ref[rows, :] = (y * (g * jax.nn.sigmoid(g))).astype(o_ref.dtype)
        return gc_f * s_f + chunk_kv(rows, zeta_f)

    lax.fori_loop(0, nc, fwd_body, r0_f, unroll=RET_UNROLL)


def _ret_call(p_rope, p_plain, decs, ret_norm_w, l):
    _, b, t_all, _ = p_rope.shape
    h = RET_HEADS
    chunk = _pick(l, (256, 128))
    return pl.pallas_call(
        functools.partial(_ret_kernel, chunk=chunk),
        grid=(b, h),
        in_specs=[pl.BlockSpec((None, 2, QK_DIM), lambda bb, hh: (hh, 0, 0)),
                  pl.BlockSpec((None, None, l, QK_DIM), lambda bb, hh: (hh, bb, 0, 0)),
                  pl.BlockSpec((None, None, t_all, QK_DIM), lambda bb, hh: (h + hh, bb, 0, 0)),
                  pl.BlockSpec((None, None, t_all, V_DIM), lambda bb, hh: (2 * DIFF_HEADS + hh, bb, 0, 0)),
                  pl.BlockSpec((None, None, l, V_DIM), lambda bb, hh: (2 * DIFF_HEADS + h + hh, bb, 0, 0)),
                  pl.BlockSpec((1, V_DIM), lambda bb, hh: (0, 0))],
        out_specs=pl.BlockSpec((None, l, V_DIM), lambda bb, hh: (bb, 0, hh)),
        out_shape=jax.ShapeDtypeStruct((b, l, RET_WIDTH), BF16),
        scratch_shapes=[pltpu.VMEM((l, V_DIM), F32)],
        compiler_params=_params(ARB, ARB),
        name="retention",
    )(decs, p_rope, p_rope, p_plain, p_plain, ret_norm_w.reshape(1, V_DIM))


def _outproj_kernel(ad_ref, ar_ref, wd_ref, wr_ref, x_ref, gate_ref, o_ref):
    y = jnp.dot(ad_ref[...], wd_ref[...], preferred_element_type=F32)
    y = y + jnp.dot(ar_ref[...], wr_ref[...], preferred_element_type=F32)
    o_ref[...] = x_ref[...] + gate_ref[...] * y


def _outproj_call(o_d, o_r, w_out, x, gate):
    b, l, d = x.shape
    kd, kr = o_d.shape[-1], o_r.shape[-1]
    tm = _pick(l, (1024, 512, 256, 128))
    tn = _pick(d, (512, 256, 128))
    per_b = l // tm
    return pl.pallas_call(
        _outproj_kernel,
        grid=(b * per_b, d // tn),
        in_specs=[pl.BlockSpec((tm, kd), lambda i, j: (i, 0)),
                  pl.BlockSpec((tm, kr), lambda i, j: (i, 0)),
                  pl.BlockSpec((kd, tn), lambda i, j: (0, j)),
                  pl.BlockSpec((kr, tn), lambda i, j: (kd // kr, j)),
                  pl.BlockSpec((tm, tn), lambda i, j: (i, j)),
                  pl.BlockSpec((None, 1, tn), lambda i, j: (i // per_b, 0, j))],
        out_specs=pl.BlockSpec((tm, tn), lambda i, j: (i, j)),
        out_shape=jax.ShapeDtypeStruct((b * l, d), F32),
        compiler_params=_params(ARB, ARB),
        name="outproj",
    )(o_d.reshape(b * l, kd), o_r.reshape(b * l, kr), w_out, w_out, x.reshape(b * l, d),
      gate.reshape(b, 1, d)).reshape(b, l, d)


def _rope_tables(l, lc):
    row, col = jnp.meshgrid(jnp.arange(l // GRID_W), jnp.arange(GRID_W), indexing="ij")
    row = row.reshape(-1).astype(F32)
    col = col.reshape(-1).astype(F32)
    half = QK_DIM // 2
    inv_freq = ROPE_BASE ** (-jnp.arange(0, half, 2, dtype=F32) / half)
    ang_r = row[:, None] * inv_freq
    ang_c = col[:, None] * inv_freq
    ang = jnp.concatenate([ang_r, ang_r, ang_c, ang_c], axis=-1)
    cos, sin = jnp.cos(ang), jnp.sin(ang)
    lo = (jnp.arange(QK_DIM) % half) < (half // 2)
    cos = jnp.concatenate([cos, jnp.ones((lc, QK_DIM), F32)], axis=0)
    sin_lo = jnp.concatenate([jnp.where(lo, -sin, 0.0), jnp.zeros((lc, QK_DIM), F32)], axis=0)
    sin_hi = jnp.concatenate([jnp.where(lo, 0.0, sin), jnp.zeros((lc, QK_DIM), F32)], axis=0)
    return cos, sin_lo, sin_hi


def _layer(x, ctx, c, c_ctx, norm_w, ada_w, ada_b, w_in, q_norm_w, k_norm_w, lam_params, subln_w,
           dec_f, dec_b, ret_norm_w, w_out):
    b, l, d = x.shape
    lc = ctx.shape[1]
    t_all = l + lc

    rows = -(-(b + 1) // 8) * 8
    cc = jnp.zeros((rows, d), F32).at[:b].set(c).at[b].set(c_ctx)
    mod = _ada_call(cc, ada_w, ada_b)
    shift, scale, gate = mod[:b, :d], mod[:b, d:2 * d], mod[:b, 2 * d:]
    shift_c = jnp.broadcast_to(mod[b, :d], (b, d))
    scale_c = jnp.broadcast_to(mod[b, d:2 * d], (b, d))
    mods = jnp.stack([scale, shift, scale_c, shift_c], axis=1)

    h_all = _norm_call(x, ctx, norm_w, mods).reshape(b * t_all, d)

    w_in_b = w_in.astype(BF16)
    tables = _rope_tables(l, lc)
    tn = 512
    blocks = lambda lo, hi: list(range(lo // tn, hi // tn))
    qk_vecs = jnp.stack([q_norm_w, k_norm_w]).reshape(2, 1, QK_DIM)
    ret_vecs = jnp.stack([jnp.ones((QK_DIM,), F32), jnp.full((QK_DIM,), QK_DIM ** -0.5, F32)]).reshape(2, 1, QK_DIM)
    call = functools.partial(_inproj_call, h_all, w_in_b, tables, rows_per_batch=t_all)
    p_nr = call(qk_vecs, mode="normrope", col_blocks=blocks(O_DQ, O_DV), out_w=V_DIM)
    p_rope = call(ret_vecs, mode="rope", col_blocks=blocks(O_RQ, O_RV), out_w=QK_DIM)
    p_plain = call(None, mode="plain", col_blocks=blocks(O_DV, O_RQ) + blocks(O_RV, O_RG + RET_WIDTH), out_w=V_DIM)
    p_nr = p_nr.reshape(-1, b, t_all, V_DIM)
    p_rope = p_rope.reshape(-1, b, t_all, QK_DIM)
    p_plain = p_plain.reshape(-1, b, t_all, V_DIM)

    o_d = _attn_call(p_nr, p_plain, lam_params, subln_w, l)
    decs = jnp.broadcast_to(jnp.stack([dec_f, dec_b], axis=1)[:, :, None], (RET_HEADS, 2, QK_DIM)).astype(F32)
    o_r = _ret_call(p_rope, p_plain, decs, ret_norm_w, l)

    return _outproj_call(o_d, o_r, w_out.astype(BF16), x, gate)


def kernel(x, c, ctx, c_ctx, norm_w, ada_w, ada_b, w_in, diff_q_norm_w, diff_k_norm_w, diff_lambda_q1,
           diff_lambda_k1, diff_lambda_q2, diff_lambda_k2, diff_subln_w, ret_decay_fwd, ret_decay_bwd,
           ret_norm_w, w_out):
    depth = norm_w.shape[0]
    assert depth == 1, "LAM_INIT is the layer-0 value"
    for layer in range(depth):
        lam_params = jnp.stack([diff_lambda_q1[layer], diff_lambda_k1[layer],
                                diff_lambda_q2[layer], diff_lambda_k2[layer]]).astype(F32)
        x = _layer(x, ctx, c, c_ctx, norm_w[layer], ada_w[layer], ada_b[layer], w_in[layer],
                   diff_q_norm_w[layer], diff_k_norm_w[layer], lam_params, diff_subln_w[layer],
                   ret_decay_fwd[layer], ret_decay_bwd[layer], ret_norm_w[layer], w_out[layer])
    return x
```

```python
import functools
import math

import jax
import jax.numpy as jnp
from jax import lax
from jax.experimental import pallas as pl
from jax.experimental.pallas import tpu as pltpu

F32 = jnp.float32
BF16 = jnp.bfloat16

GRID_W = 64
DIFF_HEADS = 8
QK_DIM = 128
V_DIM = 256
RET_HEADS = 8
ROPE_BASE = 10000.0
EPS = 1e-6
LAM_INIT = 0.8 - 0.6 * math.exp(-0.3 * 0)

DIFF_QK_W = DIFF_HEADS * 2 * QK_DIM
DIFF_WIDTH = DIFF_HEADS * V_DIM
RET_QK_W = RET_HEADS * QK_DIM
RET_WIDTH = RET_HEADS * V_DIM
O_DQ = 0
O_DK = O_DQ + DIFF_QK_W
O_DV = O_DK + DIFF_QK_W
O_DG = O_DV + DIFF_WIDTH
O_RQ = O_DG + DIFF_WIDTH
O_RK = O_RQ + RET_QK_W
O_RV = O_RK + RET_QK_W
O_RG = O_RV + RET_WIDTH

V7X_VMEM_LIMIT_BYTES = 56 * 1024 * 1024
ARB = "arbitrary"
RET_UNROLL = 8


def _pick(n, candidates):
    for c in candidates:
        if n % c == 0:
            return c
    raise ValueError(f"no tile in {candidates} divides {n}")


def _params(*sem):
    return pltpu.CompilerParams(dimension_semantics=sem, vmem_limit_bytes=V7X_VMEM_LIMIT_BYTES)


def _ada_kernel(c_ref, w_ref, b_ref, o_ref):
    c = c_ref[...]
    a = (c * jax.nn.sigmoid(c)).astype(BF16)
    o_ref[...] = jnp.dot(a, w_ref[...].astype(BF16), preferred_element_type=F32) + b_ref[...]


def _ada_call(cc, ada_w, ada_b):
    rows, d = cc.shape
    n = ada_w.shape[1]
    tn = _pick(n, (512, 256, 128))
    return pl.pallas_call(
        _ada_kernel,
        grid=(n // tn,),
        in_specs=[pl.BlockSpec((rows, d), lambda j: (0, 0)),
                  pl.BlockSpec((d, tn), lambda j: (0, j)),
                  pl.BlockSpec((1, tn), lambda j: (0, j))],
        out_specs=pl.BlockSpec((rows, tn), lambda j: (0, j)),
        out_shape=jax.ShapeDtypeStruct((rows, n), F32),
        compiler_params=_params(ARB),
        name="ada_mod",
    )(cc, ada_w, ada_b.reshape(1, n))


def _norm_kernel(x_ref, ctx_ref, nw_ref, mod_ref, o_ref, *, n_lat):
    i = pl.program_id(1)

    def norm_mod(xv, scale, shift):
        ms = jnp.mean(xv * xv, axis=-1, keepdims=True)
        y = xv * lax.rsqrt(ms + EPS) * nw_ref[...]
        return (y * (1.0 + scale) + shift).astype(o_ref.dtype)

    @pl.when(i < n_lat)
    def _():
        o_ref[...] = norm_mod(x_ref[...], mod_ref[0:1, :], mod_ref[1:2, :])

    @pl.when(i >= n_lat)
    def _():
        o_ref[...] = norm_mod(ctx_ref[...], mod_ref[2:3, :], mod_ref[3:4, :])


def _norm_call(x, ctx, norm_w, mods):
    b, l, d = x.shape
    lc = ctx.shape[1]
    tr = _pick(math.gcd(l, lc), (256, 128, 64, 32, 16))
    n_lat, n_ctx = l // tr, lc // tr
    return pl.pallas_call(
        functools.partial(_norm_kernel, n_lat=n_lat),
        grid=(b, n_lat + n_ctx),
        in_specs=[pl.BlockSpec((None, tr, d), lambda bb, i: (bb, jnp.minimum(i, n_lat - 1), 0)),
                  pl.BlockSpec((None, tr, d), lambda bb, i: (bb, jnp.maximum(i - n_lat, 0), 0)),
                  pl.BlockSpec((1, d), lambda bb, i: (0, 0)),
                  pl.BlockSpec((None, 4, d), lambda bb, i: (bb, 0, 0))],
        out_specs=pl.BlockSpec((None, tr, d), lambda bb, i: (bb, i, 0)),
        out_shape=jax.ShapeDtypeStruct((b, l + lc, d), BF16),
        compiler_params=_params(ARB, ARB),
        name="norm_mod",
    )(x, ctx, norm_w.reshape(1, d), mods)


def _rope(xh, cos, sin_lo, sin_hi):
    return xh * cos + pltpu.roll(xh, 96, 1) * sin_lo + pltpu.roll(xh, 32, 1) * sin_hi


def _inproj_plain_kernel(cb_ref, a_ref, w_ref, o_ref, *, out_w):
    del cb_ref
    acc = jnp.dot(a_ref[...], w_ref[...], preferred_element_type=F32)
    for c in range(acc.shape[1] // out_w):
        o_ref[c] = acc[:, c * out_w:(c + 1) * out_w].astype(o_ref.dtype)


def _inproj_rope_kernel(cb_ref, a_ref, w_ref, cos_ref, slo_ref, shi_ref, vec_ref, o_ref, acc_sc, *, norm, out_w):
    del cb_ref

    @pl.when(pl.program_id(0) == 0)
    def _():
        acc_sc[...] = jnp.zeros_like(acc_sc)

    cos, slo, shi, vec = cos_ref[...], slo_ref[...], shi_ref[...], vec_ref[...]
    per = out_w // QK_DIM
    for t in range(acc_sc.shape[1] // QK_DIM):
        xh = acc_sc[:, t * QK_DIM:(t + 1) * QK_DIM]
        if norm:
            xh = xh * lax.rsqrt(jnp.mean(xh * xh, axis=-1, keepdims=True) + EPS) * vec
            r = _rope(xh, cos, slo, shi)
        else:
            r = _rope(xh, cos, slo, shi) * vec
        lane0 = (t % per) * QK_DIM
        o_ref[t // per, :, lane0:lane0 + QK_DIM] = r.astype(o_ref.dtype)
    acc_sc[...] = jnp.dot(a_ref[...], w_ref[...], preferred_element_type=F32)


def _inproj_call(a2d, w, tables, vecs, *, mode, col_blocks, out_w, rows_per_batch):
    m, d = a2d.shape
    tn = 512
    tm = _pick(rows_per_batch, (768, 512, 384, 256, 128))
    n_i, n_j = m // tm, len(col_blocks)
    n_out = n_j * tn // out_w
    col_tab = jnp.asarray(col_blocks, jnp.int32)
    out_shape = jax.ShapeDtypeStruct((n_out, m, out_w), BF16)
    if mode == "plain":
        return pl.pallas_call(
            functools.partial(_inproj_plain_kernel, out_w=out_w),
            grid_spec=pltpu.PrefetchScalarGridSpec(
                num_scalar_prefetch=1,
                grid=(n_i, n_j),
                in_specs=[pl.BlockSpec((tm, d), lambda i, j, cb: (i, 0)),
                          pl.BlockSpec((d, tn), lambda i, j, cb: (0, cb[j]))],
                out_specs=pl.BlockSpec((tn // out_w, tm, out_w), lambda i, j, cb: (j, i, 0)),
            ),
            out_shape=out_shape,
            compiler_params=_params(ARB, ARB),
            name="inproj_plain",
        )(col_tab, a2d, w)

    steps = n_i * n_j
    t_blocks = rows_per_batch // tm
    j_per_vec = n_j // vecs.shape[0]
    cur = lambda s: jnp.minimum(s, steps - 1)
    prev = lambda s: jnp.maximum(s - 1, 0)
    tab_spec = pl.BlockSpec((tm, QK_DIM), lambda s, cb: ((prev(s) // n_j) % t_blocks, 0))
    return pl.pallas_call(
        functools.partial(_inproj_rope_kernel, norm=(mode == "normrope"), out_w=out_w),
        grid_spec=pltpu.PrefetchScalarGridSpec(
            num_scalar_prefetch=1,
            grid=(steps + 1,),
            in_specs=[pl.BlockSpec((tm, d), lambda s, cb: (cur(s) // n_j, 0)),
                      pl.BlockSpec((d, tn), lambda s, cb: (0, cb[cur(s) % n_j])),
                      tab_spec, tab_spec, tab_spec,
                      pl.BlockSpec((None, 1, QK_DIM), lambda s, cb: ((prev(s) % n_j) // j_per_vec, 0, 0))],
            out_specs=pl.BlockSpec((tn // out_w, tm, out_w), lambda s, cb: (prev(s) % n_j, prev(s) // n_j, 0)),
            scratch_shapes=[pltpu.VMEM((tm, tn), F32)],
        ),
        out_shape=out_shape,
        compiler_params=_params(ARB),
        name="inproj_" + mode,
    )(col_tab, a2d, w, *tables, vecs)


def _attn_kernel(lamp_ref, q_ref, k_ref, v_ref, g_ref, sw_ref, o_ref, vt_sc, s_sc, p_sc, acc_sc, *, tk):
    t_all = k_ref.shape[0]
    tq = q_ref.shape[0]
    n = t_all // tk
    c_exp = (QK_DIM ** -0.5) * math.log2(math.e)
    nt = (((1,), (1,)), ((), ()))

    @pl.when(pl.program_id(2) == 0)
    def _():
        def transpose_chunk(c, carry):
            rows = pl.ds(pl.multiple_of(c * tk, tk), tk)
            vt_sc[c] = v_ref[rows, :].astype(F32).T.astype(BF16)
            return carry
        lax.fori_loop(0, n, transpose_chunk, 0)

    qt = q_ref[...].astype(F32).T.astype(BF16)
    qts = (qt[:QK_DIM], qt[QK_DIM:])
    acc_sc[...] = jnp.zeros_like(acc_sc)

    def stage_qk(c):
        rows = pl.ds(pl.multiple_of(c * tk, tk), tk)
        kc = k_ref[rows, :]
        cms = []
        for sub in range(2):
            s = jnp.dot(kc[:, sub * QK_DIM:(sub + 1) * QK_DIM], qts[sub],
                        preferred_element_type=F32)
            s_sc[sub] = s
            cms.append(jnp.max(s, axis=0, keepdims=True))
        return tuple(cms)

    def stage_softmax(cms, ms, ls):
        new_m, new_l, alphas = [], [], []
        for sub in range(2):
            m_new = jnp.maximum(ms[sub], cms[sub])
            alpha = jnp.exp2((ms[sub] - m_new) * c_exp)
            p = jnp.exp2(s_sc[sub] * c_exp - m_new * c_exp)
            new_l.append(alpha * ls[sub] + jnp.sum(p.reshape(tk // 8, 8, tq), axis=0))
            p_sc[sub] = p.astype(BF16)
            new_m.append(m_new)
            alphas.append(alpha)
        return tuple(new_m), tuple(new_l), tuple(alphas)

    def stage_pv(c, alphas):
        vt = vt_sc[c]
        for sub in range(2):
            acc_sc[sub] = alphas[sub] * acc_sc[sub] + jnp.dot(vt, p_sc[sub], preferred_element_type=F32)

    neg = jnp.full((1, tq), -jnp.inf, F32)
    ms = (neg, neg)
    ls = (jnp.zeros((8, tq), F32),) * 2
    cms = stage_qk(0)
    ms, ls, alphas = stage_softmax(cms, ms, ls)
    cms = stage_qk(1)

    def tick(t, carry):
        cms, ms, ls, alphas = carry
        stage_pv(t - 2, alphas)
        ms, ls, alphas = stage_softmax(cms, ms, ls)
        cms = stage_qk(t)
        return cms, ms, ls, alphas

    cms, ms, ls, alphas = lax.fori_loop(2, n, tick, (cms, ms, ls, alphas))
    stage_pv(n - 2, alphas)
    ms, ls, alphas = stage_softmax(cms, ms, ls)
    stage_pv(n - 1, alphas)

    lp = lamp_ref[...]
    lam = (jnp.exp(jnp.sum(lp[0:1] * lp[1:2], axis=1, keepdims=True))
           - jnp.exp(jnp.sum(lp[2:3] * lp[3:4], axis=1, keepdims=True)) + LAM_INIT)
    inv1 = 1.0 / jnp.sum(ls[0], axis=0, keepdims=True)
    inv2 = lam / jnp.sum(ls[1], axis=0, keepdims=True)
    o = (acc_sc[0] * inv1 - acc_sc[1] * inv2).T
    y = o * lax.rsqrt(jnp.mean(o * o, axis=-1, keepdims=True) + EPS) * sw_ref[...] * (1.0 - LAM_INIT)
    g = g_ref[...].astype(F32)
    o_ref[...] = (y * (g * jax.nn.sigmoid(g))).astype(o_ref.dtype)


def _attn_call(p_nr, p_plain, lam_params, subln_w, l):
    _, b, t_all, _ = p_nr.shape
    tq = _pick(l, (512, 256, 128))
    tk = _pick(t_all, (768, 256))
    n = t_all // tk
    assert n >= 2, "the three-stage skew needs at least two key chunks"
    h = DIFF_HEADS
    return pl.pallas_call(
        functools.partial(_attn_kernel, tk=tk),
        grid=(b, h, l // tq),
        in_specs=[pl.BlockSpec((4, QK_DIM), lambda bb, hh, i: (0, 0)),
                  pl.BlockSpec((None, None, tq, V_DIM), lambda bb, hh, i: (hh, bb, i, 0)),
                  pl.BlockSpec((None, None, t_all, V_DIM), lambda bb, hh, i: (h + hh, bb, 0, 0)),
                  pl.BlockSpec((None, None, t_all, V_DIM), lambda bb, hh, i: (hh, bb, 0, 0)),
                  pl.BlockSpec((None, None, tq, V_DIM), lambda bb, hh, i: (h + hh, bb, i, 0)),
                  pl.BlockSpec((1, V_DIM), lambda bb, hh, i: (0, 0))],
        out_specs=pl.BlockSpec((None, tq, V_DIM), lambda bb, hh, i: (bb, i, hh)),
        out_shape=jax.ShapeDtypeStruct((b, l, DIFF_WIDTH), BF16),
        scratch_shapes=[pltpu.VMEM((n, V_DIM, tk), BF16),
                        pltpu.VMEM((2, tk, tq), F32),
                        pltpu.VMEM((2, tk, tq), BF16),
                        pltpu.VMEM((2, V_DIM, tq), F32)],
        compiler_params=_params(ARB, ARB, ARB),
        name="diff_attn",
    )(lam_params, p_nr, p_nr, p_plain, p_plain, subln_w.reshape(1, V_DIM))


def _ret_kernel(dec_ref, q_ref, k_ref, v_ref, g_ref, nw_ref, o_ref, ob_sc, *, chunk):
    l = q_ref.shape[0]
    lc = k_ref.shape[0] - l
    nc = l // chunk
    cdim = (((1,), (1,)), ((), ()))
    rdim = (((0,), (0,)), ((), ()))

    dec = dec_ref[...]
    lg = jnp.log1p(-jnp.exp2(-dec))
    lg_f, lg_b = lg[0:1, 0:1], lg[1:2, 0:1]

    def col_iota(n):
        return lax.broadcasted_iota(jnp.int32, (n, 1), 0).astype(F32)

    ic = col_iota(chunk)
    xi_f = jnp.exp(lg_f * (ic + 1.0))
    xi_b = jnp.exp(lg_b * (chunk - ic))
    zeta_f = jnp.exp(lg_f * (chunk - 1.0 - ic))
    zeta_b = jnp.exp(lg_b * ic)
    gc_f = jnp.exp(lg_f * chunk)
    gc_b = jnp.exp(lg_b * chunk)
    ri = lax.broadcasted_iota(jnp.int32, (chunk, chunk), 0)
    ci = lax.broadcasted_iota(jnp.int32, (chunk, chunk), 1)
    dist = (ri - ci).astype(F32)
    dmat = jnp.where(ri >= ci, jnp.exp(lg_f * jnp.maximum(dist, 0.0)), jnp.exp(lg_b * jnp.maximum(-dist, 0.0)))

    im = col_iota(lc)
    kctx = k_ref[pl.ds(l, lc), :].astype(F32)
    vctx = v_ref[pl.ds(l, lc), :]
    r0_f = lax.dot_general((kctx * jnp.exp(lg_f * (lc - 1.0 - im))).astype(BF16), vctx, rdim,
                           preferred_element_type=F32)
    r0_b = lax.dot_general((kctx * jnp.exp(lg_b * im)).astype(BF16), vctx, rdim, preferred_element_type=F32)

    def chunk_kv(rows, zeta):
        kz = (k_ref[rows, :].astype(F32) * zeta).astype(BF16)
        return lax.dot_general(kz, v_ref[rows, :], rdim, preferred_element_type=F32)

    def bwd_body(j, s_b):
        c = nc - 1 - j
        rows = pl.ds(pl.multiple_of(c * chunk, chunk), chunk)
        cross = jnp.dot(q_ref[rows, :], s_b.astype(BF16), preferred_element_type=F32)
        ob_sc[rows, :] = cross * xi_b
        return gc_b * s_b + chunk_kv(rows, zeta_b)

    lax.fori_loop(0, nc, bwd_body, r0_b, unroll=RET_UNROLL)

    def fwd_body(c, s_f):
        rows = pl.ds(pl.multiple_of(c * chunk, chunk), chunk)
        qc = q_ref[rows, :]
        scores = lax.dot_general(qc, k_ref[rows, :], cdim, preferred_element_type=F32) * dmat
        o = jnp.dot(scores.astype(BF16), v_ref[rows, :], preferred_element_type=F32)
        o = o + jnp.dot(qc, s_f.astype(BF16), preferred_element_type=F32) * xi_f + ob_sc[rows, :]
        y = o * lax.rsqrt(jnp.mean(o * o, axis=-1, keepdims=True) + EPS) * nw_ref[...]
        g = g_ref[rows, :].astype(F32)
        o_ref[rows, :] = (y * (g * jax.nn.sigmoid(g))).astype(o_ref.dtype)
        return gc_f * s_f + chunk_kv(rows, zeta_f)

    lax.fori_loop(0, nc, fwd_body, r0_f, unroll=RET_UNROLL)


def _ret_call(p_rope, p_plain, decs, ret_norm_w, l):
    _, b, t_all, _ = p_rope.shape
    h = RET_HEADS
    chunk = _pick(l, (256, 128))
    return pl.pallas_call(
        functools.partial(_ret_kernel, chunk=chunk),
        grid=(b, h),
        in_specs=[pl.BlockSpec((None, 2, QK_DIM), lambda bb, hh: (hh, 0, 0)),
                  pl.BlockSpec((None, None, l, QK_DIM), lambda bb, hh: (hh, bb, 0, 0)),
                  pl.BlockSpec((None, None, t_all, QK_DIM), lambda bb, hh: (h + hh, bb, 0, 0)),
                  pl.BlockSpec((None, None, t_all, V_DIM), lambda bb, hh: (2 * DIFF_HEADS + hh, bb, 0, 0)),
                  pl.BlockSpec((None, None, l, V_DIM), lambda bb, hh: (2 * DIFF_HEADS + h + hh, bb, 0, 0)),
                  pl.BlockSpec((1, V_DIM), lambda bb, hh: (0, 0))],
        out_specs=pl.BlockSpec((None, l, V_DIM), lambda bb, hh: (bb, 0, hh)),
        out_shape=jax.ShapeDtypeStruct((b, l, RET_WIDTH), BF16),
        scratch_shapes=[pltpu.VMEM((l, V_DIM), F32)],
        compiler_params=_params(ARB, ARB),
        name="retention",
    )(decs, p_rope, p_rope, p_plain, p_plain, ret_norm_w.reshape(1, V_DIM))


def _outproj_kernel(ad_ref, ar_ref, wd_ref, wr_ref, x_ref, gate_ref, o_ref):
    y = jnp.dot(ad_ref[...], wd_ref[...], preferred_element_type=F32)
    y = y + jnp.dot(ar_ref[...], wr_ref[...], preferred_element_type=F32)
    o_ref[...] = x_ref[...] + gate_ref[...] * y


def _outproj_call(o_d, o_r, w_out, x, gate):
    b, l, d = x.shape
    kd, kr = o_d.shape[-1], o_r.shape[-1]
    tm = _pick(l, (1024, 512, 256, 128))
    tn = _pick(d, (512, 256, 128))
    per_b = l // tm
    return pl.pallas_call(
        _outproj_kernel,
        grid=(b * per_b, d // tn),
        in_specs=[pl.BlockSpec((tm, kd), lambda i, j: (i, 0)),
                  pl.BlockSpec((tm, kr), lambda i, j: (i, 0)),
                  pl.BlockSpec((kd, tn), lambda i, j: (0, j)),
                  pl.BlockSpec((kr, tn), lambda i, j: (kd // kr, j)),
                  pl.BlockSpec((tm, tn), lambda i, j: (i, j)),
                  pl.BlockSpec((None, 1, tn), lambda i, j: (i // per_b, 0, j))],
        out_specs=pl.BlockSpec((tm, tn), lambda i, j: (i, j)),
        out_shape=jax.ShapeDtypeStruct((b * l, d), F32),
        compiler_params=_params(ARB, ARB),
        name="outproj",
    )(o_d.reshape(b * l, kd), o_r.reshape(b * l, kr), w_out, w_out, x.reshape(b * l, d),
      gate.reshape(b, 1, d)).reshape(b, l, d)


def _rope_tables(l, lc):
    row, col = jnp.meshgrid(jnp.arange(l // GRID_W), jnp.arange(GRID_W), indexing="ij")
    row = row.reshape(-1).astype(F32)
    col = col.reshape(-1).astype(F32)
    half = QK_DIM // 2
    inv_freq = ROPE_BASE ** (-jnp.arange(0, half, 2, dtype=F32) / half)
    ang_r = row[:, None] * inv_freq
    ang_c = col[:, None] * inv_freq
    ang = jnp.concatenate([ang_r, ang_r, ang_c, ang_c], axis=-1)
    cos, sin = jnp.cos(ang), jnp.sin(ang)
    lo = (jnp.arange(QK_DIM) % half) < (half // 2)
    cos = jnp.concatenate([cos, jnp.ones((lc, QK_DIM), F32)], axis=0)
    sin_lo = jnp.concatenate([jnp.where(lo, -sin, 0.0), jnp.zeros((lc, QK_DIM), F32)], axis=0)
    sin_hi = jnp.concatenate([jnp.where(lo, 0.0, sin), jnp.zeros((lc, QK_DIM), F32)], axis=0)
    return cos, sin_lo, sin_hi


def _layer(x, ctx, c, c_ctx, norm_w, ada_w, ada_b, w_in, q_norm_w, k_norm_w, lam_params, subln_w,
           dec_f, dec_b, ret_norm_w, w_out):
    b, l, d = x.shape
    lc = ctx.shape[1]
    t_all = l + lc

    rows = -(-(b + 1) // 8) * 8
    cc = jnp.zeros((rows, d), F32).at[:b].set(c).at[b].set(c_ctx)
    mod = _ada_call(cc, ada_w, ada_b)
    shift, scale, gate = mod[:b, :d], mod[:b, d:2 * d], mod[:b, 2 * d:]
    shift_c = jnp.broadcast_to(mod[b, :d], (b, d))
    scale_c = jnp.broadcast_to(mod[b, d:2 * d], (b, d))
    mods = jnp.stack([scale, shift, scale_c, shift_c], axis=1)

    h_all = _norm_call(x, ctx, norm_w, mods).reshape(b * t_all, d)

    w_in_b = w_in.astype(BF16)
    tables = _rope_tables(l, lc)
    tn = 512
    blocks = lambda lo, hi: list(range(lo // tn, hi // tn))
    qk_vecs = jnp.stack([q_norm_w, k_norm_w]).reshape(2, 1, QK_DIM)
    ret_vecs = jnp.stack([jnp.ones((QK_DIM,), F32), jnp.full((QK_DIM,), QK_DIM ** -0.5, F32)]).reshape(2, 1, QK_DIM)
    call = functools.partial(_inproj_call, h_all, w_in_b, tables, rows_per_batch=t_all)
    p_nr = call(qk_vecs, mode="normrope", col_blocks=blocks(O_DQ, O_DV), out_w=V_DIM)
    p_rope = call(ret_vecs, mode="rope", col_blocks=blocks(O_RQ, O_RV), out_w=QK_DIM)
    p_plain = call(None, mode="plain", col_blocks=blocks(O_DV, O_RQ) + blocks(O_RV, O_RG + RET_WIDTH), out_w=V_DIM)
    p_nr = p_nr.reshape(-1, b, t_all, V_DIM)
    p_rope = p_rope.reshape(-1, b, t_all, QK_DIM)
    p_plain = p_plain.reshape(-1, b, t_all, V_DIM)

    o_d = _attn_call(p_nr, p_plain, lam_params, subln_w, l)
    decs = jnp.broadcast_to(jnp.stack([dec_f, dec_b], axis=1)[:, :, None], (RET_HEADS, 2, QK_DIM)).astype(F32)
    o_r = _ret_call(p_rope, p_plain, decs, ret_norm_w, l)

    return _outproj_call(o_d, o_r, w_out.astype(BF16), x, gate)


def kernel(x, c, ctx, c_ctx, norm_w, ada_w, ada_b, w_in, diff_q_norm_w, diff_k_norm_w, diff_lambda_q1,
           diff_lambda_k1, diff_lambda_q2, diff_lambda_k2, diff_subln_w, ret_decay_fwd, ret_decay_bwd,
           ret_norm_w, w_out):
    depth = norm_w.shape[0]
    assert depth == 1, "LAM_INIT is the layer-0 value"
    for layer in range(depth):
        lam_params = jnp.stack([diff_lambda_q1[layer], diff_lambda_k1[layer],
                                diff_lambda_q2[layer], diff_lambda_k2[layer]]).astype(F32)
        x = _layer(x, ctx, c, c_ctx, norm_w[layer], ada_w[layer], ada_b[layer], w_in[layer],
                   diff_q_norm_w[layer], diff_k_norm_w[layer], lam_params, diff_subln_w[layer],
                   ret_decay_fwd[layer], ret_decay_bwd[layer], ret_norm_w[layer], w_out[layer])
    return x
```

```python
import functools
import math

import jax
import jax.numpy as jnp
from jax import lax
from jax.experimental import pallas as pl
from jax.experimental.pallas import tpu as pltpu

F32 = jnp.float32
BF16 = jnp.bfloat16

GRID_W = 64
DIFF_HEADS = 8
QK_DIM = 128
V_DIM = 256
RET_HEADS = 8
ROPE_BASE = 10000.0
EPS = 1e-6
LAM_INIT = 0.8 - 0.6 * math.exp(-0.3 * 0)

DIFF_QK_W = DIFF_HEADS * 2 * QK_DIM
DIFF_WIDTH = DIFF_HEADS * V_DIM
RET_QK_W = RET_HEADS * QK_DIM
RET_WIDTH = RET_HEADS * V_DIM
O_DQ = 0
O_DK = O_DQ + DIFF_QK_W
O_DV = O_DK + DIFF_QK_W
O_DG = O_DV + DIFF_WIDTH
O_RQ = O_DG + DIFF_WIDTH
O_RK = O_RQ + RET_QK_W
O_RV = O_RK + RET_QK_W
O_RG = O_RV + RET_WIDTH

V7X_VMEM_LIMIT_BYTES = 56 * 1024 * 1024
ARB = "arbitrary"
RET_UNROLL = 8


def _pick(n, candidates):
    for c in candidates:
        if n % c == 0:
            return c
    raise ValueError(f"no tile in {candidates} divides {n}")


def _params(*sem):
    return pltpu.CompilerParams(dimension_semantics=sem, vmem_limit_bytes=V7X_VMEM_LIMIT_BYTES)


def _ada_kernel(c_ref, w_ref, b_ref, o_ref):
    c = c_ref[...]
    a = (c * jax.nn.sigmoid(c)).astype(BF16)
    o_ref[...] = jnp.dot(a, w_ref[...].astype(BF16), preferred_element_type=F32) + b_ref[...]


def _ada_call(cc, ada_w, ada_b):
    rows, d = cc.shape
    n = ada_w.shape[1]
    tn = _pick(n, (512, 256, 128))
    return pl.pallas_call(
        _ada_kernel,
        grid=(n // tn,),
        in_specs=[pl.BlockSpec((rows, d), lambda j: (0, 0)),
                  pl.BlockSpec((d, tn), lambda j: (0, j)),
                  pl.BlockSpec((1, tn), lambda j: (0, j))],
        out_specs=pl.BlockSpec((rows, tn), lambda j: (0, j)),
        out_shape=jax.ShapeDtypeStruct((rows, n), F32),
        compiler_params=_params(ARB),
        name="ada_mod",
    )(cc, ada_w, ada_b.reshape(1, n))


def _norm_kernel(x_ref, ctx_ref, nw_ref, mod_ref, o_ref, *, n_lat):
    i = pl.program_id(1)

    def norm_mod(xv, scale, shift):
        ms = jnp.mean(xv * xv, axis=-1, keepdims=True)
        y = xv * lax.rsqrt(ms + EPS) * nw_ref[...]
        return (y * (1.0 + scale) + shift).astype(o_ref.dtype)

    @pl.when(i < n_lat)
    def _():
        o_ref[...] = norm_mod(x_ref[...], mod_ref[0:1, :], mod_ref[1:2, :])

    @pl.when(i >= n_lat)
    def _():
        o_ref[...] = norm_mod(ctx_ref[...], mod_ref[2:3, :], mod_ref[3:4, :])


def _norm_call(x, ctx, norm_w, mods):
    b, l, d = x.shape
    lc = ctx.shape[1]
    tr = _pick(math.gcd(l, lc), (256, 128, 64, 32, 16))
    n_lat, n_ctx = l // tr, lc // tr
    return pl.pallas_call(
        functools.partial(_norm_kernel, n_lat=n_lat),
        grid=(b, n_lat + n_ctx),
        in_specs=[pl.BlockSpec((None, tr, d), lambda bb, i: (bb, jnp.minimum(i, n_lat - 1), 0)),
                  pl.BlockSpec((None, tr, d), lambda bb, i: (bb, jnp.maximum(i - n_lat, 0), 0)),
                  pl.BlockSpec((1, d), lambda bb, i: (0, 0)),
                  pl.BlockSpec((None, 4, d), lambda bb, i: (bb, 0, 0))],
        out_specs=pl.BlockSpec((None, tr, d), lambda bb, i: (bb, i, 0)),
        out_shape=jax.ShapeDtypeStruct((b, l + lc, d), BF16),
        compiler_params=_params(ARB, ARB),
        name="norm_mod",
    )(x, ctx, norm_w.reshape(1, d), mods)


def _rope(xh, cos, sin_lo, sin_hi):
    return xh * cos + pltpu.roll(xh, 96, 1) * sin_lo + pltpu.roll(xh, 32, 1) * sin_hi


def _inproj_plain_kernel(cb_ref, a_ref, w_ref, o_ref, *, out_w):
    del cb_ref
    acc = jnp.dot(a_ref[...], w_ref[...], preferred_element_type=F32)
    for c in range(acc.shape[1] // out_w):
        o_ref[c] = acc[:, c * out_w:(c + 1) * out_w].astype(o_ref.dtype)


def _inproj_rope_kernel(cb_ref, a_ref, w_ref, cos_ref, slo_ref, shi_ref, vec_ref, o_ref, acc_sc, *, norm, out_w):
    del cb_ref

    @pl.when(pl.program_id(0) == 0)
    def _():
        acc_sc[...] = jnp.zeros_like(acc_sc)

    cos, slo, shi, vec = cos_ref[...], slo_ref[...], shi_ref[...], vec_ref[...]
    per = out_w // QK_DIM
    for t in range(acc_sc.shape[1] // QK_DIM):
        xh = acc_sc[:, t * QK_DIM:(t + 1) * QK_DIM]
        if norm:
            xh = xh * lax.rsqrt(jnp.mean(xh * xh, axis=-1, keepdims=True) + EPS) * vec
            r = _rope(xh, cos, slo, shi)
        else:
            r = _rope(xh, cos, slo, shi) * vec
        lane0 = (t % per) * QK_DIM
        o_ref[t // per, :, lane0:lane0 + QK_DIM] = r.astype(o_ref.dtype)
    acc_sc[...] = jnp.dot(a_ref[...], w_ref[...], preferred_element_type=F32)


def _inproj_call(a2d, w, tables, vecs, *, mode, col_blocks, out_w, rows_per_batch):
    m, d = a2d.shape
    tn = 512
    tm = _pick(rows_per_batch, (768, 512, 384, 256, 128))
    n_i, n_j = m // tm, len(col_blocks)
    n_out = n_j * tn // out_w
    col_tab = jnp.asarray(col_blocks, jnp.int32)
    out_shape = jax.ShapeDtypeStruct((n_out, m, out_w), BF16)
    if mode == "plain":
        return pl.pallas_call(
            functools.partial(_inproj_plain_kernel, out_w=out_w),
            grid_spec=pltpu.PrefetchScalarGridSpec(
                num_scalar_prefetch=1,
                grid=(n_i, n_j),
                in_specs=[pl.BlockSpec((tm, d), lambda i, j, cb: (i, 0)),
                          pl.BlockSpec((d, tn), lambda i, j, cb: (0, cb[j]))],
                out_specs=pl.BlockSpec((tn // out_w, tm, out_w), lambda i, j, cb: (j, i, 0)),
            ),
            out_shape=out_shape,
            compiler_params=_params(ARB, ARB),
            name="inproj_plain",
        )(col_tab, a2d, w)

    steps = n_i * n_j
    t_blocks = rows_per_batch // tm
    j_per_vec = n_j // vecs.shape[0]
    cur = lambda s: jnp.minimum(s, steps - 1)
    prev = lambda s: jnp.maximum(s - 1, 0)
    tab_spec = pl.BlockSpec((tm, QK_DIM), lambda s, cb: ((prev(s) // n_j) % t_blocks, 0))
    return pl.pallas_call(
        functools.partial(_inproj_rope_kernel, norm=(mode == "normrope"), out_w=out_w),
        grid_spec=pltpu.PrefetchScalarGridSpec(
            num_scalar_prefetch=1,
            grid=(steps + 1,),
            in_specs=[pl.BlockSpec((tm, d), lambda s, cb: (cur(s) // n_j, 0)),
                      pl.BlockSpec((d, tn), lambda s, cb: (0, cb[cur(s) % n_j])),
                      tab_spec, tab_spec, tab_spec,
                      pl.BlockSpec((None, 1, QK_DIM), lambda s, cb: ((prev(s) % n_j) // j_per_vec, 0, 0))],
            out_specs=pl.BlockSpec((tn // out_w, tm, out_w), lambda s, cb: (prev(s) % n_j, prev(s) // n_j, 0)),
            scratch_shapes=[pltpu.VMEM((tm, tn), F32)],
        ),
        out_shape=out_shape,
        compiler_params=_params(ARB),
        name="inproj_" + mode,
    )(col_tab, a2d, w, *tables, vecs)


def _attn_kernel(lamp_ref, q_ref, k_ref, v_ref, g_ref, sw_ref, o_ref, vt_sc, s_sc, p_sc, acc_sc, *, tk):
    t_all = k_ref.shape[0]
    tq = q_ref.shape[0]
    n = t_all // tk
    c_exp = (QK_DIM ** -0.5) * math.log2(math.e)
    nt = (((1,), (1,)), ((), ()))

    @pl.when(pl.program_id(2) == 0)
    def _():
        def transpose_chunk(c, carry):
            rows = pl.ds(pl.multiple_of(c * tk, tk), tk)
            vt_sc[c] = v_ref[rows, :].astype(F32).T.astype(BF16)
            return carry
        lax.fori_loop(0, n, transpose_chunk, 0)

    qt = q_ref[...].astype(F32).T.astype(BF16)
    qts = (qt[:QK_DIM], qt[QK_DIM:])
    acc_sc[...] = jnp.zeros_like(acc_sc)

    def stage_qk(c):
        rows = pl.ds(pl.multiple_of(c * tk, tk), tk)
        kc = k_ref[rows, :]
        cms = []
        for sub in range(2):
            s = jnp.dot(kc[:, sub * QK_DIM:(sub + 1) * QK_DIM], qts[sub],
                        preferred_element_type=F32)
            s_sc[sub] = s
            cms.append(jnp.max(s, axis=0, keepdims=True))
        return tuple(cms)

    def stage_softmax(cms, ms, ls):
        new_m, new_l, alphas = [], [], []
        for sub in range(2):
            m_new = jnp.maximum(ms[sub], cms[sub])
            alpha = jnp.exp2((ms[sub] - m_new) * c_exp)
            p = jnp.exp2(s_sc[sub] * c_exp - m_new * c_exp)
            new_l.append(alpha * ls[sub] + jnp.sum(p.reshape(tk // 8, 8, tq), axis=0))
            p_sc[sub] = p.astype(BF16)
            new_m.append(m_new)
            alphas.append(alpha)
        return tuple(new_m), tuple(new_l), tuple(alphas)

    def stage_pv(c, alphas):
        vt = vt_sc[c]
        for sub in range(2):
            acc_sc[sub] = alphas[sub] * acc_sc[sub] + jnp.dot(vt, p_sc[sub], preferred_element_type=F32)

    neg = jnp.full((1, tq), -jnp.inf, F32)
    ms = (neg, neg)
    ls = (jnp.zeros((8, tq), F32),) * 2
    cms = stage_qk(0)
    ms, ls, alphas = stage_softmax(cms, ms, ls)
    cms = stage_qk(1)

    def tick(t, carry):
        cms, ms, ls, alphas = carry
        stage_pv(t - 2, alphas)
        ms, ls, alphas = stage_softmax(cms, ms, ls)
        cms = stage_qk(t)
        return cms, ms, ls, alphas

    cms, ms, ls, alphas = lax.fori_loop(2, n, tick, (cms, ms, ls, alphas))
    stage_pv(n - 2, alphas)
    ms, ls, alphas = stage_softmax(cms, ms, ls)
    stage_pv(n - 1, alphas)

    lp = lamp_ref[...]
    lam = (jnp.exp(jnp.sum(lp[0:1] * lp[1:2], axis=1, keepdims=True))
           - jnp.exp(jnp.sum(lp[2:3] * lp[3:4], axis=1, keepdims=True)) + LAM_INIT)
    inv1 = 1.0 / jnp.sum(ls[0], axis=0, keepdims=True)
    inv2 = lam / jnp.sum(ls[1], axis=0, keepdims=True)
    o = (acc_sc[0] * inv1 - acc_sc[1] * inv2).T
    y = o * lax.rsqrt(jnp.mean(o * o, axis=-1, keepdims=True) + EPS) * sw_ref[...] * (1.0 - LAM_INIT)
    g = g_ref[...].astype(F32)
    o_ref[...] = (y * (g * jax.nn.sigmoid(g))).astype(o_ref.dtype)


def _attn_call(p_nr, p_plain, lam_params, subln_w, l):
    _, b, t_all, _ = p_nr.shape
    tq = _pick(l, (1024, 512, 256, 128))
    tk = _pick(t_all, (768, 256))
    n = t_all // tk
    assert n >= 2, "the three-stage skew needs at least two key chunks"
    h = DIFF_HEADS
    return pl.pallas_call(
        functools.partial(_attn_kernel, tk=tk),
        grid=(b, h, l // tq),
        in_specs=[pl.BlockSpec((4, QK_DIM), lambda bb, hh, i: (0, 0)),
                  pl.BlockSpec((None, None, tq, V_DIM), lambda bb, hh, i: (hh, bb, i, 0)),
                  pl.BlockSpec((None, None, t_all, V_DIM), lambda bb, hh, i: (h + hh, bb, 0, 0)),
                  pl.BlockSpec((None, None, t_all, V_DIM), lambda bb, hh, i: (hh, bb, 0, 0)),
                  pl.BlockSpec((None, None, tq, V_DIM), lambda bb, hh, i: (h + hh, bb, i, 0)),
                  pl.BlockSpec((1, V_DIM), lambda bb, hh, i: (0, 0))],
        out_specs=pl.BlockSpec((None, tq, V_DIM), lambda bb, hh, i: (bb, i, hh)),
        out_shape=jax.ShapeDtypeStruct((b, l, DIFF_WIDTH), BF16),
        scratch_shapes=[pltpu.VMEM((n, V_DIM, tk), BF16),
                        pltpu.VMEM((2, tk, tq), F32),
                        pltpu.VMEM((2, tk, tq), BF16),
                        pltpu.VMEM((2, V_DIM, tq), F32)],
        compiler_params=_params(ARB, ARB, ARB),
        name="diff_attn",
    )(lam_params, p_nr, p_nr, p_plain, p_plain, subln_w.reshape(1, V_DIM))


def _ret_kernel(dec_ref, q_ref, k_ref, v_ref, g_ref, nw_ref, o_ref, ob_sc, *, chunk):
    l = q_ref.shape[0]
    lc = k_ref.shape[0] - l
    nc = l // chunk
    cdim = (((1,), (1,)), ((), ()))
    rdim = (((0,), (0,)), ((), ()))

    dec = dec_ref[...]
    lg = jnp.log1p(-jnp.exp2(-dec))
    lg_f, lg_b = lg[0:1, 0:1], lg[1:2, 0:1]

    def col_iota(n):
        return lax.broadcasted_iota(jnp.int32, (n, 1), 0).astype(F32)

    ic = col_iota(chunk)
    xi_f = jnp.exp(lg_f * (ic + 1.0))
    xi_b = jnp.exp(lg_b * (chunk - ic))
    zeta_f = jnp.exp(lg_f * (chunk - 1.0 - ic))
    zeta_b = jnp.exp(lg_b * ic)
    gc_f = jnp.exp(lg_f * chunk)
    gc_b = jnp.exp(lg_b * chunk)
    ri = lax.broadcasted_iota(jnp.int32, (chunk, chunk), 0)
    ci = lax.broadcasted_iota(jnp.int32, (chunk, chunk), 1)
    dist = (ri - ci).astype(F32)
    dmat = jnp.where(ri >= ci, jnp.exp(lg_f * jnp.maximum(dist, 0.0)), jnp.exp(lg_b * jnp.maximum(-dist, 0.0)))

    im = col_iota(lc)
    kctx = k_ref[pl.ds(l, lc), :].astype(F32)
    vctx = v_ref[pl.ds(l, lc), :]
    r0_f = lax.dot_general((kctx * jnp.exp(lg_f * (lc - 1.0 - im))).astype(BF16), vctx, rdim,
                           preferred_element_type=F32)
    r0_b = lax.dot_general((kctx * jnp.exp(lg_b * im)).astype(BF16), vctx, rdim, preferred_element_type=F32)

    def chunk_kv(rows, zeta):
        kz = (k_ref[rows, :].astype(F32) * zeta).astype(BF16)
        return lax.dot_general(kz, v_ref[rows, :], rdim, preferred_element_type=F32)

    def bwd_body(j, s_b):
        c = nc - 1 - j
        rows = pl.ds(pl.multiple_of(c * chunk, chunk), chunk)
        cross = jnp.dot(q_ref[rows, :], s_b.astype(BF16), preferred_element_type=F32)
        ob_sc[rows, :] = cross * xi_b
        return gc_b * s_b + chunk_kv(rows, zeta_b)

    lax.fori_loop(0, nc, bwd_body, r0_b, unroll=RET_UNROLL)

    def fwd_body(c, s_f):
        rows = pl.ds(pl.multiple_of(c * chunk, chunk), chunk)
        qc = q_ref[rows, :]
        scores = lax.dot_general(qc, k_ref[rows, :], cdim, preferred_element_type=F32) * dmat
        o = jnp.dot(scores.astype(BF16), v_ref[rows, :], preferred_element_type=F32)
        o = o + jnp.dot(qc, s_f.astype(BF16), preferred_element_type=F32) * xi_f + ob_sc[rows, :]
        y = o * lax.rsqrt(jnp.mean(o * o, axis=-1, keepdims=True) + EPS) * nw_ref[...]
        g = g_ref[rows, :].astype(F32)
        o_ref[rows, :] = (y * (g * jax.nn.sigmoid(g))).astype(o_ref.dtype)
        return gc_f * s_f + chunk_kv(rows, zeta_f)

    lax.fori_loop(0, nc, fwd_body, r0_f, unroll=RET_UNROLL)


def _ret_call(p_rope, p_plain, decs, ret_norm_w, l):
    _, b, t_all, _ = p_rope.shape
    h = RET_HEADS
    chunk = _pick(l, (256, 128))
    return pl.pallas_call(
        functools.partial(_ret_kernel, chunk=chunk),
        grid=(b, h),
        in_specs=[pl.BlockSpec((None, 2, QK_DIM), lambda bb, hh: (hh, 0, 0)),
                  pl.BlockSpec((None, None, l, QK_DIM), lambda bb, hh: (hh, bb, 0, 0)),
                  pl.BlockSpec((None, None, t_all, QK_DIM), lambda bb, hh: (h + hh, bb, 0, 0)),
                  pl.BlockSpec((None, None, t_all, V_DIM), lambda bb, hh: (2 * DIFF_HEADS + hh, bb, 0, 0)),
                  pl.BlockSpec((None, None, l, V_DIM), lambda bb, hh: (2 * DIFF_HEADS + h + hh, bb, 0, 0)),
                  pl.BlockSpec((1, V_DIM), lambda bb, hh: (0, 0))],
        out_specs=pl.BlockSpec((None, l, V_DIM), lambda bb, hh: (bb, 0, hh)),
        out_shape=jax.ShapeDtypeStruct((b, l, RET_WIDTH), BF16),
        scratch_shapes=[pltpu.VMEM((l, V_DIM), F32)],
        compiler_params=_params(ARB, ARB),
        name="retention",
    )(decs, p_rope, p_rope, p_plain, p_plain, ret_norm_w.reshape(1, V_DIM))


def _outproj_kernel(ad_ref, ar_ref, wd_ref, wr_ref, x_ref, gate_ref, o_ref):
    y = jnp.dot(ad_ref[...], wd_ref[...], preferred_element_type=F32)
    y = y + jnp.dot(ar_ref[...], wr_ref[...], preferred_element_type=F32)
    o_ref[...] = x_ref[...] + gate_ref[...] * y


def _outproj_call(o_d, o_r, w_out, x, gate):
    b, l, d = x.shape
    kd, kr = o_d.shape[-1], o_r.shape[-1]
    tm = _pick(l, (1024, 512, 256, 128))
    tn = _pick(d, (512, 256, 128))
    per_b = l // tm
    return pl.pallas_call(
        _outproj_kernel,
        grid=(b * per_b, d // tn),
        in_specs=[pl.BlockSpec((tm, kd), lambda i, j: (i, 0)),
                  pl.BlockSpec((tm, kr), lambda i, j: (i, 0)),
                  pl.BlockSpec((kd, tn), lambda i, j: (0, j)),
                  pl.BlockSpec((kr, tn), lambda i, j: (kd // kr, j)),
                  pl.BlockSpec((tm, tn), lambda i, j: (i, j)),
                  pl.BlockSpec((None, 1, tn), lambda i, j: (i // per_b, 0, j))],
        out_specs=pl.BlockSpec((tm, tn), lambda i, j: (i, j)),
        out_shape=jax.ShapeDtypeStruct((b * l, d), F32),
        compiler_params=_params(ARB, ARB),
        name="outproj",
    )(o_d.reshape(b * l, kd), o_r.reshape(b * l, kr), w_out, w_out, x.reshape(b * l, d),
      gate.reshape(b, 1, d)).reshape(b, l, d)


def _rope_tables(l, lc):
    row, col = jnp.meshgrid(jnp.arange(l // GRID_W), jnp.arange(GRID_W), indexing="ij")
    row = row.reshape(-1).astype(F32)
    col = col.reshape(-1).astype(F32)
    half = QK_DIM // 2
    inv_freq = ROPE_BASE ** (-jnp.arange(0, half, 2, dtype=F32) / half)
    ang_r = row[:, None] * inv_freq
    ang_c = col[:, None] * inv_freq
    ang = jnp.concatenate([ang_r, ang_r, ang_c, ang_c], axis=-1)
    cos, sin = jnp.cos(ang), jnp.sin(ang)
    lo = (jnp.arange(QK_DIM) % half) < (half // 2)
    cos = jnp.concatenate([cos, jnp.ones((lc, QK_DIM), F32)], axis=0)
    sin_lo = jnp.concatenate([jnp.where(lo, -sin, 0.0), jnp.zeros((lc, QK_DIM), F32)], axis=0)
    sin_hi = jnp.concatenate([jnp.where(lo, 0.0, sin), jnp.zeros((lc, QK_DIM), F32)], axis=0)
    return cos, sin_lo, sin_hi


def _layer(x, ctx, c, c_ctx, norm_w, ada_w, ada_b, w_in, q_norm_w, k_norm_w, lam_params, subln_w,
           dec_f, dec_b, ret_norm_w, w_out):
    b, l, d = x.shape
    lc = ctx.shape[1]
    t_all = l + lc

    rows = -(-(b + 1) // 8) * 8
    cc = jnp.zeros((rows, d), F32).at[:b].set(c).at[b].set(c_ctx)
    mod = _ada_call(cc, ada_w, ada_b)
    shift, scale, gate = mod[:b, :d], mod[:b, d:2 * d], mod[:b, 2 * d:]
    shift_c = jnp.broadcast_to(mod[b, :d], (b, d))
    scale_c = jnp.broadcast_to(mod[b, d:2 * d], (b, d))
    mods = jnp.stack([scale, shift, scale_c, shift_c], axis=1)

    h_all = _norm_call(x, ctx, norm_w, mods).reshape(b * t_all, d)

    w_in_b = w_in.astype(BF16)
    tables = _rope_tables(l, lc)
    tn = 512
    blocks = lambda lo, hi: list(range(lo // tn, hi // tn))
    qk_vecs = jnp.stack([q_norm_w, k_norm_w]).reshape(2, 1, QK_DIM)
    ret_vecs = jnp.stack([jnp.ones((QK_DIM,), F32), jnp.full((QK_DIM,), QK_DIM ** -0.5, F32)]).reshape(2, 1, QK_DIM)
    call = functools.partial(_inproj_call, h_all, w_in_b, tables, rows_per_batch=t_all)
    p_nr = call(qk_vecs, mode="normrope", col_blocks=blocks(O_DQ, O_DV), out_w=V_DIM)
    p_rope = call(ret_vecs, mode="rope", col_blocks=blocks(O_RQ, O_RV), out_w=QK_DIM)
    p_plain = call(None, mode="plain", col_blocks=blocks(O_DV, O_RQ) + blocks(O_RV, O_RG + RET_WIDTH), out_w=V_DIM)
    p_nr = p_nr.reshape(-1, b, t_all, V_DIM)
    p_rope = p_rope.reshape(-1, b, t_all, QK_DIM)
    p_plain = p_plain.reshape(-1, b, t_all, V_DIM)

    o_d = _attn_call(p_nr, p_plain, lam_params, subln_w, l)
    decs = jnp.broadcast_to(jnp.stack([dec_f, dec_b], axis=1)[:, :, None], (RET_HEADS, 2, QK_DIM)).astype(F32)
    o_r = _ret_call(p_rope, p_plain, decs, ret_norm_w, l)

    return _outproj_call(o_d, o_r, w_out.astype(BF16), x, gate)


def kernel(x, c, ctx, c_ctx, norm_w, ada_w, ada_b, w_in, diff_q_norm_w, diff_k_norm_w, diff_lambda_q1,
           diff_lambda_k1, diff_lambda_q2, diff_lambda_k2, diff_subln_w, ret_decay_fwd, ret_decay_bwd,
           ret_norm_w, w_out):
    depth = norm_w.shape[0]
    assert depth == 1, "LAM_INIT is the layer-0 value"
    for layer in range(depth):
        lam_params = jnp.stack([diff_lambda_q1[layer], diff_lambda_k1[layer],
                                diff_lambda_q2[layer], diff_lambda_k2[layer]]).astype(F32)
        x = _layer(x, ctx, c, c_ctx, norm_w[layer], ada_w[layer], ada_b[layer], w_in[layer],
                   diff_q_norm_w[layer], diff_k_norm_w[layer], lam_params, diff_subln_w[layer],
                   ret_decay_fwd[layer], ret_decay_bwd[layer], ret_norm_w[layer], w_out[layer])
    return x
```

```python
import functools
import math

import jax
import jax.numpy as jnp
from jax import lax
from jax.experimental import pallas as pl
from jax.experimental.pallas import tpu as pltpu

F32 = jnp.float32
BF16 = jnp.bfloat16

GRID_W = 64
DIFF_HEADS = 8
QK_DIM = 128
V_DIM = 256
RET_HEADS = 8
ROPE_BASE = 10000.0
EPS = 1e-6
LAM_INIT = 0.8 - 0.6 * math.exp(-0.3 * 0)

DIFF_QK_W = DIFF_HEADS * 2 * QK_DIM
DIFF_WIDTH = DIFF_HEADS * V_DIM
RET_QK_W = RET_HEADS * QK_DIM
RET_WIDTH = RET_HEADS * V_DIM
O_DQ = 0
O_DK = O_DQ + DIFF_QK_W
O_DV = O_DK + DIFF_QK_W
O_DG = O_DV + DIFF_WIDTH
O_RQ = O_DG + DIFF_WIDTH
O_RK = O_RQ + RET_QK_W
O_RV = O_RK + RET_QK_W
O_RG = O_RV + RET_WIDTH

V7X_VMEM_LIMIT_BYTES = 56 * 1024 * 1024
ARB = "arbitrary"
RET_UNROLL = 8
INPROJ_TN = 1024


def _pick(n, candidates):
    for c in candidates:
        if n % c == 0:
            return c
    raise ValueError(f"no tile in {candidates} divides {n}")


def _params(*sem):
    return pltpu.CompilerParams(dimension_semantics=sem, vmem_limit_bytes=V7X_VMEM_LIMIT_BYTES)


def _ada_kernel(c_ref, w_ref, b_ref, o_ref):
    c = c_ref[...]
    a = (c * jax.nn.sigmoid(c)).astype(BF16)
    o_ref[...] = jnp.dot(a, w_ref[...].astype(BF16), preferred_element_type=F32) + b_ref[...]


def _ada_call(cc, ada_w, ada_b):
    rows, d = cc.shape
    n = ada_w.shape[1]
    tn = _pick(n, (512, 256, 128))
    return pl.pallas_call(
        _ada_kernel,
        grid=(n // tn,),
        in_specs=[pl.BlockSpec((rows, d), lambda j: (0, 0)),
                  pl.BlockSpec((d, tn), lambda j: (0, j)),
                  pl.BlockSpec((1, tn), lambda j: (0, j))],
        out_specs=pl.BlockSpec((rows, tn), lambda j: (0, j)),
        out_shape=jax.ShapeDtypeStruct((rows, n), F32),
        compiler_params=_params(ARB),
        name="ada_mod",
    )(cc, ada_w, ada_b.reshape(1, n))


def _norm_kernel(x_ref, ctx_ref, nw_ref, mod_ref, o_ref, *, n_lat):
    i = pl.program_id(1)

    def norm_mod(xv, scale, shift):
        ms = jnp.mean(xv * xv, axis=-1, keepdims=True)
        y = xv * lax.rsqrt(ms + EPS) * nw_ref[...]
        return (y * (1.0 + scale) + shift).astype(o_ref.dtype)

    @pl.when(i < n_lat)
    def _():
        o_ref[...] = norm_mod(x_ref[...], mod_ref[0:1, :], mod_ref[1:2, :])

    @pl.when(i >= n_lat)
    def _():
        o_ref[...] = norm_mod(ctx_ref[...], mod_ref[2:3, :], mod_ref[3:4, :])


def _norm_call(x, ctx, norm_w, mods):
    b, l, d = x.shape
    lc = ctx.shape[1]
    tr = _pick(math.gcd(l, lc), (256, 128, 64, 32, 16))
    n_lat, n_ctx = l // tr, lc // tr
    return pl.pallas_call(
        functools.partial(_norm_kernel, n_lat=n_lat),
        grid=(b, n_lat + n_ctx),
        in_specs=[pl.BlockSpec((None, tr, d), lambda bb, i: (bb, jnp.minimum(i, n_lat - 1), 0)),
                  pl.BlockSpec((None, tr, d), lambda bb, i: (bb, jnp.maximum(i - n_lat, 0), 0)),
                  pl.BlockSpec((1, d), lambda bb, i: (0, 0)),
                  pl.BlockSpec((None, 4, d), lambda bb, i: (bb, 0, 0))],
        out_specs=pl.BlockSpec((None, tr, d), lambda bb, i: (bb, i, 0)),
        out_shape=jax.ShapeDtypeStruct((b, l + lc, d), BF16),
        compiler_params=_params(ARB, ARB),
        name="norm_mod",
    )(x, ctx, norm_w.reshape(1, d), mods)


def _rope(xh, cos, sin_lo, sin_hi):
    return xh * cos + pltpu.roll(xh, 96, 1) * sin_lo + pltpu.roll(xh, 32, 1) * sin_hi


def _inproj_plain_kernel(cb_ref, a_ref, w_ref, o_ref, *, out_w):
    del cb_ref
    acc = jnp.dot(a_ref[...], w_ref[...], preferred_element_type=F32)
    for c in range(acc.shape[1] // out_w):
        o_ref[c] = acc[:, c * out_w:(c + 1) * out_w].astype(o_ref.dtype)


def _inproj_rope_kernel(cb_ref, a_ref, w_ref, cos_ref, slo_ref, shi_ref, vec_ref, o_ref, acc_sc, *, norm, out_w):
    del cb_ref

    @pl.when(pl.program_id(0) == 0)
    def _():
        acc_sc[...] = jnp.zeros_like(acc_sc)

    cos, slo, shi, vec = cos_ref[...], slo_ref[...], shi_ref[...], vec_ref[...]
    per = out_w // QK_DIM
    for t in range(acc_sc.shape[1] // QK_DIM):
        xh = acc_sc[:, t * QK_DIM:(t + 1) * QK_DIM]
        if norm:
            xh = xh * lax.rsqrt(jnp.mean(xh * xh, axis=-1, keepdims=True) + EPS) * vec
            r = _rope(xh, cos, slo, shi)
        else:
            r = _rope(xh, cos, slo, shi) * vec
        lane0 = (t % per) * QK_DIM
        o_ref[t // per, :, lane0:lane0 + QK_DIM] = r.astype(o_ref.dtype)
    acc_sc[...] = jnp.dot(a_ref[...], w_ref[...], preferred_element_type=F32)


def _inproj_call(a2d, w, tables, vecs, *, mode, col_blocks, out_w, rows_per_batch):
    m, d = a2d.shape
    tn = INPROJ_TN
    tm = _pick(rows_per_batch, (768, 512, 384, 256, 128))
    n_i, n_j = m // tm, len(col_blocks)
    n_out = n_j * tn // out_w
    col_tab = jnp.asarray(col_blocks, jnp.int32)
    out_shape = jax.ShapeDtypeStruct((n_out, m, out_w), BF16)
    if mode == "plain":
        return pl.pallas_call(
            functools.partial(_inproj_plain_kernel, out_w=out_w),
            grid_spec=pltpu.PrefetchScalarGridSpec(
                num_scalar_prefetch=1,
                grid=(n_i, n_j),
                in_specs=[pl.BlockSpec((tm, d), lambda i, j, cb: (i, 0)),
                          pl.BlockSpec((d, tn), lambda i, j, cb: (0, cb[j]))],
                out_specs=pl.BlockSpec((tn // out_w, tm, out_w), lambda i, j, cb: (j, i, 0)),
            ),
            out_shape=out_shape,
            compiler_params=_params(ARB, ARB),
            name="inproj_plain",
        )(col_tab, a2d, w)

    steps = n_i * n_j
    t_blocks = rows_per_batch // tm
    j_per_vec = n_j // vecs.shape[0]
    cur = lambda s: jnp.minimum(s, steps - 1)
    prev = lambda s: jnp.maximum(s - 1, 0)
    tab_spec = pl.BlockSpec((tm, QK_DIM), lambda s, cb: ((prev(s) // n_j) % t_blocks, 0))
    return pl.pallas_call(
        functools.partial(_inproj_rope_kernel, norm=(mode == "normrope"), out_w=out_w),
        grid_spec=pltpu.PrefetchScalarGridSpec(
            num_scalar_prefetch=1,
            grid=(steps + 1,),
            in_specs=[pl.BlockSpec((tm, d), lambda s, cb: (cur(s) // n_j, 0)),
                      pl.BlockSpec((d, tn), lambda s, cb: (0, cb[cur(s) % n_j])),
                      tab_spec, tab_spec, tab_spec,
                      pl.BlockSpec((None, 1, QK_DIM), lambda s, cb: ((prev(s) % n_j) // j_per_vec, 0, 0))],
            out_specs=pl.BlockSpec((tn // out_w, tm, out_w), lambda s, cb: (prev(s) % n_j, prev(s) // n_j, 0)),
            scratch_shapes=[pltpu.VMEM((tm, tn), F32)],
        ),
        out_shape=out_shape,
        compiler_params=_params(ARB),
        name="inproj_" + mode,
    )(col_tab, a2d, w, *tables, vecs)


def _attn_kernel(lamp_ref, q_ref, k_ref, v_ref, g_ref, sw_ref, o_ref, vt_sc, s_sc, p_sc, acc_sc, *, tk):
    t_all = k_ref.shape[0]
    tq = q_ref.shape[0]
    n = t_all // tk
    c_exp = (QK_DIM ** -0.5) * math.log2(math.e)
    nt = (((1,), (1,)), ((), ()))

    @pl.when(pl.program_id(2) == 0)
    def _():
        def transpose_chunk(c, carry):
            rows = pl.ds(pl.multiple_of(c * tk, tk), tk)
            vt_sc[c] = v_ref[rows, :].astype(F32).T.astype(BF16)
            return carry
        lax.fori_loop(0, n, transpose_chunk, 0)

    qt = q_ref[...].astype(F32).T.astype(BF16)
    qts = (qt[:QK_DIM], qt[QK_DIM:])
    acc_sc[...] = jnp.zeros_like(acc_sc)

    def stage_qk(c):
        rows = pl.ds(pl.multiple_of(c * tk, tk), tk)
        kc = k_ref[rows, :]
        cms = []
        for sub in range(2):
            s = jnp.dot(kc[:, sub * QK_DIM:(sub + 1) * QK_DIM], qts[sub],
                        preferred_element_type=F32)
            s_sc[sub] = s
            cms.append(jnp.max(s, axis=0, keepdims=True))
        return tuple(cms)

    def stage_softmax(cms, ms, ls):
        new_m, new_l, alphas = [], [], []
        for sub in range(2):
            m_new = jnp.maximum(ms[sub], cms[sub])
            alpha = jnp.exp2((ms[sub] - m_new) * c_exp)
            p = jnp.exp2(s_sc[sub] * c_exp - m_new * c_exp)
            new_l.append(alpha * ls[sub] + jnp.sum(p.reshape(tk // 8, 8, tq), axis=0))
            p_sc[sub] = p.astype(BF16)
            new_m.append(m_new)
            alphas.append(alpha)
        return tuple(new_m), tuple(new_l), tuple(alphas)

    def stage_pv(c, alphas):
        vt = vt_sc[c]
        for sub in range(2):
            acc_sc[sub] = alphas[sub] * acc_sc[sub] + jnp.dot(vt, p_sc[sub], preferred_element_type=F32)

    neg = jnp.full((1, tq), -jnp.inf, F32)
    ms = (neg, neg)
    ls = (jnp.zeros((8, tq), F32),) * 2
    cms = stage_qk(0)
    ms, ls, alphas = stage_softmax(cms, ms, ls)
    cms = stage_qk(1)

    def tick(t, carry):
        cms, ms, ls, alphas = carry
        stage_pv(t - 2, alphas)
        ms, ls, alphas = stage_softmax(cms, ms, ls)
        cms = stage_qk(t)
        return cms, ms, ls, alphas

    cms, ms, ls, alphas = lax.fori_loop(2, n, tick, (cms, ms, ls, alphas))
    stage_pv(n - 2, alphas)
    ms, ls, alphas = stage_softmax(cms, ms, ls)
    stage_pv(n - 1, alphas)

    lp = lamp_ref[...]
    lam = (jnp.exp(jnp.sum(lp[0:1] * lp[1:2], axis=1, keepdims=True))
           - jnp.exp(jnp.sum(lp[2:3] * lp[3:4], axis=1, keepdims=True)) + LAM_INIT)
    inv1 = 1.0 / jnp.sum(ls[0], axis=0, keepdims=True)
    inv2 = lam / jnp.sum(ls[1], axis=0, keepdims=True)
    o = (acc_sc[0] * inv1 - acc_sc[1] * inv2).T
    y = o * lax.rsqrt(jnp.mean(o * o, axis=-1, keepdims=True) + EPS) * sw_ref[...] * (1.0 - LAM_INIT)
    g = g_ref[...].astype(F32)
    o_ref[...] = (y * (g * jax.nn.sigmoid(g))).astype(o_ref.dtype)


def _attn_call(p_nr, p_plain, lam_params, subln_w, l):
    _, b, t_all, _ = p_nr.shape
    tq = _pick(l, (1024, 512, 256, 128))
    tk = _pick(t_all, (768, 256))
    n = t_all // tk
    assert n >= 2, "the three-stage skew needs at least two key chunks"
    h = DIFF_HEADS
    return pl.pallas_call(
        functools.partial(_attn_kernel, tk=tk),
        grid=(b, h, l // tq),
        in_specs=[pl.BlockSpec((4, QK_DIM), lambda bb, hh, i: (0, 0)),
                  pl.BlockSpec((None, None, tq, V_DIM), lambda bb, hh, i: (hh, bb, i, 0)),
                  pl.BlockSpec((None, None, t_all, V_DIM), lambda bb, hh, i: (h + hh, bb, 0, 0)),
                  pl.BlockSpec((None, None, t_all, V_DIM), lambda bb, hh, i: (hh, bb, 0, 0)),
                  pl.BlockSpec((None, None, tq, V_DIM), lambda bb, hh, i: (h + hh, bb, i, 0)),
                  pl.BlockSpec((1, V_DIM), lambda bb, hh, i: (0, 0))],
        out_specs=pl.BlockSpec((None, tq, V_DIM), lambda bb, hh, i: (bb, i, hh)),
        out_shape=jax.ShapeDtypeStruct((b, l, DIFF_WIDTH), BF16),
        scratch_shapes=[pltpu.VMEM((n, V_DIM, tk), BF16),
                        pltpu.VMEM((2, tk, tq), F32),
                        pltpu.VMEM((2, tk, tq), BF16),
                        pltpu.VMEM((2, V_DIM, tq), F32)],
        compiler_params=_params(ARB, ARB, ARB),
        name="diff_attn",
    )(lam_params, p_nr, p_nr, p_plain, p_plain, subln_w.reshape(1, V_DIM))


def _ret_kernel(dec_ref, q_ref, k_ref, v_ref, g_ref, nw_ref, o_ref, ob_sc, *, chunk):
    l = q_ref.shape[0]
    lc = k_ref.shape[0] - l
    nc = l // chunk
    cdim = (((1,), (1,)), ((), ()))
    rdim = (((0,), (0,)), ((), ()))

    dec = dec_ref[...]
    lg = jnp.log1p(-jnp.exp2(-dec))
    lg_f, lg_b = lg[0:1, 0:1], lg[1:2, 0:1]

    def col_iota(n):
        return lax.broadcasted_iota(jnp.int32, (n, 1), 0).astype(F32)

    ic = col_iota(chunk)
    xi_f = jnp.exp(lg_f * (ic + 1.0))
    xi_b = jnp.exp(lg_b * (chunk - ic))
    zeta_f = jnp.exp(lg_f * (chunk - 1.0 - ic))
    zeta_b = jnp.exp(lg_b * ic)
    gc_f = jnp.exp(lg_f * chunk)
    gc_b = jnp.exp(lg_b * chunk)
    ri = lax.broadcasted_iota(jnp.int32, (chunk, chunk), 0)
    ci = lax.broadcasted_iota(jnp.int32, (chunk, chunk), 1)
    dist = (ri - ci).astype(F32)
    dmat = jnp.where(ri >= ci, jnp.exp(lg_f * jnp.maximum(dist, 0.0)), jnp.exp(lg_b * jnp.maximum(-dist, 0.0)))

    im = col_iota(lc)
    kctx = k_ref[pl.ds(l, lc), :].astype(F32)
    vctx = v_ref[pl.ds(l, lc), :]
    r0_f = lax.dot_general((kctx * jnp.exp(lg_f * (lc - 1.0 - im))).astype(BF16), vctx, rdim,
                           preferred_element_type=F32)
    r0_b = lax.dot_general((kctx * jnp.exp(lg_b * im)).astype(BF16), vctx, rdim, preferred_element_type=F32)

    def chunk_kv(rows, zeta):
        kz = (k_ref[rows, :].astype(F32) * zeta).astype(BF16)
        return lax.dot_general(kz, v_ref[rows, :], rdim, preferred_element_type=F32)

    def bwd_body(j, s_b):
        c = nc - 1 - j
        rows = pl.ds(pl.multiple_of(c * chunk, chunk), chunk)
        cross = jnp.dot(q_ref[rows, :], s_b.astype(BF16), preferred_element_type=F32)
        ob_sc[rows, :] = cross * xi_b
        return gc_b * s_b + chunk_kv(rows, zeta_b)

    lax.fori_loop(0, nc, bwd_body, r0_b, unroll=RET_UNROLL)

    def fwd_body(c, s_f):
        rows = pl.ds(pl.multiple_of(c * chunk, chunk), chunk)
        qc = q_ref[rows, :]
        scores = lax.dot_general(qc, k_ref[rows, :], cdim, preferred_element_type=F32) * dmat
        o = jnp.dot(scores.astype(BF16), v_ref[rows, :], preferred_element_type=F32)
        o = o + jnp.dot(qc, s_f.astype(BF16), preferred_element_type=F32) * xi_f + ob_sc[rows, :]
        y = o * lax.rsqrt(jnp.mean(o * o, axis=-1, keepdims=True) + EPS) * nw_ref[...]
        g = g_ref[rows, :].astype(F32)
        o_ref[rows, :] = (y * (g * jax.nn.sigmoid(g))).astype(o_ref.dtype)
        return gc_f * s_f + chunk_kv(rows, zeta_f)

    lax.fori_loop(0, nc, fwd_body, r0_f, unroll=RET_UNROLL)


def _ret_call(p_rope, p_plain, decs, ret_norm_w, l):
    _, b, t_all, _ = p_rope.shape
    h = RET_HEADS
    chunk = _pick(l, (256, 128))
    return pl.pallas_call(
        functools.partial(_ret_kernel, chunk=chunk),
        grid=(b, h),
        in_specs=[pl.BlockSpec((None, 2, QK_DIM), lambda bb, hh: (hh, 0, 0)),
                  pl.BlockSpec((None, None, l, QK_DIM), lambda bb, hh: (hh, bb, 0, 0)),
                  pl.BlockSpec((None, None, t_all, QK_DIM), lambda bb, hh: (h + hh, bb, 0, 0)),
                  pl.BlockSpec((None, None, t_all, V_DIM), lambda bb, hh: (2 * DIFF_HEADS + hh, bb, 0, 0)),
                  pl.BlockSpec((None, None, l, V_DIM), lambda bb, hh: (2 * DIFF_HEADS + h + hh, bb, 0, 0)),
                  pl.BlockSpec((1, V_DIM), lambda bb, hh: (0, 0))],
        out_specs=pl.BlockSpec((None, l, V_DIM), lambda bb, hh: (bb, 0, hh)),
        out_shape=jax.ShapeDtypeStruct((b, l, RET_WIDTH), BF16),
        scratch_shapes=[pltpu.VMEM((l, V_DIM), F32)],
        compiler_params=_params(ARB, ARB),
        name="retention",
    )(decs, p_rope, p_rope, p_plain, p_plain, ret_norm_w.reshape(1, V_DIM))


def _outproj_kernel(ad_ref, ar_ref, wd_ref, wr_ref, x_ref, gate_ref, o_ref):
    y = jnp.dot(ad_ref[...], wd_ref[...], preferred_element_type=F32)
    y = y + jnp.dot(ar_ref[...], wr_ref[...], preferred_element_type=F32)
    o_ref[...] = x_ref[...] + gate_ref[...] * y


def _outproj_call(o_d, o_r, w_out, x, gate):
    b, l, d = x.shape
    kd, kr = o_d.shape[-1], o_r.shape[-1]
    tm = _pick(l, (1024, 512, 256, 128))
    tn = _pick(d, (512, 256, 128))
    per_b = l // tm
    return pl.pallas_call(
        _outproj_kernel,
        grid=(b * per_b, d // tn),
        in_specs=[pl.BlockSpec((tm, kd), lambda i, j: (i, 0)),
                  pl.BlockSpec((tm, kr), lambda i, j: (i, 0)),
                  pl.BlockSpec((kd, tn), lambda i, j: (0, j)),
                  pl.BlockSpec((kr, tn), lambda i, j: (kd // kr, j)),
                  pl.BlockSpec((tm, tn), lambda i, j: (i, j)),
                  pl.BlockSpec((None, 1, tn), lambda i, j: (i // per_b, 0, j))],
        out_specs=pl.BlockSpec((tm, tn), lambda i, j: (i, j)),
        out_shape=jax.ShapeDtypeStruct((b * l, d), F32),
        compiler_params=_params(ARB, ARB),
        name="outproj",
    )(o_d.reshape(b * l, kd), o_r.reshape(b * l, kr), w_out, w_out, x.reshape(b * l, d),
      gate.reshape(b, 1, d)).reshape(b, l, d)


def _rope_tables(l, lc):
    row, col = jnp.meshgrid(jnp.arange(l // GRID_W), jnp.arange(GRID_W), indexing="ij")
    row = row.reshape(-1).astype(F32)
    col = col.reshape(-1).astype(F32)
    half = QK_DIM // 2
    inv_freq = ROPE_BASE ** (-jnp.arange(0, half, 2, dtype=F32) / half)
    ang_r = row[:, None] * inv_freq
    ang_c = col[:, None] * inv_freq
    ang = jnp.concatenate([ang_r, ang_r, ang_c, ang_c], axis=-1)
    cos, sin = jnp.cos(ang), jnp.sin(ang)
    lo = (jnp.arange(QK_DIM) % half) < (half // 2)
    cos = jnp.concatenate([cos, jnp.ones((lc, QK_DIM), F32)], axis=0)
    sin_lo = jnp.concatenate([jnp.where(lo, -sin, 0.0), jnp.zeros((lc, QK_DIM), F32)], axis=0)
    sin_hi = jnp.concatenate([jnp.where(lo, 0.0, sin), jnp.zeros((lc, QK_DIM), F32)], axis=0)
    return cos, sin_lo, sin_hi


def _layer(x, ctx, c, c_ctx, norm_w, ada_w, ada_b, w_in, q_norm_w, k_norm_w, lam_params, subln_w,
           dec_f, dec_b, ret_norm_w, w_out):
    b, l, d = x.shape
    lc = ctx.shape[1]
    t_all = l + lc

    rows = -(-(b + 1) // 8) * 8
    cc = jnp.zeros((rows, d), F32).at[:b].set(c).at[b].set(c_ctx)
    mod = _ada_call(cc, ada_w, ada_b)
    shift, scale, gate = mod[:b, :d], mod[:b, d:2 * d], mod[:b, 2 * d:]
    shift_c = jnp.broadcast_to(mod[b, :d], (b, d))
    scale_c = jnp.broadcast_to(mod[b, d:2 * d], (b, d))
    mods = jnp.stack([scale, shift, scale_c, shift_c], axis=1)

    h_all = _norm_call(x, ctx, norm_w, mods).reshape(b * t_all, d)

    w_in_b = w_in.astype(BF16)
    tables = _rope_tables(l, lc)
    blocks = lambda lo, hi: list(range(lo // INPROJ_TN, hi // INPROJ_TN))
    qk_vecs = jnp.stack([q_norm_w, k_norm_w]).reshape(2, 1, QK_DIM)
    ret_vecs = jnp.stack([jnp.ones((QK_DIM,), F32), jnp.full((QK_DIM,), QK_DIM ** -0.5, F32)]).reshape(2, 1, QK_DIM)
    call = functools.partial(_inproj_call, h_all, w_in_b, tables, rows_per_batch=t_all)
    p_nr = call(qk_vecs, mode="normrope", col_blocks=blocks(O_DQ, O_DV), out_w=V_DIM)
    p_rope = call(ret_vecs, mode="rope", col_blocks=blocks(O_RQ, O_RV), out_w=QK_DIM)
    p_plain = call(None, mode="plain", col_blocks=blocks(O_DV, O_RQ) + blocks(O_RV, O_RG + RET_WIDTH), out_w=V_DIM)
    p_nr = p_nr.reshape(-1, b, t_all, V_DIM)
    p_rope = p_rope.reshape(-1, b, t_all, QK_DIM)
    p_plain = p_plain.reshape(-1, b, t_all, V_DIM)

    o_d = _attn_call(p_nr, p_plain, lam_params, subln_w, l)
    decs = jnp.broadcast_to(jnp.stack([dec_f, dec_b], axis=1)[:, :, None], (RET_HEADS, 2, QK_DIM)).astype(F32)
    o_r = _ret_call(p_rope, p_plain, decs, ret_norm_w, l)

    return _outproj_call(o_d, o_r, w_out.astype(BF16), x, gate)


def kernel(x, c, ctx, c_ctx, norm_w, ada_w, ada_b, w_in, diff_q_norm_w, diff_k_norm_w, diff_lambda_q1,
           diff_lambda_k1, diff_lambda_q2, diff_lambda_k2, diff_subln_w, ret_decay_fwd, ret_decay_bwd,
           ret_norm_w, w_out):
    depth = norm_w.shape[0]
    assert depth == 1, "LAM_INIT is the layer-0 value"
    for layer in range(depth):
        lam_params = jnp.stack([diff_lambda_q1[layer], diff_lambda_k1[layer],
                                diff_lambda_q2[layer], diff_lambda_k2[layer]]).astype(F32)
        x = _layer(x, ctx, c, c_ctx, norm_w[layer], ada_w[layer], ada_b[layer], w_in[layer],
                   diff_q_norm_w[layer], diff_k_norm_w[layer], lam_params, diff_subln_w[layer],
                   ret_decay_fwd[layer], ret_decay_bwd[layer], ret_norm_w[layer], w_out[layer])
    return x
```

```python
import functools
import math

import jax
import jax.numpy as jnp
from jax import lax
from jax.experimental import pallas as pl
from jax.experimental.pallas import tpu as pltpu

F32 = jnp.float32
BF16 = jnp.bfloat16

GRID_W = 64
DIFF_HEADS = 8
QK_DIM = 128
V_DIM = 256
RET_HEADS = 8
ROPE_BASE = 10000.0
EPS = 1e-6
LAM_INIT = 0.8 - 0.6 * math.exp(-0.3 * 0)

DIFF_QK_W = DIFF_HEADS * 2 * QK_DIM
DIFF_WIDTH = DIFF_HEADS * V_DIM
RET_QK_W = RET_HEADS * QK_DIM
RET_WIDTH = RET_HEADS * V_DIM
O_DQ = 0
O_DK = O_DQ + DIFF_QK_W
O_DV = O_DK + DIFF_QK_W
O_DG = O_DV + DIFF_WIDTH
O_RQ = O_DG + DIFF_WIDTH
O_RK = O_RQ + RET_QK_W
O_RV = O_RK + RET_QK_W
O_RG = O_RV + RET_WIDTH

V7X_VMEM_LIMIT_BYTES = 56 * 1024 * 1024
ARB = "arbitrary"
RET_UNROLL = 8
INPROJ_TN = 1024
CAST_ROWS, CAST_COLS = 256, 2048


def _pick(n, candidates):
    for c in candidates:
        if n % c == 0:
            return c
    raise ValueError(f"no tile in {candidates} divides {n}")


def _params(*sem):
    return pltpu.CompilerParams(dimension_semantics=sem, vmem_limit_bytes=V7X_VMEM_LIMIT_BYTES)


def _ada_kernel(c_ref, w_ref, b_ref, o_ref):
    c = c_ref[...]
    a = (c * jax.nn.sigmoid(c)).astype(BF16)
    o_ref[...] = jnp.dot(a, w_ref[...].astype(BF16), preferred_element_type=F32) + b_ref[...]


def _ada_call(cc, ada_w, ada_b):
    rows, d = cc.shape
    n = ada_w.shape[1]
    tn = _pick(n, (512, 256, 128))
    return pl.pallas_call(
        _ada_kernel,
        grid=(n // tn,),
        in_specs=[pl.BlockSpec((rows, d), lambda j: (0, 0)),
                  pl.BlockSpec((d, tn), lambda j: (0, j)),
                  pl.BlockSpec((1, tn), lambda j: (0, j))],
        out_specs=pl.BlockSpec((rows, tn), lambda j: (0, j)),
        out_shape=jax.ShapeDtypeStruct((rows, n), F32),
        compiler_params=_params(ARB),
        name="ada_mod",
    )(cc, ada_w, ada_b.reshape(1, n))


def _norm_kernel(x_ref, ctx_ref, nw_ref, mod_ref, o_ref, *, n_lat):
    i = pl.program_id(1)

    def norm_mod(xv, scale, shift):
        ms = jnp.mean(xv * xv, axis=-1, keepdims=True)
        y = xv * lax.rsqrt(ms + EPS) * nw_ref[...]
        return (y * (1.0 + scale) + shift).astype(o_ref.dtype)

    @pl.when(i < n_lat)
    def _():
        o_ref[...] = norm_mod(x_ref[...], mod_ref[0:1, :], mod_ref[1:2, :])

    @pl.when(i >= n_lat)
    def _():
        o_ref[...] = norm_mod(ctx_ref[...], mod_ref[2:3, :], mod_ref[3:4, :])


def _norm_call(x, ctx, norm_w, mods):
    b, l, d = x.shape
    lc = ctx.shape[1]
    tr = _pick(math.gcd(l, lc), (256, 128, 64, 32, 16))
    n_lat, n_ctx = l // tr, lc // tr
    return pl.pallas_call(
        functools.partial(_norm_kernel, n_lat=n_lat),
        grid=(b, n_lat + n_ctx),
        in_specs=[pl.BlockSpec((None, tr, d), lambda bb, i: (bb, jnp.minimum(i, n_lat - 1), 0)),
                  pl.BlockSpec((None, tr, d), lambda bb, i: (bb, jnp.maximum(i - n_lat, 0), 0)),
                  pl.BlockSpec((1, d), lambda bb, i: (0, 0)),
                  pl.BlockSpec((None, 4, d), lambda bb, i: (bb, 0, 0))],
        out_specs=pl.BlockSpec((None, tr, d), lambda bb, i: (bb, i, 0)),
        out_shape=jax.ShapeDtypeStruct((b, l + lc, d), BF16),
        compiler_params=_params(ARB, ARB),
        name="norm_mod",
    )(x, ctx, norm_w.reshape(1, d), mods)


def _rope(xh, cos, sin_lo, sin_hi):
    return xh * cos + pltpu.roll(xh, 96, 1) * sin_lo + pltpu.roll(xh, 32, 1) * sin_hi


def _inproj_plain_kernel(cb_ref, a_ref, w_ref, o_ref, *, out_w):
    del cb_ref
    acc = jnp.dot(a_ref[...], w_ref[...], preferred_element_type=F32)
    for c in range(acc.shape[1] // out_w):
        o_ref[c] = acc[:, c * out_w:(c + 1) * out_w].astype(o_ref.dtype)


def _inproj_rope_kernel(cb_ref, a_ref, w_ref, cos_ref, slo_ref, shi_ref, vec_ref, *rest, norm, out_w, cast_steps):
    del cb_ref
    if cast_steps:
        cast_in_ref, o_ref, cast_out_ref, acc_sc = rest

        @pl.when(pl.program_id(0) < cast_steps)
        def _():
            cast_out_ref[...] = cast_in_ref[...].astype(cast_out_ref.dtype)
    else:
        o_ref, acc_sc = rest

    @pl.when(pl.program_id(0) == 0)
    def _():
        acc_sc[...] = jnp.zeros_like(acc_sc)

    cos, slo, shi, vec = cos_ref[...], slo_ref[...], shi_ref[...], vec_ref[...]
    per = out_w // QK_DIM
    for t in range(acc_sc.shape[1] // QK_DIM):
        xh = acc_sc[:, t * QK_DIM:(t + 1) * QK_DIM]
        if norm:
            xh = xh * lax.rsqrt(jnp.mean(xh * xh, axis=-1, keepdims=True) + EPS) * vec
            r = _rope(xh, cos, slo, shi)
        else:
            r = _rope(xh, cos, slo, shi) * vec
        lane0 = (t % per) * QK_DIM
        o_ref[t // per, :, lane0:lane0 + QK_DIM] = r.astype(o_ref.dtype)
    acc_sc[...] = jnp.dot(a_ref[...], w_ref[...], preferred_element_type=F32)


def _cast_to_bf16(src, col0, ncols):
    return src[:, col0:col0 + ncols].astype(BF16)


def _inproj_call(a2d, w, tables, vecs, *, mode, col_blocks, out_w, rows_per_batch, cast=None):
    m, d = a2d.shape
    tn = INPROJ_TN
    tm = _pick(rows_per_batch, (768, 512, 384, 256, 128))
    n_i, n_j = m // tm, len(col_blocks)
    n_out = n_j * tn // out_w
    col_tab = jnp.asarray(col_blocks, jnp.int32)
    out_shape = jax.ShapeDtypeStruct((n_out, m, out_w), BF16)
    if mode == "plain":
        return pl.pallas_call(
            functools.partial(_inproj_plain_kernel, out_w=out_w),
            grid_spec=pltpu.PrefetchScalarGridSpec(
                num_scalar_prefetch=1,
                grid=(n_i, n_j),
                in_specs=[pl.BlockSpec((tm, d), lambda i, j, cb: (i, 0)),
                          pl.BlockSpec((d, tn), lambda i, j, cb: (0, cb[j]))],
                out_specs=pl.BlockSpec((tn // out_w, tm, out_w), lambda i, j, cb: (j, i, 0)),
            ),
            out_shape=out_shape,
            compiler_params=_params(ARB, ARB),
            name="inproj_plain",
        )(col_tab, a2d, w)

    steps = n_i * n_j
    t_blocks = rows_per_batch // tm
    j_per_vec = n_j // vecs.shape[0]
    cur = lambda s: jnp.minimum(s, steps - 1)
    prev = lambda s: jnp.maximum(s - 1, 0)
    tab_spec = pl.BlockSpec((tm, QK_DIM), lambda s, cb: ((prev(s) // n_j) % t_blocks, 0))
    in_specs = [pl.BlockSpec((tm, d), lambda s, cb: (cur(s) // n_j, 0)),
                pl.BlockSpec((d, tn), lambda s, cb: (0, cb[cur(s) % n_j])),
                tab_spec, tab_spec, tab_spec,
                pl.BlockSpec((None, 1, QK_DIM), lambda s, cb: ((prev(s) % n_j) // j_per_vec, 0, 0))]
    out_specs = [pl.BlockSpec((tn // out_w, tm, out_w), lambda s, cb: (prev(s) % n_j, prev(s) // n_j, 0))]
    out_shapes = [out_shape]
    args = [a2d, w, *tables, vecs]

    cast_src, cast_col0, cast_ncols = cast
    cast_rows = cast_src.shape[0]
    n_c = cast_ncols // CAST_COLS
    cast_steps = (cast_rows // CAST_ROWS) * n_c
    ride = (cast_ncols % CAST_COLS == 0 and cast_col0 % CAST_COLS == 0 and cast_rows % CAST_ROWS == 0
            and cast_steps <= steps)
    if ride:
        blk = lambda s: jnp.minimum(s, cast_steps - 1)
        c0 = cast_col0 // CAST_COLS
        in_specs.append(pl.BlockSpec((CAST_ROWS, CAST_COLS), lambda s, cb: (blk(s) // n_c, c0 + blk(s) % n_c)))
        out_specs.append(pl.BlockSpec((CAST_ROWS, CAST_COLS), lambda s, cb: (blk(s) // n_c, blk(s) % n_c)))
        out_shapes.append(jax.ShapeDtypeStruct((cast_rows, cast_ncols), BF16))
        args.append(cast_src)
    res = pl.pallas_call(
        functools.partial(_inproj_rope_kernel, norm=(mode == "normrope"), out_w=out_w,
                          cast_steps=cast_steps if ride else 0),
        grid_spec=pltpu.PrefetchScalarGridSpec(
            num_scalar_prefetch=1,
            grid=(steps + 1,),
            in_specs=in_specs,
            out_specs=out_specs,
            scratch_shapes=[pltpu.VMEM((tm, tn), F32)],
        ),
        out_shape=out_shapes,
        compiler_params=_params(ARB),
        name="inproj_" + mode,
    )(col_tab, *args)
    if ride:
        return res[0], res[1]
    return res[0], _cast_to_bf16(cast_src, cast_col0, cast_ncols)


def _attn_kernel(lamp_ref, q_ref, k_ref, v_ref, g_ref, sw_ref, o_ref, vt_sc, s_sc, p_sc, acc_sc, *, tk):
    t_all = k_ref.shape[0]
    tq = q_ref.shape[0]
    n = t_all // tk
    c_exp = (QK_DIM ** -0.5) * math.log2(math.e)
    nt = (((1,), (1,)), ((), ()))

    @pl.when(pl.program_id(2) == 0)
    def _():
        def transpose_chunk(c, carry):
            rows = pl.ds(pl.multiple_of(c * tk, tk), tk)
            vt_sc[c] = v_ref[rows, :].astype(F32).T.astype(BF16)
            return carry
        lax.fori_loop(0, n, transpose_chunk, 0)

    qt = q_ref[...].astype(F32).T.astype(BF16)
    qts = (qt[:QK_DIM], qt[QK_DIM:])
    acc_sc[...] = jnp.zeros_like(acc_sc)

    def stage_qk(c):
        rows = pl.ds(pl.multiple_of(c * tk, tk), tk)
        kc = k_ref[rows, :]
        cms = []
        for sub in range(2):
            s = jnp.dot(kc[:, sub * QK_DIM:(sub + 1) * QK_DIM], qts[sub],
                        preferred_element_type=F32)
            s_sc[sub] = s
            cms.append(jnp.max(s, axis=0, keepdims=True))
        return tuple(cms)

    def stage_softmax(cms, ms, ls):
        new_m, new_l, alphas = [], [], []
        for sub in range(2):
            m_new = jnp.maximum(ms[sub], cms[sub])
            alpha = jnp.exp2((ms[sub] - m_new) * c_exp)
            p = jnp.exp2(s_sc[sub] * c_exp - m_new * c_exp)
            new_l.append(alpha * ls[sub] + jnp.sum(p.reshape(tk // 8, 8, tq), axis=0))
            p_sc[sub] = p.astype(BF16)
            new_m.append(m_new)
            alphas.append(alpha)
        return tuple(new_m), tuple(new_l), tuple(alphas)

    def stage_pv(c, alphas):
        vt = vt_sc[c]
        for sub in range(2):
            acc_sc[sub] = alphas[sub] * acc_sc[sub] + jnp.dot(vt, p_sc[sub], preferred_element_type=F32)

    neg = jnp.full((1, tq), -jnp.inf, F32)
    ms = (neg, neg)
    ls = (jnp.zeros((8, tq), F32),) * 2
    cms = stage_qk(0)
    ms, ls, alphas = stage_softmax(cms, ms, ls)
    cms = stage_qk(1)

    def tick(t, carry):
        cms, ms, ls, alphas = carry
        stage_pv(t - 2, alphas)
        ms, ls, alphas = stage_softmax(cms, ms, ls)
        cms = stage_qk(t)
        return cms, ms, ls, alphas

    cms, ms, ls, alphas = lax.fori_loop(2, n, tick, (cms, ms, ls, alphas))
    stage_pv(n - 2, alphas)
    ms, ls, alphas = stage_softmax(cms, ms, ls)
    stage_pv(n - 1, alphas)

    lp = lamp_ref[...]
    lam = (jnp.exp(jnp.sum(lp[0:1] * lp[1:2], axis=1, keepdims=True))
           - jnp.exp(jnp.sum(lp[2:3] * lp[3:4], axis=1, keepdims=True)) + LAM_INIT)
    inv1 = 1.0 / jnp.sum(ls[0], axis=0, keepdims=True)
    inv2 = lam / jnp.sum(ls[1], axis=0, keepdims=True)
    o = (acc_sc[0] * inv1 - acc_sc[1] * inv2).T
    y = o * lax.rsqrt(jnp.mean(o * o, axis=-1, keepdims=True) + EPS) * sw_ref[...] * (1.0 - LAM_INIT)
    g = g_ref[...].astype(F32)
    o_ref[...] = (y * (g * jax.nn.sigmoid(g))).astype(o_ref.dtype)


def _attn_call(p_nr, p_plain, lam_params, subln_w, l):
    _, b, t_all, _ = p_nr.shape
    tq = _pick(l, (1024, 512, 256, 128))
    tk = _pick(t_all, (768, 256))
    n = t_all // tk
    assert n >= 2, "the three-stage skew needs at least two key chunks"
    h = DIFF_HEADS
    return pl.pallas_call(
        functools.partial(_attn_kernel, tk=tk),
        grid=(b, h, l // tq),
        in_specs=[pl.BlockSpec((4, QK_DIM), lambda bb, hh, i: (0, 0)),
                  pl.BlockSpec((None, None, tq, V_DIM), lambda bb, hh, i: (hh, bb, i, 0)),
                  pl.BlockSpec((None, None, t_all, V_DIM), lambda bb, hh, i: (h + hh, bb, 0, 0)),
                  pl.BlockSpec((None, None, t_all, V_DIM), lambda bb, hh, i: (hh, bb, 0, 0)),
                  pl.BlockSpec((None, None, tq, V_DIM), lambda bb, hh, i: (h + hh, bb, i, 0)),
                  pl.BlockSpec((1, V_DIM), lambda bb, hh, i: (0, 0))],
        out_specs=pl.BlockSpec((None, tq, V_DIM), lambda bb, hh, i: (bb, i, hh)),
        out_shape=jax.ShapeDtypeStruct((b, l, DIFF_WIDTH), BF16),
        scratch_shapes=[pltpu.VMEM((n, V_DIM, tk), BF16),
                        pltpu.VMEM((2, tk, tq), F32),
                        pltpu.VMEM((2, tk, tq), BF16),
                        pltpu.VMEM((2, V_DIM, tq), F32)],
        compiler_params=_params(ARB, ARB, ARB),
        name="diff_attn",
    )(lam_params, p_nr, p_nr, p_plain, p_plain, subln_w.reshape(1, V_DIM))


def _ret_kernel(dec_ref, q_ref, k_ref, v_ref, g_ref, nw_ref, o_ref, ob_sc, *, chunk):
    l = q_ref.shape[0]
    lc = k_ref.shape[0] - l
    nc = l // chunk
    cdim = (((1,), (1,)), ((), ()))
    rdim = (((0,), (0,)), ((), ()))

    dec = dec_ref[...]
    lg = jnp.log1p(-jnp.exp2(-dec))
    lg_f, lg_b = lg[0:1, 0:1], lg[1:2, 0:1]

    def col_iota(n):
        return lax.broadcasted_iota(jnp.int32, (n, 1), 0).astype(F32)

    ic = col_iota(chunk)
    xi_f = jnp.exp(lg_f * (ic + 1.0))
    xi_b = jnp.exp(lg_b * (chunk - ic))
    zeta_f = jnp.exp(lg_f * (chunk - 1.0 - ic))
    zeta_b = jnp.exp(lg_b * ic)
    gc_f = jnp.exp(lg_f * chunk)
    gc_b = jnp.exp(lg_b * chunk)
    ri = lax.broadcasted_iota(jnp.int32, (chunk, chunk), 0)
    ci = lax.broadcasted_iota(jnp.int32, (chunk, chunk), 1)
    dist = (ri - ci).astype(F32)
    dmat = jnp.where(ri >= ci, jnp.exp(lg_f * jnp.maximum(dist, 0.0)), jnp.exp(lg_b * jnp.maximum(-dist, 0.0)))

    im = col_iota(lc)
    kctx = k_ref[pl.ds(l, lc), :].astype(F32)
    vctx = v_ref[pl.ds(l, lc), :]
    r0_f = lax.dot_general((kctx * jnp.exp(lg_f * (lc - 1.0 - im))).astype(BF16), vctx, rdim,
                           preferred_element_type=F32)
    r0_b = lax.dot_general((kctx * jnp.exp(lg_b * im)).astype(BF16), vctx, rdim, preferred_element_type=F32)

    def chunk_kv(rows, zeta):
        kz = (k_ref[rows, :].astype(F32) * zeta).astype(BF16)
        return lax.dot_general(kz, v_ref[rows, :], rdim, preferred_element_type=F32)

    def bwd_body(j, s_b):
        c = nc - 1 - j
        rows = pl.ds(pl.multiple_of(c * chunk, chunk), chunk)
        cross = jnp.dot(q_ref[rows, :], s_b.astype(BF16), preferred_element_type=F32)
        ob_sc[rows, :] = cross * xi_b
        return gc_b * s_b + chunk_kv(rows, zeta_b)

    lax.fori_loop(0, nc, bwd_body, r0_b, unroll=RET_UNROLL)

    def fwd_body(c, s_f):
        rows = pl.ds(pl.multiple_of(c * chunk, chunk), chunk)
        qc = q_ref[rows, :]
        scores = lax.dot_general(qc, k_ref[rows, :], cdim, preferred_element_type=F32) * dmat
        o = jnp.dot(scores.astype(BF16), v_ref[rows, :], preferred_element_type=F32)
        o = o + jnp.dot(qc, s_f.astype(BF16), preferred_element_type=F32) * xi_f + ob_sc[rows, :]
        y = o * lax.rsqrt(jnp.mean(o * o, axis=-1, keepdims=True) + EPS) * nw_ref[...]
        g = g_ref[rows, :].astype(F32)
        o_ref[rows, :] = (y * (g * jax.nn.sigmoid(g))).astype(o_ref.dtype)
        return gc_f * s_f + chunk_kv(rows, zeta_f)

    lax.fori_loop(0, nc, fwd_body, r0_f, unroll=RET_UNROLL)


def _ret_call(p_rope, p_plain, decs, ret_norm_w, l):
    _, b, t_all, _ = p_rope.shape
    h = RET_HEADS
    chunk = _pick(l, (256, 128))
    return pl.pallas_call(
        functools.partial(_ret_kernel, chunk=chunk),
        grid=(b, h),
        in_specs=[pl.BlockSpec((None, 2, QK_DIM), lambda bb, hh: (hh, 0, 0)),
                  pl.BlockSpec((None, None, l, QK_DIM), lambda bb, hh: (hh, bb, 0, 0)),
                  pl.BlockSpec((None, None, t_all, QK_DIM), lambda bb, hh: (h + hh, bb, 0, 0)),
                  pl.BlockSpec((None, None, t_all, V_DIM), lambda bb, hh: (2 * DIFF_HEADS + hh, bb, 0, 0)),
                  pl.BlockSpec((None, None, l, V_DIM), lambda bb, hh: (2 * DIFF_HEADS + h + hh, bb, 0, 0)),
                  pl.BlockSpec((1, V_DIM), lambda bb, hh: (0, 0))],
        out_specs=pl.BlockSpec((None, l, V_DIM), lambda bb, hh: (bb, 0, hh)),
        out_shape=jax.ShapeDtypeStruct((b, l, RET_WIDTH), BF16),
        scratch_shapes=[pltpu.VMEM((l, V_DIM), F32)],
        compiler_params=_params(ARB, ARB),
        name="retention",
    )(decs, p_rope, p_rope, p_plain, p_plain, ret_norm_w.reshape(1, V_DIM))


def _outproj_kernel(ad_ref, ar_ref, wd_ref, wr_ref, x_ref, gate_ref, o_ref):
    y = jnp.dot(ad_ref[...], wd_ref[...], preferred_element_type=F32)
    y = y + jnp.dot(ar_ref[...], wr_ref[...], preferred_element_type=F32)
    o_ref[...] = x_ref[...] + gate_ref[...] * y


def _outproj_call(o_d, o_r, w_out, x, gate):
    b, l, d = x.shape
    kd, kr = o_d.shape[-1], o_r.shape[-1]
    tm = _pick(l, (1024, 512, 256, 128))
    tn = _pick(d, (512, 256, 128))
    per_b = l // tm
    return pl.pallas_call(
        _outproj_kernel,
        grid=(b * per_b, d // tn),
        in_specs=[pl.BlockSpec((tm, kd), lambda i, j: (i, 0)),
                  pl.BlockSpec((tm, kr), lambda i, j: (i, 0)),
                  pl.BlockSpec((kd, tn), lambda i, j: (0, j)),
                  pl.BlockSpec((kr, tn), lambda i, j: (kd // kr, j)),
                  pl.BlockSpec((tm, tn), lambda i, j: (i, j)),
                  pl.BlockSpec((None, 1, tn), lambda i, j: (i // per_b, 0, j))],
        out_specs=pl.BlockSpec((tm, tn), lambda i, j: (i, j)),
        out_shape=jax.ShapeDtypeStruct((b * l, d), F32),
        compiler_params=_params(ARB, ARB),
        name="outproj",
    )(o_d.reshape(b * l, kd), o_r.reshape(b * l, kr), w_out, w_out, x.reshape(b * l, d),
      gate.reshape(b, 1, d)).reshape(b, l, d)


def _rope_tables(l, lc):
    row, col = jnp.meshgrid(jnp.arange(l // GRID_W), jnp.arange(GRID_W), indexing="ij")
    row = row.reshape(-1).astype(F32)
    col = col.reshape(-1).astype(F32)
    half = QK_DIM // 2
    inv_freq = ROPE_BASE ** (-jnp.arange(0, half, 2, dtype=F32) / half)
    ang_r = row[:, None] * inv_freq
    ang_c = col[:, None] * inv_freq
    ang = jnp.concatenate([ang_r, ang_r, ang_c, ang_c], axis=-1)
    cos, sin = jnp.cos(ang), jnp.sin(ang)
    lo = (jnp.arange(QK_DIM) % half) < (half // 2)
    cos = jnp.concatenate([cos, jnp.ones((lc, QK_DIM), F32)], axis=0)
    sin_lo = jnp.concatenate([jnp.where(lo, -sin, 0.0), jnp.zeros((lc, QK_DIM), F32)], axis=0)
    sin_hi = jnp.concatenate([jnp.where(lo, 0.0, sin), jnp.zeros((lc, QK_DIM), F32)], axis=0)
    return cos, sin_lo, sin_hi


def _layer(x, ctx, c, c_ctx, norm_w, ada_w, ada_b, w_in, q_norm_w, k_norm_w, lam_params, subln_w,
           dec_f, dec_b, ret_norm_w, w_out):
    b, l, d = x.shape
    lc = ctx.shape[1]
    t_all = l + lc

    rows = -(-(b + 1) // 8) * 8
    cc = jnp.zeros((rows, d), F32).at[:b].set(c).at[b].set(c_ctx)
    mod = _ada_call(cc, ada_w, ada_b)
    shift, scale, gate = mod[:b, :d], mod[:b, d:2 * d], mod[:b, 2 * d:]
    shift_c = jnp.broadcast_to(mod[b, :d], (b, d))
    scale_c = jnp.broadcast_to(mod[b, d:2 * d], (b, d))
    mods = jnp.stack([scale, shift, scale_c, shift_c], axis=1)

    h_all = _norm_call(x, ctx, norm_w, mods).reshape(b * t_all, d)

    d_in = w_in.shape[1]
    w_qk = _cast_to_bf16(w_in, O_DQ, O_DV)
    tables = _rope_tables(l, lc)
    blocks = lambda lo, hi, base: list(range((lo - base) // INPROJ_TN, (hi - base) // INPROJ_TN))
    qk_vecs = jnp.stack([q_norm_w, k_norm_w]).reshape(2, 1, QK_DIM)
    ret_vecs = jnp.stack([jnp.ones((QK_DIM,), F32), jnp.full((QK_DIM,), QK_DIM ** -0.5, F32)]).reshape(2, 1, QK_DIM)
    call = functools.partial(_inproj_call, h_all, tables=tables, rows_per_batch=t_all)
    p_nr, w_rest = call(w=w_qk, vecs=qk_vecs, mode="normrope", col_blocks=blocks(O_DQ, O_DV, O_DQ), out_w=V_DIM,
                        cast=(w_in, O_DV, d_in - O_DV))
    p_rope, w_out_b = call(w=w_rest, vecs=ret_vecs, mode="rope", col_blocks=blocks(O_RQ, O_RV, O_DV), out_w=QK_DIM,
                           cast=(w_out, 0, w_out.shape[1]))
    p_plain = call(w=w_rest, vecs=None, mode="plain",
                   col_blocks=blocks(O_DV, O_RQ, O_DV) + blocks(O_RV, d_in, O_DV), out_w=V_DIM)
    p_nr = p_nr.reshape(-1, b, t_all, V_DIM)
    p_rope = p_rope.reshape(-1, b, t_all, QK_DIM)
    p_plain = p_plain.reshape(-1, b, t_all, V_DIM)

    o_d = _attn_call(p_nr, p_plain, lam_params, subln_w, l)
    decs = jnp.broadcast_to(jnp.stack([dec_f, dec_b], axis=1)[:, :, None], (RET_HEADS, 2, QK_DIM)).astype(F32)
    o_r = _ret_call(p_rope, p_plain, decs, ret_norm_w, l)

    return _outproj_call(o_d, o_r, w_out_b, x, gate)


def kernel(x, c, ctx, c_ctx, norm_w, ada_w, ada_b, w_in, diff_q_norm_w, diff_k_norm_w, diff_lambda_q1,
           diff_lambda_k1, diff_lambda_q2, diff_lambda_k2, diff_subln_w, ret_decay_fwd, ret_decay_bwd,
           ret_norm_w, w_out):
    depth = norm_w.shape[0]
    assert depth == 1, "LAM_INIT is the layer-0 value"
    for layer in range(depth):
        lam_params = jnp.stack([diff_lambda_q1[layer], diff_lambda_k1[layer],
                                diff_lambda_q2[layer], diff_lambda_k2[layer]]).astype(F32)
        x = _layer(x, ctx, c, c_ctx, norm_w[layer], ada_w[layer], ada_b[layer], w_in[layer],
                   diff_q_norm_w[layer], diff_k_norm_w[layer], lam_params, diff_subln_w[layer],
                   ret_decay_fwd[layer], ret_decay_bwd[layer], ret_norm_w[layer], w_out[layer])
    return x
```

```python
import functools
import math

import jax
import jax.numpy as jnp
from jax import lax
from jax.experimental import pallas as pl
from jax.experimental.pallas import tpu as pltpu

F32 = jnp.float32
BF16 = jnp.bfloat16

GRID_W = 64
DIFF_HEADS = 8
QK_DIM = 128
V_DIM = 256
RET_HEADS = 8
ROPE_BASE = 10000.0
EPS = 1e-6
LAM_INIT = 0.8 - 0.6 * math.exp(-0.3 * 0)

DIFF_QK_W = DIFF_HEADS * 2 * QK_DIM
DIFF_WIDTH = DIFF_HEADS * V_DIM
RET_QK_W = RET_HEADS * QK_DIM
RET_WIDTH = RET_HEADS * V_DIM
O_DQ = 0
O_DK = O_DQ + DIFF_QK_W
O_DV = O_DK + DIFF_QK_W
O_DG = O_DV + DIFF_WIDTH
O_RQ = O_DG + DIFF_WIDTH
O_RK = O_RQ + RET_QK_W
O_RV = O_RK + RET_QK_W
O_RG = O_RV + RET_WIDTH

V7X_VMEM_LIMIT_BYTES = 56 * 1024 * 1024
ARB = "arbitrary"
RET_UNROLL = 8
INPROJ_TN = 1024
ATTN_TK = 1024
CAST_ROWS, CAST_COLS = 256, 2048


def _pick(n, candidates):
    for c in candidates:
        if n % c == 0:
            return c
    raise ValueError(f"no tile in {candidates} divides {n}")


def _params(*sem):
    return pltpu.CompilerParams(dimension_semantics=sem, vmem_limit_bytes=V7X_VMEM_LIMIT_BYTES)


def _ada_kernel(c_ref, w_ref, b_ref, o_ref):
    c = c_ref[...]
    a = (c * jax.nn.sigmoid(c)).astype(BF16)
    o_ref[...] = jnp.dot(a, w_ref[...].astype(BF16), preferred_element_type=F32) + b_ref[...]


def _ada_call(cc, ada_w, ada_b):
    rows, d = cc.shape
    n = ada_w.shape[1]
    tn = _pick(n, (512, 256, 128))
    return pl.pallas_call(
        _ada_kernel,
        grid=(n // tn,),
        in_specs=[pl.BlockSpec((rows, d), lambda j: (0, 0)),
                  pl.BlockSpec((d, tn), lambda j: (0, j)),
                  pl.BlockSpec((1, tn), lambda j: (0, j))],
        out_specs=pl.BlockSpec((rows, tn), lambda j: (0, j)),
        out_shape=jax.ShapeDtypeStruct((rows, n), F32),
        compiler_params=_params(ARB),
        name="ada_mod",
    )(cc, ada_w, ada_b.reshape(1, n))


def _norm_kernel(x_ref, ctx_ref, nw_ref, mod_ref, o_ref, *, n_lat):
    i = pl.program_id(1)

    def norm_mod(xv, scale, shift):
        ms = jnp.mean(xv * xv, axis=-1, keepdims=True)
        y = xv * lax.rsqrt(ms + EPS) * nw_ref[...]
        return (y * (1.0 + scale) + shift).astype(o_ref.dtype)

    @pl.when(i < n_lat)
    def _():
        o_ref[...] = norm_mod(x_ref[...], mod_ref[0:1, :], mod_ref[1:2, :])

    @pl.when(i >= n_lat)
    def _():
        o_ref[...] = norm_mod(ctx_ref[...], mod_ref[2:3, :], mod_ref[3:4, :])


def _norm_call(x, ctx, norm_w, mods):
    b, l, d = x.shape
    lc = ctx.shape[1]
    tr = _pick(math.gcd(l, lc), (256, 128, 64, 32, 16))
    n_lat, n_ctx = l // tr, lc // tr
    return pl.pallas_call(
        functools.partial(_norm_kernel, n_lat=n_lat),
        grid=(b, n_lat + n_ctx),
        in_specs=[pl.BlockSpec((None, tr, d), lambda bb, i: (bb, jnp.minimum(i, n_lat - 1), 0)),
                  pl.BlockSpec((None, tr, d), lambda bb, i: (bb, jnp.maximum(i - n_lat, 0), 0)),
                  pl.BlockSpec((1, d), lambda bb, i: (0, 0)),
                  pl.BlockSpec((None, 4, d), lambda bb, i: (bb, 0, 0))],
        out_specs=pl.BlockSpec((None, tr, d), lambda bb, i: (bb, i, 0)),
        out_shape=jax.ShapeDtypeStruct((b, l + lc, d), BF16),
        compiler_params=_params(ARB, ARB),
        name="norm_mod",
    )(x, ctx, norm_w.reshape(1, d), mods)


def _rope(xh, cos, sin_lo, sin_hi):
    return xh * cos + pltpu.roll(xh, 96, 1) * sin_lo + pltpu.roll(xh, 32, 1) * sin_hi


def _inproj_plain_kernel(cb_ref, a_ref, w_ref, o_ref, *, out_w):
    del cb_ref
    acc = jnp.dot(a_ref[...], w_ref[...], preferred_element_type=F32)
    for c in range(acc.shape[1] // out_w):
        o_ref[c] = acc[:, c * out_w:(c + 1) * out_w].astype(o_ref.dtype)


def _inproj_rope_kernel(cb_ref, a_ref, w_ref, cos_ref, slo_ref, shi_ref, vec_ref, *rest, norm, out_w, cast_steps):
    del cb_ref
    if cast_steps:
        cast_in_ref, o_ref, cast_out_ref, acc_sc = rest

        @pl.when(pl.program_id(0) < cast_steps)
        def _():
            cast_out_ref[...] = cast_in_ref[...].astype(cast_out_ref.dtype)
    else:
        o_ref, acc_sc = rest

    @pl.when(pl.program_id(0) == 0)
    def _():
        acc_sc[...] = jnp.zeros_like(acc_sc)

    cos, slo, shi, vec = cos_ref[...], slo_ref[...], shi_ref[...], vec_ref[...]
    per = out_w // QK_DIM
    for t in range(acc_sc.shape[1] // QK_DIM):
        xh = acc_sc[:, t * QK_DIM:(t + 1) * QK_DIM]
        if norm:
            xh = xh * lax.rsqrt(jnp.mean(xh * xh, axis=-1, keepdims=True) + EPS) * vec
            r = _rope(xh, cos, slo, shi)
        else:
            r = _rope(xh, cos, slo, shi) * vec
        lane0 = (t % per) * QK_DIM
        o_ref[t // per, :, lane0:lane0 + QK_DIM] = r.astype(o_ref.dtype)
    acc_sc[...] = jnp.dot(a_ref[...], w_ref[...], preferred_element_type=F32)


def _cast_to_bf16(src, col0, ncols):
    return src[:, col0:col0 + ncols].astype(BF16)


def _inproj_call(a2d, w, tables, vecs, *, mode, col_blocks, out_w, rows_per_batch, cast=None):
    m, d = a2d.shape
    tn = INPROJ_TN
    tm = _pick(rows_per_batch, (768, 512, 384, 256, 128))
    n_i, n_j = m // tm, len(col_blocks)
    n_out = n_j * tn // out_w
    col_tab = jnp.asarray(col_blocks, jnp.int32)
    out_shape = jax.ShapeDtypeStruct((n_out, m, out_w), BF16)
    if mode == "plain":
        return pl.pallas_call(
            functools.partial(_inproj_plain_kernel, out_w=out_w),
            grid_spec=pltpu.PrefetchScalarGridSpec(
                num_scalar_prefetch=1,
                grid=(n_i, n_j),
                in_specs=[pl.BlockSpec((tm, d), lambda i, j, cb: (i, 0)),
                          pl.BlockSpec((d, tn), lambda i, j, cb: (0, cb[j]))],
                out_specs=pl.BlockSpec((tn // out_w, tm, out_w), lambda i, j, cb: (j, i, 0)),
            ),
            out_shape=out_shape,
            compiler_params=_params(ARB, ARB),
            name="inproj_plain",
        )(col_tab, a2d, w)

    steps = n_i * n_j
    t_blocks = rows_per_batch // tm
    j_per_vec = n_j // vecs.shape[0]
    cur = lambda s: jnp.minimum(s, steps - 1)
    prev = lambda s: jnp.maximum(s - 1, 0)
    tab_spec = pl.BlockSpec((tm, QK_DIM), lambda s, cb: ((prev(s) // n_j) % t_blocks, 0))
    in_specs = [pl.BlockSpec((tm, d), lambda s, cb: (cur(s) // n_j, 0)),
                pl.BlockSpec((d, tn), lambda s, cb: (0, cb[cur(s) % n_j])),
                tab_spec, tab_spec, tab_spec,
                pl.BlockSpec((None, 1, QK_DIM), lambda s, cb: ((prev(s) % n_j) // j_per_vec, 0, 0))]
    out_specs = [pl.BlockSpec((tn // out_w, tm, out_w), lambda s, cb: (prev(s) % n_j, prev(s) // n_j, 0))]
    out_shapes = [out_shape]
    args = [a2d, w, *tables, vecs]

    cast_src, cast_col0, cast_ncols = cast
    cast_rows = cast_src.shape[0]
    n_c = cast_ncols // CAST_COLS
    cast_steps = (cast_rows // CAST_ROWS) * n_c
    ride = (cast_ncols % CAST_COLS == 0 and cast_col0 % CAST_COLS == 0 and cast_rows % CAST_ROWS == 0
            and cast_steps <= steps)
    if ride:
        blk = lambda s: jnp.minimum(s, cast_steps - 1)
        c0 = cast_col0 // CAST_COLS
        in_specs.append(pl.BlockSpec((CAST_ROWS, CAST_COLS), lambda s, cb: (blk(s) // n_c, c0 + blk(s) % n_c)))
        out_specs.append(pl.BlockSpec((CAST_ROWS, CAST_COLS), lambda s, cb: (blk(s) // n_c, blk(s) % n_c)))
        out_shapes.append(jax.ShapeDtypeStruct((cast_rows, cast_ncols), BF16))
        args.append(cast_src)
    res = pl.pallas_call(
        functools.partial(_inproj_rope_kernel, norm=(mode == "normrope"), out_w=out_w,
                          cast_steps=cast_steps if ride else 0),
        grid_spec=pltpu.PrefetchScalarGridSpec(
            num_scalar_prefetch=1,
            grid=(steps + 1,),
            in_specs=in_specs,
            out_specs=out_specs,
            scratch_shapes=[pltpu.VMEM((tm, tn), F32)],
        ),
        out_shape=out_shapes,
        compiler_params=_params(ARB),
        name="inproj_" + mode,
    )(col_tab, *args)
    if ride:
        return res[0], res[1]
    return res[0], _cast_to_bf16(cast_src, cast_col0, cast_ncols)


def _attn_kernel(lamp_ref, q_ref, k_ref, v_ref, g_ref, sw_ref, o_ref, vt_sc, vt_tail_sc, s_sc, p_sc, acc_sc, *, tk):
    t_all = k_ref.shape[0]
    tq = q_ref.shape[0]
    n_big, tail = divmod(t_all, tk)
    n = n_big + (1 if tail else 0)
    size = lambda c: tk if c < n_big else tail
    c_exp = (QK_DIM ** -0.5) * math.log2(math.e)

    def transposed(rows):
        return v_ref[rows, :].astype(F32).T.astype(BF16)

    @pl.when(pl.program_id(2) == 0)
    def _():
        def transpose_chunk(c, carry):
            vt_sc[c] = transposed(pl.ds(pl.multiple_of(c * tk, tk), tk))
            return carry
        lax.fori_loop(0, n_big, transpose_chunk, 0)
        if tail:
            vt_tail_sc[...] = transposed(pl.ds(n_big * tk, tail))

    qt = q_ref[...].astype(F32).T.astype(BF16)
    qts = (qt[:QK_DIM], qt[QK_DIM:])
    acc_sc[...] = jnp.zeros_like(acc_sc)

    def stage_qk(c, rows):
        start = c * tk if isinstance(c, int) else pl.multiple_of(c * tk, tk)
        kc = k_ref[pl.ds(start, rows), :]
        cms = []
        for sub in range(2):
            s = jnp.dot(kc[:, sub * QK_DIM:(sub + 1) * QK_DIM], qts[sub],
                        preferred_element_type=F32)
            s_sc[sub, :rows] = s
            cms.append(jnp.max(s, axis=0, keepdims=True))
        return tuple(cms)

    def stage_softmax(rows, cms, ms, ls):
        new_m, new_l, alphas = [], [], []
        for sub in range(2):
            m_new = jnp.maximum(ms[sub], cms[sub])
            alpha = jnp.exp2((ms[sub] - m_new) * c_exp)
            p = jnp.exp2(s_sc[sub, :rows] * c_exp - m_new * c_exp)
            new_l.append(alpha * ls[sub] + jnp.sum(p.reshape(rows // 8, 8, tq), axis=0))
            p_sc[sub, :rows] = p.astype(BF16)
            new_m.append(m_new)
            alphas.append(alpha)
        return tuple(new_m), tuple(new_l), tuple(alphas)

    def stage_pv(c, rows, alphas):
        vt = vt_sc[c] if rows == tk else vt_tail_sc[...]
        for sub in range(2):
            acc_sc[sub] = alphas[sub] * acc_sc[sub] + jnp.dot(vt, p_sc[sub, :rows], preferred_element_type=F32)

    def tick(t, carry):
        cms, ms, ls, alphas = carry
        static = isinstance(t, int)
        if not static or 0 <= t - 2 < n:
            stage_pv(t - 2, size(t - 2) if static else tk, alphas)
        if not static or 0 <= t - 1 < n:
            ms, ls, alphas = stage_softmax(size(t - 1) if static else tk, cms, ms, ls)
        if not static or t < n:
            cms = stage_qk(t, size(t) if static else tk)
        return cms, ms, ls, alphas

    neg = jnp.full((1, tq), -jnp.inf, F32)
    zero = jnp.zeros((1, tq), F32)
    carry = ((neg, neg), (neg, neg), (jnp.zeros((8, tq), F32),) * 2, (zero, zero))
    loop_lo, loop_hi = 2, max(2, n_big)
    for t in range(loop_lo):
        carry = tick(t, carry)
    carry = lax.fori_loop(loop_lo, loop_hi, tick, carry)
    for t in range(loop_hi, n + 2):
        carry = tick(t, carry)
    _, ms, ls, _ = carry

    lp = lamp_ref[...]
    lam = (jnp.exp(jnp.sum(lp[0:1] * lp[1:2], axis=1, keepdims=True))
           - jnp.exp(jnp.sum(lp[2:3] * lp[3:4], axis=1, keepdims=True)) + LAM_INIT)
    inv1 = 1.0 / jnp.sum(ls[0], axis=0, keepdims=True)
    inv2 = lam / jnp.sum(ls[1], axis=0, keepdims=True)
    o = (acc_sc[0] * inv1 - acc_sc[1] * inv2).T
    y = o * lax.rsqrt(jnp.mean(o * o, axis=-1, keepdims=True) + EPS) * sw_ref[...] * (1.0 - LAM_INIT)
    g = g_ref[...].astype(F32)
    o_ref[...] = (y * (g * jax.nn.sigmoid(g))).astype(o_ref.dtype)


def _attn_call(p_nr, p_plain, lam_params, subln_w, l):
    _, b, t_all, _ = p_nr.shape
    tq = _pick(l, (1024, 512, 256, 128))
    tk = ATTN_TK
    n_big, tail = divmod(t_all, tk)
    assert n_big >= 2 and tail % 256 == 0, "needs two full key chunks and a tail that fills whole MXU passes"
    h = DIFF_HEADS
    return pl.pallas_call(
        functools.partial(_attn_kernel, tk=tk),
        grid=(b, h, l // tq),
        in_specs=[pl.BlockSpec((4, QK_DIM), lambda bb, hh, i: (0, 0)),
                  pl.BlockSpec((None, None, tq, V_DIM), lambda bb, hh, i: (hh, bb, i, 0)),
                  pl.BlockSpec((None, None, t_all, V_DIM), lambda bb, hh, i: (h + hh, bb, 0, 0)),
                  pl.BlockSpec((None, None, t_all, V_DIM), lambda bb, hh, i: (hh, bb, 0, 0)),
                  pl.BlockSpec((None, None, tq, V_DIM), lambda bb, hh, i: (h + hh, bb, i, 0)),
                  pl.BlockSpec((1, V_DIM), lambda bb, hh, i: (0, 0))],
        out_specs=pl.BlockSpec((None, tq, V_DIM), lambda bb, hh, i: (bb, i, hh)),
        out_shape=jax.ShapeDtypeStruct((b, l, DIFF_WIDTH), BF16),
        scratch_shapes=[pltpu.VMEM((n_big, V_DIM, tk), BF16),
                        pltpu.VMEM((V_DIM, max(tail, 256)), BF16),
                        pltpu.VMEM((2, tk, tq), F32),
                        pltpu.VMEM((2, tk, tq), BF16),
                        pltpu.VMEM((2, V_DIM, tq), F32)],
        compiler_params=_params(ARB, ARB, ARB),
        name="diff_attn",
    )(lam_params, p_nr, p_nr, p_plain, p_plain, subln_w.reshape(1, V_DIM))


def _ret_kernel(dec_ref, q_ref, k_ref, v_ref, g_ref, nw_ref, o_ref, ob_sc, *, chunk):
    l = q_ref.shape[0]
    lc = k_ref.shape[0] - l
    nc = l // chunk
    cdim = (((1,), (1,)), ((), ()))
    rdim = (((0,), (0,)), ((), ()))

    dec = dec_ref[...]
    lg = jnp.log1p(-jnp.exp2(-dec))
    lg_f, lg_b = lg[0:1, 0:1], lg[1:2, 0:1]

    def col_iota(n):
        return lax.broadcasted_iota(jnp.int32, (n, 1), 0).astype(F32)

    ic = col_iota(chunk)
    xi_f = jnp.exp(lg_f * (ic + 1.0))
    xi_b = jnp.exp(lg_b * (chunk - ic))
    zeta_f = jnp.exp(lg_f * (chunk - 1.0 - ic))
    zeta_b = jnp.exp(lg_b * ic)
    gc_f = jnp.exp(lg_f * chunk)
    gc_b = jnp.exp(lg_b * chunk)
    ri = lax.broadcasted_iota(jnp.int32, (chunk, chunk), 0)
    ci = lax.broadcasted_iota(jnp.int32, (chunk, chunk), 1)
    dist = (ri - ci).astype(F32)
    dmat = jnp.where(ri >= ci, jnp.exp(lg_f * jnp.maximum(dist, 0.0)), jnp.exp(lg_b * jnp.maximum(-dist, 0.0)))

    im = col_iota(lc)
    kctx = k_ref[pl.ds(l, lc), :].astype(F32)
    vctx = v_ref[pl.ds(l, lc), :]
    r0_f = lax.dot_general((kctx * jnp.exp(lg_f * (lc - 1.0 - im))).astype(BF16), vctx, rdim,
                           preferred_element_type=F32)
    r0_b = lax.dot_general((kctx * jnp.exp(lg_b * im)).astype(BF16), vctx, rdim, preferred_element_type=F32)

    def chunk_kv(rows, zeta):
        kz = (k_ref[rows, :].astype(F32) * zeta).astype(BF16)
        return lax.dot_general(kz, v_ref[rows, :], rdim, preferred_element_type=F32)

    def bwd_body(j, s_b):
        c = nc - 1 - j
        rows = pl.ds(pl.multiple_of(c * chunk, chunk), chunk)
        cross = jnp.dot(q_ref[rows, :], s_b.astype(BF16), preferred_element_type=F32)
        ob_sc[rows, :] = cross * xi_b
        return gc_b * s_b + chunk_kv(rows, zeta_b)

    lax.fori_loop(0, nc, bwd_body, r0_b, unroll=RET_UNROLL)

    def fwd_body(c, s_f):
        rows = pl.ds(pl.multiple_of(c * chunk, chunk), chunk)
        qc = q_ref[rows, :]
        scores = lax.dot_general(qc, k_ref[rows, :], cdim, preferred_element_type=F32) * dmat
        o = jnp.dot(scores.astype(BF16), v_ref[rows, :], preferred_element_type=F32)
        o = o + jnp.dot(qc, s_f.astype(BF16), preferred_element_type=F32) * xi_f + ob_sc[rows, :]
        y = o * lax.rsqrt(jnp.mean(o * o, axis=-1, keepdims=True) + EPS) * nw_ref[...]
        g = g_ref[rows, :].astype(F32)
        o_ref[rows, :] = (y * (g * jax.nn.sigmoid(g))).astype(o_ref.dtype)
        return gc_f * s_f + chunk_kv(rows, zeta_f)

    lax.fori_loop(0, nc, fwd_body, r0_f, unroll=RET_UNROLL)


def _ret_call(p_rope, p_plain, decs, ret_norm_w, l):
    _, b, t_all, _ = p_rope.shape
    h = RET_HEADS
    chunk = _pick(l, (256, 128))
    return pl.pallas_call(
        functools.partial(_ret_kernel, chunk=chunk),
        grid=(b, h),
        in_specs=[pl.BlockSpec((None, 2, QK_DIM), lambda bb, hh: (hh, 0, 0)),
                  pl.BlockSpec((None, None, l, QK_DIM), lambda bb, hh: (hh, bb, 0, 0)),
                  pl.BlockSpec((None, None, t_all, QK_DIM), lambda bb, hh: (h + hh, bb, 0, 0)),
                  pl.BlockSpec((None, None, t_all, V_DIM), lambda bb, hh: (2 * DIFF_HEADS + hh, bb, 0, 0)),
                  pl.BlockSpec((None, None, l, V_DIM), lambda bb, hh: (2 * DIFF_HEADS + h + hh, bb, 0, 0)),
                  pl.BlockSpec((1, V_DIM), lambda bb, hh: (0, 0))],
        out_specs=pl.BlockSpec((None, l, V_DIM), lambda bb, hh: (bb, 0, hh)),
        out_shape=jax.ShapeDtypeStruct((b, l, RET_WIDTH), BF16),
        scratch_shapes=[pltpu.VMEM((l, V_DIM), F32)],
        compiler_params=_params(ARB, ARB),
        name="retention",
    )(decs, p_rope, p_rope, p_plain, p_plain, ret_norm_w.reshape(1, V_DIM))


def _outproj_kernel(ad_ref, ar_ref, wd_ref, wr_ref, x_ref, gate_ref, o_ref):
    y = jnp.dot(ad_ref[...], wd_ref[...], preferred_element_type=F32)
    y = y + jnp.dot(ar_ref[...], wr_ref[...], preferred_element_type=F32)
    o_ref[...] = x_ref[...] + gate_ref[...] * y


def _outproj_call(o_d, o_r, w_out, x, gate):
    b, l, d = x.shape
    kd, kr = o_d.shape[-1], o_r.shape[-1]
    tm = _pick(l, (1024, 512, 256, 128))
    tn = _pick(d, (512, 256, 128))
    per_b = l // tm
    return pl.pallas_call(
        _outproj_kernel,
        grid=(b * per_b, d // tn),
        in_specs=[pl.BlockSpec((tm, kd), lambda i, j: (i, 0)),
                  pl.BlockSpec((tm, kr), lambda i, j: (i, 0)),
                  pl.BlockSpec((kd, tn), lambda i, j: (0, j)),
                  pl.BlockSpec((kr, tn), lambda i, j: (kd // kr, j)),
                  pl.BlockSpec((tm, tn), lambda i, j: (i, j)),
                  pl.BlockSpec((None, 1, tn), lambda i, j: (i // per_b, 0, j))],
        out_specs=pl.BlockSpec((tm, tn), lambda i, j: (i, j)),
        out_shape=jax.ShapeDtypeStruct((b * l, d), F32),
        compiler_params=_params(ARB, ARB),
        name="outproj",
    )(o_d.reshape(b * l, kd), o_r.reshape(b * l, kr), w_out, w_out, x.reshape(b * l, d),
      gate.reshape(b, 1, d)).reshape(b, l, d)


def _rope_tables(l, lc):
    row, col = jnp.meshgrid(jnp.arange(l // GRID_W), jnp.arange(GRID_W), indexing="ij")
    row = row.reshape(-1).astype(F32)
    col = col.reshape(-1).astype(F32)
    half = QK_DIM // 2
    inv_freq = ROPE_BASE ** (-jnp.arange(0, half, 2, dtype=F32) / half)
    ang_r = row[:, None] * inv_freq
    ang_c = col[:, None] * inv_freq
    ang = jnp.concatenate([ang_r, ang_r, ang_c, ang_c], axis=-1)
    cos, sin = jnp.cos(ang), jnp.sin(ang)
    lo = (jnp.arange(QK_DIM) % half) < (half // 2)
    cos = jnp.concatenate([cos, jnp.ones((lc, QK_DIM), F32)], axis=0)
    sin_lo = jnp.concatenate([jnp.where(lo, -sin, 0.0), jnp.zeros((lc, QK_DIM), F32)], axis=0)
    sin_hi = jnp.concatenate([jnp.where(lo, 0.0, sin), jnp.zeros((lc, QK_DIM), F32)], axis=0)
    return cos, sin_lo, sin_hi


def _layer(x, ctx, c, c_ctx, norm_w, ada_w, ada_b, w_in, q_norm_w, k_norm_w, lam_params, subln_w,
           dec_f, dec_b, ret_norm_w, w_out):
    b, l, d = x.shape
    lc = ctx.shape[1]
    t_all = l + lc

    rows = -(-(b + 1) // 8) * 8
    cc = jnp.zeros((rows, d), F32).at[:b].set(c).at[b].set(c_ctx)
    mod = _ada_call(cc, ada_w, ada_b)
    shift, scale, gate = mod[:b, :d], mod[:b, d:2 * d], mod[:b, 2 * d:]
    shift_c = jnp.broadcast_to(mod[b, :d], (b, d))
    scale_c = jnp.broadcast_to(mod[b, d:2 * d], (b, d))
    mods = jnp.stack([scale, shift, scale_c, shift_c], axis=1)

    h_all = _norm_call(x, ctx, norm_w, mods).reshape(b * t_all, d)

    d_in = w_in.shape[1]
    w_qk = _cast_to_bf16(w_in, O_DQ, O_DV)
    tables = _rope_tables(l, lc)
    blocks = lambda lo, hi, base: list(range((lo - base) // INPROJ_TN, (hi - base) // INPROJ_TN))
    qk_vecs = jnp.stack([q_norm_w, k_norm_w]).reshape(2, 1, QK_DIM)
    ret_vecs = jnp.stack([jnp.ones((QK_DIM,), F32), jnp.full((QK_DIM,), QK_DIM ** -0.5, F32)]).reshape(2, 1, QK_DIM)
    call = functools.partial(_inproj_call, h_all, tables=tables, rows_per_batch=t_all)
    p_nr, w_rest = call(w=w_qk, vecs=qk_vecs, mode="normrope", col_blocks=blocks(O_DQ, O_DV, O_DQ), out_w=V_DIM,
                        cast=(w_in, O_DV, d_in - O_DV))
    p_rope, w_out_b = call(w=w_rest, vecs=ret_vecs, mode="rope", col_blocks=blocks(O_RQ, O_RV, O_DV), out_w=QK_DIM,
                           cast=(w_out, 0, w_out.shape[1]))
    p_plain = call(w=w_rest, vecs=None, mode="plain",
                   col_blocks=blocks(O_DV, O_RQ, O_DV) + blocks(O_RV, d_in, O_DV), out_w=V_DIM)
    p_nr = p_nr.reshape(-1, b, t_all, V_DIM)
    p_rope = p_rope.reshape(-1, b, t_all, QK_DIM)
    p_plain = p_plain.reshape(-1, b, t_all, V_DIM)

    o_d = _attn_call(p_nr, p_plain, lam_params, subln_w, l)
    decs = jnp.broadcast_to(jnp.stack([dec_f, dec_b], axis=1)[:, :, None], (RET_HEADS, 2, QK_DIM)).astype(F32)
    o_r = _ret_call(p_rope, p_plain, decs, ret_norm_w, l)

    return _outproj_call(o_d, o_r, w_out_b, x, gate)


def kernel(x, c, ctx, c_ctx, norm_w, ada_w, ada_b, w_in, diff_q_norm_w, diff_k_norm_w, diff_lambda_q1,
           diff_lambda_k1, diff_lambda_q2, diff_lambda_k2, diff_subln_w, ret_decay_fwd, ret_decay_bwd,
           ret_norm_w, w_out):
    depth = norm_w.shape[0]
    assert depth == 1, "LAM_INIT is the layer-0 value"
    for layer in range(depth):
        lam_params = jnp.stack([diff_lambda_q1[layer], diff_lambda_k1[layer],
                                diff_lambda_q2[layer], diff_lambda_k2[layer]]).astype(F32)
        x = _layer(x, ctx, c, c_ctx, norm_w[layer], ada_w[layer], ada_b[layer], w_in[layer],
                   diff_q_norm_w[layer], diff_k_norm_w[layer], lam_params, diff_subln_w[layer],
                   ret_decay_fwd[layer], ret_decay_bwd[layer], ret_norm_w[layer], w_out[layer])
    return x
```

```python
import functools
import math

import jax
import jax.numpy as jnp
from jax import lax
from jax.experimental import pallas as pl
from jax.experimental.pallas import tpu as pltpu

F32 = jnp.float32
BF16 = jnp.bfloat16

GRID_W = 64
DIFF_HEADS = 8
QK_DIM = 128
V_DIM = 256
RET_HEADS = 8
ROPE_BASE = 10000.0
EPS = 1e-6
LAM_INIT = 0.8 - 0.6 * math.exp(-0.3 * 0)

DIFF_QK_W = DIFF_HEADS * 2 * QK_DIM
DIFF_WIDTH = DIFF_HEADS * V_DIM
RET_QK_W = RET_HEADS * QK_DIM
RET_WIDTH = RET_HEADS * V_DIM
O_DQ = 0
O_DK = O_DQ + DIFF_QK_W
O_DV = O_DK + DIFF_QK_W
O_DG = O_DV + DIFF_WIDTH
O_RQ = O_DG + DIFF_WIDTH
O_RK = O_RQ + RET_QK_W
O_RV = O_RK + RET_QK_W
O_RG = O_RV + RET_WIDTH

V7X_VMEM_LIMIT_BYTES = 56 * 1024 * 1024
V7X_MXU_DIM = 256
ARB = "arbitrary"
RET_UNROLL = 8
INPROJ_TN = 1024
ATTN_TK = 1024
CAST_ROWS, CAST_COLS = 256, 2048


def _pick(n, candidates):
    for c in candidates:
        if n % c == 0:
            return c
    raise ValueError(f"no tile in {candidates} divides {n}")


def _params(*sem):
    return pltpu.CompilerParams(dimension_semantics=sem, vmem_limit_bytes=V7X_VMEM_LIMIT_BYTES)


def _ada_kernel(c_ref, w_ref, b_ref, o_ref):
    c = c_ref[...]
    a = (c * jax.nn.sigmoid(c)).astype(BF16)
    o_ref[...] = jnp.dot(a, w_ref[...].astype(BF16), preferred_element_type=F32) + b_ref[...]


def _ada_call(cc, ada_w, ada_b):
    rows, d = cc.shape
    n = ada_w.shape[1]
    tn = _pick(n, (512, 256, 128))
    return pl.pallas_call(
        _ada_kernel,
        grid=(n // tn,),
        in_specs=[pl.BlockSpec((rows, d), lambda j: (0, 0)),
                  pl.BlockSpec((d, tn), lambda j: (0, j)),
                  pl.BlockSpec((1, tn), lambda j: (0, j))],
        out_specs=pl.BlockSpec((rows, tn), lambda j: (0, j)),
        out_shape=jax.ShapeDtypeStruct((rows, n), F32),
        compiler_params=_params(ARB),
        name="ada_mod",
    )(cc, ada_w, ada_b.reshape(1, n))


def _norm_kernel(x_ref, ctx_ref, nw_ref, mod_ref, o_ref, *, n_lat):
    i = pl.program_id(1)

    def norm_mod(xv, scale, shift):
        ms = jnp.mean(xv * xv, axis=-1, keepdims=True)
        y = xv * lax.rsqrt(ms + EPS) * nw_ref[...]
        return (y * (1.0 + scale) + shift).astype(o_ref.dtype)

    @pl.when(i < n_lat)
    def _():
        o_ref[...] = norm_mod(x_ref[...], mod_ref[0:1, :], mod_ref[1:2, :])

    @pl.when(i >= n_lat)
    def _():
        o_ref[...] = norm_mod(ctx_ref[...], mod_ref[2:3, :], mod_ref[3:4, :])


def _norm_call(x, ctx, norm_w, mods):
    b, l, d = x.shape
    lc = ctx.shape[1]
    tr = _pick(math.gcd(l, lc), (256, 128, 64, 32, 16))
    n_lat, n_ctx = l // tr, lc // tr
    return pl.pallas_call(
        functools.partial(_norm_kernel, n_lat=n_lat),
        grid=(b, n_lat + n_ctx),
        in_specs=[pl.BlockSpec((None, tr, d), lambda bb, i: (bb, jnp.minimum(i, n_lat - 1), 0)),
                  pl.BlockSpec((None, tr, d), lambda bb, i: (bb, jnp.maximum(i - n_lat, 0), 0)),
                  pl.BlockSpec((1, d), lambda bb, i: (0, 0)),
                  pl.BlockSpec((None, 4, d), lambda bb, i: (bb, 0, 0))],
        out_specs=pl.BlockSpec((None, tr, d), lambda bb, i: (bb, i, 0)),
        out_shape=jax.ShapeDtypeStruct((b, l + lc, d), BF16),
        compiler_params=_params(ARB, ARB),
        name="norm_mod",
    )(x, ctx, norm_w.reshape(1, d), mods)


def _rope(xh, cos, sin_lo, sin_hi):
    return xh * cos + pltpu.roll(xh, 96, 1) * sin_lo + pltpu.roll(xh, 32, 1) * sin_hi


def _inproj_plain_kernel(cb_ref, a_ref, w_ref, o_ref, *, out_w):
    del cb_ref
    acc = jnp.dot(a_ref[...], w_ref[...], preferred_element_type=F32)
    for c in range(acc.shape[1] // out_w):
        o_ref[c] = acc[:, c * out_w:(c + 1) * out_w].astype(o_ref.dtype)


def _inproj_rope_kernel(cb_ref, a_ref, w_ref, cos_ref, slo_ref, shi_ref, vec_ref, *rest, norm, out_w, cast_steps):
    del cb_ref
    if cast_steps:
        cast_in_ref, o_ref, cast_out_ref, acc_sc = rest

        @pl.when(pl.program_id(0) < cast_steps)
        def _():
            cast_out_ref[...] = cast_in_ref[...].astype(cast_out_ref.dtype)
    else:
        o_ref, acc_sc = rest

    @pl.when(pl.program_id(0) == 0)
    def _():
        acc_sc[...] = jnp.zeros_like(acc_sc)

    cos, slo, shi, vec = cos_ref[...], slo_ref[...], shi_ref[...], vec_ref[...]
    per = out_w // QK_DIM
    for t in range(acc_sc.shape[1] // QK_DIM):
        xh = acc_sc[:, t * QK_DIM:(t + 1) * QK_DIM]
        if norm:
            xh = xh * lax.rsqrt(jnp.mean(xh * xh, axis=-1, keepdims=True) + EPS) * vec
            r = _rope(xh, cos, slo, shi)
        else:
            r = _rope(xh, cos, slo, shi) * vec
        lane0 = (t % per) * QK_DIM
        o_ref[t // per, :, lane0:lane0 + QK_DIM] = r.astype(o_ref.dtype)
    acc_sc[...] = jnp.dot(a_ref[...], w_ref[...], preferred_element_type=F32)


def _cast_to_bf16(src, col0, ncols):
    return src[:, col0:col0 + ncols].astype(BF16)


def _inproj_call(a2d, w, tables, vecs, *, mode, col_blocks, out_w, rows_per_batch, cast=None):
    m, d = a2d.shape
    tn = INPROJ_TN
    tm = _pick(rows_per_batch, (768, 512, 384, 256, 128))
    n_i, n_j = m // tm, len(col_blocks)
    n_out = n_j * tn // out_w
    col_tab = jnp.asarray(col_blocks, jnp.int32)
    out_shape = jax.ShapeDtypeStruct((n_out, m, out_w), BF16)
    if mode == "plain":
        return pl.pallas_call(
            functools.partial(_inproj_plain_kernel, out_w=out_w),
            grid_spec=pltpu.PrefetchScalarGridSpec(
                num_scalar_prefetch=1,
                grid=(n_i, n_j),
                in_specs=[pl.BlockSpec((tm, d), lambda i, j, cb: (i, 0)),
                          pl.BlockSpec((d, tn), lambda i, j, cb: (0, cb[j]))],
                out_specs=pl.BlockSpec((tn // out_w, tm, out_w), lambda i, j, cb: (j, i, 0)),
            ),
            out_shape=out_shape,
            compiler_params=_params(ARB, ARB),
            name="inproj_plain",
        )(col_tab, a2d, w)

    steps = n_i * n_j
    t_blocks = rows_per_batch // tm
    j_per_vec = n_j // vecs.shape[0]
    cur = lambda s: jnp.minimum(s, steps - 1)
    prev = lambda s: jnp.maximum(s - 1, 0)
    tab_spec = pl.BlockSpec((tm, QK_DIM), lambda s, cb: ((prev(s) // n_j) % t_blocks, 0))
    in_specs = [pl.BlockSpec((tm, d), lambda s, cb: (cur(s) // n_j, 0)),
                pl.BlockSpec((d, tn), lambda s, cb: (0, cb[cur(s) % n_j])),
                tab_spec, tab_spec, tab_spec,
                pl.BlockSpec((None, 1, QK_DIM), lambda s, cb: ((prev(s) % n_j) // j_per_vec, 0, 0))]
    out_specs = [pl.BlockSpec((tn // out_w, tm, out_w), lambda s, cb: (prev(s) % n_j, prev(s) // n_j, 0))]
    out_shapes = [out_shape]
    args = [a2d, w, *tables, vecs]

    cast_src, cast_col0, cast_ncols = cast
    cast_rows = cast_src.shape[0]
    n_c = cast_ncols // CAST_COLS
    cast_steps = (cast_rows // CAST_ROWS) * n_c
    ride = (cast_ncols % CAST_COLS == 0 and cast_col0 % CAST_COLS == 0 and cast_rows % CAST_ROWS == 0
            and cast_steps <= steps)
    if ride:
        blk = lambda s: jnp.minimum(s, cast_steps - 1)
        c0 = cast_col0 // CAST_COLS
        in_specs.append(pl.BlockSpec((CAST_ROWS, CAST_COLS), lambda s, cb: (blk(s) // n_c, c0 + blk(s) % n_c)))
        out_specs.append(pl.BlockSpec((CAST_ROWS, CAST_COLS), lambda s, cb: (blk(s) // n_c, blk(s) % n_c)))
        out_shapes.append(jax.ShapeDtypeStruct((cast_rows, cast_ncols), BF16))
        args.append(cast_src)
    res = pl.pallas_call(
        functools.partial(_inproj_rope_kernel, norm=(mode == "normrope"), out_w=out_w,
                          cast_steps=cast_steps if ride else 0),
        grid_spec=pltpu.PrefetchScalarGridSpec(
            num_scalar_prefetch=1,
            grid=(steps + 1,),
            in_specs=in_specs,
            out_specs=out_specs,
            scratch_shapes=[pltpu.VMEM((tm, tn), F32)],
        ),
        out_shape=out_shapes,
        compiler_params=_params(ARB),
        name="inproj_" + mode,
    )(col_tab, *args)
    if ride:
        return res[0], res[1]
    return res[0], _cast_to_bf16(cast_src, cast_col0, cast_ncols)


def _attn_kernel(lamp_ref, q_ref, k_ref, v_ref, g_ref, sw_ref, o_ref, vt_sc, vt_tail_sc, s_sc, p_sc, acc_sc, *, tk):
    t_all = k_ref.shape[0]
    tq = q_ref.shape[0]
    n_big, tail = divmod(t_all, tk)
    n = n_big + (1 if tail else 0)
    size = lambda c: tk if c < n_big else tail
    c_exp = (QK_DIM ** -0.5) * math.log2(math.e)

    def transposed(rows):
        return v_ref[rows, :].astype(F32).T.astype(BF16)

    @pl.when(pl.program_id(2) == 0)
    def _():
        def transpose_chunk(c, carry):
            vt_sc[c] = transposed(pl.ds(pl.multiple_of(c * tk, tk), tk))
            return carry
        lax.fori_loop(0, n_big, transpose_chunk, 0)
        if tail:
            vt_tail_sc[...] = transposed(pl.ds(n_big * tk, tail))

    qt = q_ref[...].astype(F32).T.astype(BF16)
    qts = (qt[:QK_DIM], qt[QK_DIM:])
    acc_sc[...] = jnp.zeros_like(acc_sc)

    def stage_qk(c, rows):
        start = c * tk if isinstance(c, int) else pl.multiple_of(c * tk, tk)
        kc = k_ref[pl.ds(start, rows), :]
        cms = []
        for sub in range(2):
            s = jnp.dot(kc[:, sub * QK_DIM:(sub + 1) * QK_DIM], qts[sub],
                        preferred_element_type=F32)
            s_sc[sub, :rows] = s
            cms.append(jnp.max(s, axis=0, keepdims=True))
        return tuple(cms)

    def stage_softmax(rows, cms, ms, ls):
        new_m, new_l, alphas = [], [], []
        for sub in range(2):
            m_new = jnp.maximum(ms[sub], cms[sub])
            alpha = jnp.exp2((ms[sub] - m_new) * c_exp)
            p = jnp.exp2(s_sc[sub, :rows] * c_exp - m_new * c_exp)
            new_l.append(alpha * ls[sub] + jnp.sum(p.reshape(rows // 8, 8, tq), axis=0))
            p_sc[sub, :rows] = p.astype(BF16)
            new_m.append(m_new)
            alphas.append(alpha)
        return tuple(new_m), tuple(new_l), tuple(alphas)

    def stage_pv(c, rows, alphas):
        vt = vt_sc[c] if rows == tk else vt_tail_sc[...]
        for sub in range(2):
            acc_sc[sub] = alphas[sub] * acc_sc[sub] + jnp.dot(vt, p_sc[sub, :rows], preferred_element_type=F32)

    def tick(t, carry):
        cms, ms, ls, alphas = carry
        static = isinstance(t, int)
        if not static or 0 <= t - 2 < n:
            stage_pv(t - 2, size(t - 2) if static else tk, alphas)
        if not static or 0 <= t - 1 < n:
            ms, ls, alphas = stage_softmax(size(t - 1) if static else tk, cms, ms, ls)
        if not static or t < n:
            cms = stage_qk(t, size(t) if static else tk)
        return cms, ms, ls, alphas

    neg = jnp.full((1, tq), -jnp.inf, F32)
    zero = jnp.zeros((1, tq), F32)
    carry = ((neg, neg), (neg, neg), (jnp.zeros((8, tq), F32),) * 2, (zero, zero))
    loop_lo, loop_hi = 2, max(2, n_big)
    for t in range(loop_lo):
        carry = tick(t, carry)
    carry = lax.fori_loop(loop_lo, loop_hi, tick, carry)
    for t in range(loop_hi, n + 2):
        carry = tick(t, carry)
    _, ms, ls, _ = carry

    lp = lamp_ref[...]
    lam = (jnp.exp(jnp.sum(lp[0:1] * lp[1:2], axis=1, keepdims=True))
           - jnp.exp(jnp.sum(lp[2:3] * lp[3:4], axis=1, keepdims=True)) + LAM_INIT)
    inv1 = 1.0 / jnp.sum(ls[0], axis=0, keepdims=True)
    inv2 = lam / jnp.sum(ls[1], axis=0, keepdims=True)
    o = (acc_sc[0] * inv1 - acc_sc[1] * inv2).T
    y = o * lax.rsqrt(jnp.mean(o * o, axis=-1, keepdims=True) + EPS) * sw_ref[...] * (1.0 - LAM_INIT)
    g = g_ref[...].astype(F32)
    o_ref[...] = (y * (g * jax.nn.sigmoid(g))).astype(o_ref.dtype)


def _attn_call(p_nr, p_plain, lam_params, subln_w, l):
    _, b, t_all, _ = p_nr.shape
    tq = _pick(l, (1024, 512, 256, 128))
    tk = ATTN_TK
    n_big, tail = divmod(t_all, tk)
    assert n_big >= 2 and tail % V7X_MXU_DIM == 0, "needs two full key chunks and a tail of whole MXU passes"
    h = DIFF_HEADS
    return pl.pallas_call(
        functools.partial(_attn_kernel, tk=tk),
        grid=(b, h, l // tq),
        in_specs=[pl.BlockSpec((4, QK_DIM), lambda bb, hh, i: (0, 0)),
                  pl.BlockSpec((None, None, tq, V_DIM), lambda bb, hh, i: (hh, bb, i, 0)),
                  pl.BlockSpec((None, None, t_all, V_DIM), lambda bb, hh, i: (h + hh, bb, 0, 0)),
                  pl.BlockSpec((None, None, t_all, V_DIM), lambda bb, hh, i: (hh, bb, 0, 0)),
                  pl.BlockSpec((None, None, tq, V_DIM), lambda bb, hh, i: (h + hh, bb, i, 0)),
                  pl.BlockSpec((1, V_DIM), lambda bb, hh, i: (0, 0))],
        out_specs=pl.BlockSpec((None, tq, V_DIM), lambda bb, hh, i: (bb, i, hh)),
        out_shape=jax.ShapeDtypeStruct((b, l, DIFF_WIDTH), BF16),
        scratch_shapes=[pltpu.VMEM((n_big, V_DIM, tk), BF16),
                        pltpu.VMEM((V_DIM, max(tail, V7X_MXU_DIM)), BF16),
                        pltpu.VMEM((2, tk, tq), F32),
                        pltpu.VMEM((2, tk, tq), BF16),
                        pltpu.VMEM((2, V_DIM, tq), F32)],
        compiler_params=_params(ARB, ARB, ARB),
        name="diff_attn",
    )(lam_params, p_nr, p_nr, p_plain, p_plain, subln_w.reshape(1, V_DIM))


def _ret_kernel(dec_ref, q_ref, k_ref, v_ref, g_ref, nw_ref, o_ref, ob_sc, *, chunk):
    l = q_ref.shape[0]
    lc = k_ref.shape[0] - l
    nc = l // chunk
    cdim = (((1,), (1,)), ((), ()))
    rdim = (((0,), (0,)), ((), ()))

    dec = dec_ref[...]
    lg = jnp.log1p(-jnp.exp2(-dec))
    lg_f, lg_b = lg[0:1, 0:1], lg[1:2, 0:1]

    def col_iota(n):
        return lax.broadcasted_iota(jnp.int32, (n, 1), 0).astype(F32)

    ic = col_iota(chunk)
    xi_f = jnp.exp(lg_f * (ic + 1.0))
    xi_b = jnp.exp(lg_b * (chunk - ic))
    zeta_f = jnp.exp(lg_f * (chunk - 1.0 - ic))
    zeta_b = jnp.exp(lg_b * ic)
    gc_f = jnp.exp(lg_f * chunk)
    gc_b = jnp.exp(lg_b * chunk)
    ri = lax.broadcasted_iota(jnp.int32, (chunk, chunk), 0)
    ci = lax.broadcasted_iota(jnp.int32, (chunk, chunk), 1)
    dist = (ri - ci).astype(F32)
    dmat = jnp.where(ri >= ci, jnp.exp(lg_f * jnp.maximum(dist, 0.0)), jnp.exp(lg_b * jnp.maximum(-dist, 0.0)))

    im = col_iota(lc)
    kctx = k_ref[pl.ds(l, lc), :].astype(F32)
    vctx = v_ref[pl.ds(l, lc), :]
    r0_f = lax.dot_general((kctx * jnp.exp(lg_f * (lc - 1.0 - im))).astype(BF16), vctx, rdim,
                           preferred_element_type=F32)
    r0_b = lax.dot_general((kctx * jnp.exp(lg_b * im)).astype(BF16), vctx, rdim, preferred_element_type=F32)

    def chunk_kv(rows, zeta):
        kz = (k_ref[rows, :].astype(F32) * zeta).astype(BF16)
        return lax.dot_general(kz, v_ref[rows, :], rdim, preferred_element_type=F32)

    def bwd_body(j, s_b):
        c = nc - 1 - j
        rows = pl.ds(pl.multiple_of(c * chunk, chunk), chunk)
        cross = jnp.dot(q_ref[rows, :], s_b.astype(BF16), preferred_element_type=F32)
        ob_sc[rows, :] = cross * xi_b
        return gc_b * s_b + chunk_kv(rows, zeta_b)

    lax.fori_loop(0, nc, bwd_body, r0_b, unroll=RET_UNROLL)

    def fwd_body(c, s_f):
        rows = pl.ds(pl.multiple_of(c * chunk, chunk), chunk)
        qc = q_ref[rows, :]
        scores = lax.dot_general(qc, k_ref[rows, :], cdim, preferred_element_type=F32) * dmat
        o = jnp.dot(scores.astype(BF16), v_ref[rows, :], preferred_element_type=F32)
        o = o + jnp.dot(qc, s_f.astype(BF16), preferred_element_type=F32) * xi_f + ob_sc[rows, :]
        y = o * lax.rsqrt(jnp.mean(o * o, axis=-1, keepdims=True) + EPS) * nw_ref[...]
        g = g_ref[rows, :].astype(F32)
        o_ref[rows, :] = (y * (g * jax.nn.sigmoid(g))).astype(o_ref.dtype)
        return gc_f * s_f + chunk_kv(rows, zeta_f)

    lax.fori_loop(0, nc, fwd_body, r0_f, unroll=RET_UNROLL)


def _ret_call(p_rope, p_plain, decs, ret_norm_w, l):
    _, b, t_all, _ = p_rope.shape
    h = RET_HEADS
    chunk = _pick(l, (256, 128))
    return pl.pallas_call(
        functools.partial(_ret_kernel, chunk=chunk),
        grid=(b, h),
        in_specs=[pl.BlockSpec((None, 2, QK_DIM), lambda bb, hh: (hh, 0, 0)),
                  pl.BlockSpec((None, None, l, QK_DIM), lambda bb, hh: (hh, bb, 0, 0)),
                  pl.BlockSpec((None, None, t_all, QK_DIM), lambda bb, hh: (h + hh, bb, 0, 0)),
                  pl.BlockSpec((None, None, t_all, V_DIM), lambda bb, hh: (2 * DIFF_HEADS + hh, bb, 0, 0)),
                  pl.BlockSpec((None, None, l, V_DIM), lambda bb, hh: (2 * DIFF_HEADS + h + hh, bb, 0, 0)),
                  pl.BlockSpec((1, V_DIM), lambda bb, hh: (0, 0))],
        out_specs=pl.BlockSpec((None, l, V_DIM), lambda bb, hh: (bb, 0, hh)),
        out_shape=jax.ShapeDtypeStruct((b, l, RET_WIDTH), BF16),
        scratch_shapes=[pltpu.VMEM((l, V_DIM), F32)],
        compiler_params=_params(ARB, ARB),
        name="retention",
    )(decs, p_rope, p_rope, p_plain, p_plain, ret_norm_w.reshape(1, V_DIM))


def _outproj_kernel(ad_ref, ar_ref, wd_ref, wr_ref, x_ref, gate_ref, o_ref):
    y = jnp.dot(ad_ref[...], wd_ref[...], preferred_element_type=F32)
    y = y + jnp.dot(ar_ref[...], wr_ref[...], preferred_element_type=F32)
    o_ref[...] = x_ref[...] + gate_ref[...] * y


def _outproj_call(o_d, o_r, w_out, x, gate):
    b, l, d = x.shape
    kd, kr = o_d.shape[-1], o_r.shape[-1]
    tm = _pick(l, (1024, 512, 256, 128))
    tn = _pick(d, (512, 256, 128))
    per_b = l // tm
    return pl.pallas_call(
        _outproj_kernel,
        grid=(b * per_b, d // tn),
        in_specs=[pl.BlockSpec((tm, kd), lambda i, j: (i, 0)),
                  pl.BlockSpec((tm, kr), lambda i, j: (i, 0)),
                  pl.BlockSpec((kd, tn), lambda i, j: (0, j)),
                  pl.BlockSpec((kr, tn), lambda i, j: (kd // kr, j)),
                  pl.BlockSpec((tm, tn), lambda i, j: (i, j)),
                  pl.BlockSpec((None, 1, tn), lambda i, j: (i // per_b, 0, j))],
        out_specs=pl.BlockSpec((tm, tn), lambda i, j: (i, j)),
        out_shape=jax.ShapeDtypeStruct((b * l, d), F32),
        compiler_params=_params(ARB, ARB),
        name="outproj",
    )(o_d.reshape(b * l, kd), o_r.reshape(b * l, kr), w_out, w_out, x.reshape(b * l, d),
      gate.reshape(b, 1, d)).reshape(b, l, d)


def _rope_tables(l, lc):
    row, col = jnp.meshgrid(jnp.arange(l // GRID_W), jnp.arange(GRID_W), indexing="ij")
    row = row.reshape(-1).astype(F32)
    col = col.reshape(-1).astype(F32)
    half = QK_DIM // 2
    inv_freq = ROPE_BASE ** (-jnp.arange(0, half, 2, dtype=F32) / half)
    ang_r = row[:, None] * inv_freq
    ang_c = col[:, None] * inv_freq
    ang = jnp.concatenate([ang_r, ang_r, ang_c, ang_c], axis=-1)
    cos, sin = jnp.cos(ang), jnp.sin(ang)
    lo = (jnp.arange(QK_DIM) % half) < (half // 2)
    cos = jnp.concatenate([cos, jnp.ones((lc, QK_DIM), F32)], axis=0)
    sin_lo = jnp.concatenate([jnp.where(lo, -sin, 0.0), jnp.zeros((lc, QK_DIM), F32)], axis=0)
    sin_hi = jnp.concatenate([jnp.where(lo, 0.0, sin), jnp.zeros((lc, QK_DIM), F32)], axis=0)
    return cos, sin_lo, sin_hi


def _layer(x, ctx, c, c_ctx, norm_w, ada_w, ada_b, w_in, q_norm_w, k_norm_w, lam_params, subln_w,
           dec_f, dec_b, ret_norm_w, w_out):
    b, l, d = x.shape
    lc = ctx.shape[1]
    t_all = l + lc

    rows = -(-(b + 1) // 8) * 8
    cc = jnp.zeros((rows, d), F32).at[:b].set(c).at[b].set(c_ctx)
    mod = _ada_call(cc, ada_w, ada_b)
    shift, scale, gate = mod[:b, :d], mod[:b, d:2 * d], mod[:b, 2 * d:]
    shift_c = jnp.broadcast_to(mod[b, :d], (b, d))
    scale_c = jnp.broadcast_to(mod[b, d:2 * d], (b, d))
    mods = jnp.stack([scale, shift, scale_c, shift_c], axis=1)

    h_all = _norm_call(x, ctx, norm_w, mods).reshape(b * t_all, d)

    d_in = w_in.shape[1]
    w_qk = _cast_to_bf16(w_in, O_DQ, O_DV)
    tables = _rope_tables(l, lc)
    blocks = lambda lo, hi, base: list(range((lo - base) // INPROJ_TN, (hi - base) // INPROJ_TN))
    qk_vecs = jnp.stack([q_norm_w, k_norm_w]).reshape(2, 1, QK_DIM)
    ret_vecs = jnp.stack([jnp.ones((QK_DIM,), F32), jnp.full((QK_DIM,), QK_DIM ** -0.5, F32)]).reshape(2, 1, QK_DIM)
    call = functools.partial(_inproj_call, h_all, tables=tables, rows_per_batch=t_all)
    p_nr, w_rest = call(w=w_qk, vecs=qk_vecs, mode="normrope", col_blocks=blocks(O_DQ, O_DV, O_DQ), out_w=V_DIM,
                        cast=(w_in, O_DV, d_in - O_DV))
    p_rope, w_out_b = call(w=w_rest, vecs=ret_vecs, mode="rope", col_blocks=blocks(O_RQ, O_RV, O_DV), out_w=QK_DIM,
                           cast=(w_out, 0, w_out.shape[1]))
    p_plain = call(w=w_rest, vecs=None, mode="plain",
                   col_blocks=blocks(O_DV, O_RQ, O_DV) + blocks(O_RV, d_in, O_DV), out_w=V_DIM)
    p_nr = p_nr.reshape(-1, b, t_all, V_DIM)
    p_rope = p_rope.reshape(-1, b, t_all, QK_DIM)
    p_plain = p_plain.reshape(-1, b, t_all, V_DIM)

    o_d = _attn_call(p_nr, p_plain, lam_params, subln_w, l)
    decs = jnp.broadcast_to(jnp.stack([dec_f, dec_b], axis=1)[:, :, None], (RET_HEADS, 2, QK_DIM)).astype(F32)
    o_r = _ret_call(p_rope, p_plain, decs, ret_norm_w, l)

    return _outproj_call(o_d, o_r, w_out_b, x, gate)


def kernel(x, c, ctx, c_ctx, norm_w, ada_w, ada_b, w_in, diff_q_norm_w, diff_k_norm_w, diff_lambda_q1,
           diff_lambda_k1, diff_lambda_q2, diff_lambda_k2, diff_subln_w, ret_decay_fwd, ret_decay_bwd,
           ret_norm_w, w_out):
    depth = norm_w.shape[0]
    assert depth == 1, "LAM_INIT is the layer-0 value"
    for layer in range(depth):
        lam_params = jnp.stack([diff_lambda_q1[layer], diff_lambda_k1[layer],
                                diff_lambda_q2[layer], diff_lambda_k2[layer]]).astype(F32)
        x = _layer(x, ctx, c, c_ctx, norm_w[layer], ada_w[layer], ada_b[layer], w_in[layer],
                   diff_q_norm_w[layer], diff_k_norm_w[layer], lam_params, diff_subln_w[layer],
                   ret_decay_fwd[layer], ret_decay_bwd[layer], ret_norm_w[layer], w_out[layer])
    return x
```

```python
import functools
import math

import jax
import jax.numpy as jnp
from jax import lax
from jax.experimental import pallas as pl
from jax.experimental.pallas import tpu as pltpu

F32 = jnp.float32
BF16 = jnp.bfloat16

GRID_W = 64
DIFF_HEADS = 8
QK_DIM = 128
V_DIM = 256
RET_HEADS = 8
ROPE_BASE = 10000.0
EPS = 1e-6
LAM_INIT = 0.8 - 0.6 * math.exp(-0.3 * 0)

DIFF_QK_W = DIFF_HEADS * 2 * QK_DIM
DIFF_WIDTH = DIFF_HEADS * V_DIM
RET_QK_W = RET_HEADS * QK_DIM
RET_WIDTH = RET_HEADS * V_DIM
O_DQ = 0
O_DK = O_DQ + DIFF_QK_W
O_DV = O_DK + DIFF_QK_W
O_DG = O_DV + DIFF_WIDTH
O_RQ = O_DG + DIFF_WIDTH
O_RK = O_RQ + RET_QK_W
O_RV = O_RK + RET_QK_W
O_RG = O_RV + RET_WIDTH

V7X_VMEM_LIMIT_BYTES = 56 * 1024 * 1024
V7X_MXU_DIM = 256
ARB = "arbitrary"
RET_UNROLL = 8
INPROJ_TN = 1024
ATTN_TK = 1024
CAST_ROWS, CAST_COLS = 256, 2048


def _pick(n, candidates):
    for c in candidates:
        if n % c == 0:
            return c
    raise ValueError(f"no tile in {candidates} divides {n}")


def _params(*sem):
    return pltpu.CompilerParams(dimension_semantics=sem, vmem_limit_bytes=V7X_VMEM_LIMIT_BYTES)


def _ada_kernel(c_ref, w_ref, b_ref, o_ref):
    c = c_ref[...]
    a = (c * jax.nn.sigmoid(c)).astype(BF16)
    o_ref[...] = jnp.dot(a, w_ref[...].astype(BF16), preferred_element_type=F32) + b_ref[...]


def _ada_call(cc, ada_w, ada_b):
    rows, d = cc.shape
    n = ada_w.shape[1]
    tn = _pick(n, (512, 256, 128))
    return pl.pallas_call(
        _ada_kernel,
        grid=(n // tn,),
        in_specs=[pl.BlockSpec((rows, d), lambda j: (0, 0)),
                  pl.BlockSpec((d, tn), lambda j: (0, j)),
                  pl.BlockSpec((1, tn), lambda j: (0, j))],
        out_specs=pl.BlockSpec((rows, tn), lambda j: (0, j)),
        out_shape=jax.ShapeDtypeStruct((rows, n), F32),
        compiler_params=_params(ARB),
        name="ada_mod",
    )(cc, ada_w, ada_b.reshape(1, n))


def _norm_kernel(x_ref, ctx_ref, nw_ref, mod_ref, o_ref, *, n_lat):
    i = pl.program_id(1)

    def norm_mod(xv, scale, shift):
        ms = jnp.mean(xv * xv, axis=-1, keepdims=True)
        y = xv * lax.rsqrt(ms + EPS) * nw_ref[...]
        return (y * (1.0 + scale) + shift).astype(o_ref.dtype)

    @pl.when(i < n_lat)
    def _():
        o_ref[...] = norm_mod(x_ref[...], mod_ref[0:1, :], mod_ref[1:2, :])

    @pl.when(i >= n_lat)
    def _():
        o_ref[...] = norm_mod(ctx_ref[...], mod_ref[2:3, :], mod_ref[3:4, :])


def _norm_call(x, ctx, norm_w, mods):
    b, l, d = x.shape
    lc = ctx.shape[1]
    tr = _pick(math.gcd(l, lc), (256, 128, 64, 32, 16))
    n_lat, n_ctx = l // tr, lc // tr
    return pl.pallas_call(
        functools.partial(_norm_kernel, n_lat=n_lat),
        grid=(b, n_lat + n_ctx),
        in_specs=[pl.BlockSpec((None, tr, d), lambda bb, i: (bb, jnp.minimum(i, n_lat - 1), 0)),
                  pl.BlockSpec((None, tr, d), lambda bb, i: (bb, jnp.maximum(i - n_lat, 0), 0)),
                  pl.BlockSpec((1, d), lambda bb, i: (0, 0)),
                  pl.BlockSpec((None, 4, d), lambda bb, i: (bb, 0, 0))],
        out_specs=pl.BlockSpec((None, tr, d), lambda bb, i: (bb, i, 0)),
        out_shape=jax.ShapeDtypeStruct((b, l + lc, d), BF16),
        compiler_params=_params(ARB, ARB),
        name="norm_mod",
    )(x, ctx, norm_w.reshape(1, d), mods)


def _rope(xh, cos, sin_lo, sin_hi):
    return xh * cos + pltpu.roll(xh, 96, 1) * sin_lo + pltpu.roll(xh, 32, 1) * sin_hi


def _inproj_plain_kernel(cb_ref, a_ref, w_ref, o_ref, *, out_w):
    del cb_ref
    acc = jnp.dot(a_ref[...], w_ref[...], preferred_element_type=F32)
    for c in range(acc.shape[1] // out_w):
        o_ref[c] = acc[:, c * out_w:(c + 1) * out_w].astype(o_ref.dtype)


def _inproj_rope_kernel(cb_ref, a_ref, w_ref, cos_ref, slo_ref, shi_ref, vec_ref, *rest, norm, out_w, cast_steps):
    del cb_ref
    if cast_steps:
        cast_in_ref, o_ref, cast_out_ref, acc_sc = rest

        @pl.when(pl.program_id(0) < cast_steps)
        def _():
            cast_out_ref[...] = cast_in_ref[...].astype(cast_out_ref.dtype)
    else:
        o_ref, acc_sc = rest

    @pl.when(pl.program_id(0) == 0)
    def _():
        acc_sc[...] = jnp.zeros_like(acc_sc)

    cos, slo, shi, vec = cos_ref[...], slo_ref[...], shi_ref[...], vec_ref[...]
    per = out_w // QK_DIM
    for t in range(acc_sc.shape[1] // QK_DIM):
        xh = acc_sc[:, t * QK_DIM:(t + 1) * QK_DIM]
        if norm:
            xh = xh * lax.rsqrt(jnp.mean(xh * xh, axis=-1, keepdims=True) + EPS) * vec
            r = _rope(xh, cos, slo, shi)
        else:
            r = _rope(xh, cos, slo, shi) * vec
        lane0 = (t % per) * QK_DIM
        o_ref[t // per, :, lane0:lane0 + QK_DIM] = r.astype(o_ref.dtype)
    acc_sc[...] = jnp.dot(a_ref[...], w_ref[...], preferred_element_type=F32)


def _cast_to_bf16(src, col0, ncols):
    return src[:, col0:col0 + ncols].astype(BF16)


def _inproj_call(a2d, w, tables, vecs, *, mode, col_blocks, out_w, rows_per_batch, cast=None):
    m, d = a2d.shape
    tn = INPROJ_TN
    tm = _pick(rows_per_batch, (768, 512, 384, 256, 128))
    n_i, n_j = m // tm, len(col_blocks)
    n_out = n_j * tn // out_w
    col_tab = jnp.asarray(col_blocks, jnp.int32)
    out_shape = jax.ShapeDtypeStruct((n_out, m, out_w), BF16)
    if mode == "plain":
        return pl.pallas_call(
            functools.partial(_inproj_plain_kernel, out_w=out_w),
            grid_spec=pltpu.PrefetchScalarGridSpec(
                num_scalar_prefetch=1,
                grid=(n_i, n_j),
                in_specs=[pl.BlockSpec((tm, d), lambda i, j, cb: (i, 0)),
                          pl.BlockSpec((d, tn), lambda i, j, cb: (0, cb[j]))],
                out_specs=pl.BlockSpec((tn // out_w, tm, out_w), lambda i, j, cb: (j, i, 0)),
            ),
            out_shape=out_shape,
            compiler_params=_params(ARB, ARB),
            name="inproj_plain",
        )(col_tab, a2d, w)

    steps = n_i * n_j
    t_blocks = rows_per_batch // tm
    j_per_vec = n_j // vecs.shape[0]
    cur = lambda s: jnp.minimum(s, steps - 1)
    prev = lambda s: jnp.maximum(s - 1, 0)
    tab_spec = pl.BlockSpec((tm, QK_DIM), lambda s, cb: ((prev(s) // n_j) % t_blocks, 0))
    in_specs = [pl.BlockSpec((tm, d), lambda s, cb: (cur(s) // n_j, 0)),
                pl.BlockSpec((d, tn), lambda s, cb: (0, cb[cur(s) % n_j])),
                tab_spec, tab_spec, tab_spec,
                pl.BlockSpec((None, 1, QK_DIM), lambda s, cb: ((prev(s) % n_j) // j_per_vec, 0, 0))]
    out_specs = [pl.BlockSpec((tn // out_w, tm, out_w), lambda s, cb: (prev(s) % n_j, prev(s) // n_j, 0))]
    out_shapes = [out_shape]
    args = [a2d, w, *tables, vecs]

    cast_src, cast_col0, cast_ncols = cast
    cast_rows = cast_src.shape[0]
    n_c = cast_ncols // CAST_COLS
    cast_steps = (cast_rows // CAST_ROWS) * n_c
    ride = (cast_ncols % CAST_COLS == 0 and cast_col0 % CAST_COLS == 0 and cast_rows % CAST_ROWS == 0
            and cast_steps <= steps)
    if ride:
        blk = lambda s: jnp.minimum(s, cast_steps - 1)
        c0 = cast_col0 // CAST_COLS
        in_specs.append(pl.BlockSpec((CAST_ROWS, CAST_COLS), lambda s, cb: (blk(s) // n_c, c0 + blk(s) % n_c)))
        out_specs.append(pl.BlockSpec((CAST_ROWS, CAST_COLS), lambda s, cb: (blk(s) // n_c, blk(s) % n_c)))
        out_shapes.append(jax.ShapeDtypeStruct((cast_rows, cast_ncols), BF16))
        args.append(cast_src)
    res = pl.pallas_call(
        functools.partial(_inproj_rope_kernel, norm=(mode == "normrope"), out_w=out_w,
                          cast_steps=cast_steps if ride else 0),
        grid_spec=pltpu.PrefetchScalarGridSpec(
            num_scalar_prefetch=1,
            grid=(steps + 1,),
            in_specs=in_specs,
            out_specs=out_specs,
            scratch_shapes=[pltpu.VMEM((tm, tn), F32)],
        ),
        out_shape=out_shapes,
        compiler_params=_params(ARB),
        name="inproj_" + mode,
    )(col_tab, *args)
    if ride:
        return res[0], res[1]
    return res[0], _cast_to_bf16(cast_src, cast_col0, cast_ncols)


def _attn_kernel(lamp_ref, q_ref, k_ref, v_ref, g_ref, sw_ref, o_ref, vt_sc, vt_tail_sc, s_sc, p_sc, acc_sc, *, tk):
    t_all = k_ref.shape[0]
    tq = q_ref.shape[0]
    n_big, tail = divmod(t_all, tk)
    n = n_big + (1 if tail else 0)
    size = lambda c: tk if c < n_big else tail
    c_exp = (QK_DIM ** -0.5) * math.log2(math.e)

    def transposed(rows):
        return v_ref[rows, :].astype(F32).T.astype(BF16)

    @pl.when(pl.program_id(2) == 0)
    def _():
        def transpose_chunk(c, carry):
            vt_sc[c] = transposed(pl.ds(pl.multiple_of(c * tk, tk), tk))
            return carry
        lax.fori_loop(0, n_big, transpose_chunk, 0)
        if tail:
            vt_tail_sc[...] = transposed(pl.ds(n_big * tk, tail))

    qt = q_ref[...].astype(F32).T.astype(BF16)
    qts = (qt[:QK_DIM], qt[QK_DIM:])
    acc_sc[...] = jnp.zeros_like(acc_sc)

    def stage_qk(c, rows):
        start = c * tk if isinstance(c, int) else pl.multiple_of(c * tk, tk)
        kc = k_ref[pl.ds(start, rows), :]
        cms = []
        for sub in range(2):
            s = jnp.dot(kc[:, sub * QK_DIM:(sub + 1) * QK_DIM], qts[sub],
                        preferred_element_type=F32)
            s_sc[sub, :rows] = s
            cms.append(jnp.max(s, axis=0, keepdims=True))
        return tuple(cms)

    def stage_softmax(rows, cms, ms, ls):
        new_m, new_l, alphas = [], [], []
        for sub in range(2):
            m_new = jnp.maximum(ms[sub], cms[sub])
            alpha = jnp.exp2((ms[sub] - m_new) * c_exp)
            p = jnp.exp2(s_sc[sub, :rows] * c_exp - m_new * c_exp)
            new_l.append(alpha * ls[sub] + jnp.sum(p.reshape(rows // 8, 8, tq), axis=0))
            p_sc[sub, :rows] = p.astype(BF16)
            new_m.append(m_new)
            alphas.append(alpha)
        return tuple(new_m), tuple(new_l), tuple(alphas)

    def stage_pv(c, rows, alphas):
        vt = vt_sc[c] if rows == tk else vt_tail_sc[...]
        for sub in range(2):
            acc_sc[sub] = alphas[sub] * acc_sc[sub] + jnp.dot(vt, p_sc[sub, :rows], preferred_element_type=F32)

    def tick(t, carry):
        cms, ms, ls, alphas = carry
        static = isinstance(t, int)
        if not static or 0 <= t - 2 < n:
            stage_pv(t - 2, size(t - 2) if static else tk, alphas)
        if not static or 0 <= t - 1 < n:
            ms, ls, alphas = stage_softmax(size(t - 1) if static else tk, cms, ms, ls)
        if not static or t < n:
            cms = stage_qk(t, size(t) if static else tk)
        return cms, ms, ls, alphas

    neg = jnp.full((1, tq), -jnp.inf, F32)
    zero = jnp.zeros((1, tq), F32)
    carry = ((neg, neg), (neg, neg), (jnp.zeros((8, tq), F32),) * 2, (zero, zero))
    loop_lo, loop_hi = 2, max(2, n_big)
    for t in range(loop_lo):
        carry = tick(t, carry)
    carry = lax.fori_loop(loop_lo, loop_hi, tick, carry)
    for t in range(loop_hi, n + 2):
        carry = tick(t, carry)
    _, ms, ls, _ = carry

    lp = lamp_ref[...]
    lam = (jnp.exp(jnp.sum(lp[0:1] * lp[1:2], axis=1, keepdims=True))
           - jnp.exp(jnp.sum(lp[2:3] * lp[3:4], axis=1, keepdims=True)) + LAM_INIT)
    inv1 = 1.0 / jnp.sum(ls[0], axis=0, keepdims=True)
    inv2 = lam / jnp.sum(ls[1], axis=0, keepdims=True)
    o = (acc_sc[0] * inv1 - acc_sc[1] * inv2).T
    y = o * lax.rsqrt(jnp.mean(o * o, axis=-1, keepdims=True) + EPS) * sw_ref[...] * (1.0 - LAM_INIT)
    g = g_ref[...].astype(F32)
    o_ref[...] = (y * (g * jax.nn.sigmoid(g))).astype(o_ref.dtype)


def _attn_call(p_nr, p_plain, lam_params, subln_w, l):
    _, b, t_all, _ = p_nr.shape
    tq = _pick(l, (1024, 512, 256, 128))
    tk = ATTN_TK
    n_big, tail = divmod(t_all, tk)
    assert n_big >= 2 and tail % V7X_MXU_DIM == 0, "needs two full key chunks and a tail of whole MXU passes"
    h = DIFF_HEADS
    return pl.pallas_call(
        functools.partial(_attn_kernel, tk=tk),
        grid=(b, h, l // tq),
        in_specs=[pl.BlockSpec((4, QK_DIM), lambda bb, hh, i: (0, 0)),
                  pl.BlockSpec((None, None, tq, V_DIM), lambda bb, hh, i: (hh, bb, i, 0)),
                  pl.BlockSpec((None, None, t_all, V_DIM), lambda bb, hh, i: (h + hh, bb, 0, 0)),
                  pl.BlockSpec((None, None, t_all, V_DIM), lambda bb, hh, i: (hh, bb, 0, 0)),
                  pl.BlockSpec((None, None, tq, V_DIM), lambda bb, hh, i: (h + hh, bb, i, 0)),
                  pl.BlockSpec((1, V_DIM), lambda bb, hh, i: (0, 0))],
        out_specs=pl.BlockSpec((None, tq, V_DIM), lambda bb, hh, i: (bb, i, hh)),
        out_shape=jax.ShapeDtypeStruct((b, l, DIFF_WIDTH), BF16),
        scratch_shapes=[pltpu.VMEM((n_big, V_DIM, tk), BF16),
                        pltpu.VMEM((V_DIM, max(tail, V7X_MXU_DIM)), BF16),
                        pltpu.VMEM((2, tk, tq), F32),
                        pltpu.VMEM((2, tk, tq), BF16),
                        pltpu.VMEM((2, V_DIM, tq), F32)],
        compiler_params=_params(ARB, ARB, ARB),
        name="diff_attn",
    )(lam_params, p_nr, p_nr, p_plain, p_plain, subln_w.reshape(1, V_DIM))


def _ret_kernel(dec_ref, q_ref, k_ref, v_ref, g_ref, nw_ref, o_ref, of_sc, ob_sc, *, chunk):
    l = q_ref.shape[0]
    lc = k_ref.shape[0] - l
    nc = l // chunk
    cdim = (((1,), (1,)), ((), ()))
    rdim = (((0,), (0,)), ((), ()))

    dec = dec_ref[...]
    lg = jnp.log1p(-jnp.exp2(-dec))
    lg_f, lg_b = lg[0:1, 0:1], lg[1:2, 0:1]

    def col_iota(n):
        return lax.broadcasted_iota(jnp.int32, (n, 1), 0).astype(F32)

    ic = col_iota(chunk)
    xi_f = jnp.exp(lg_f * (ic + 1.0))
    xi_b = jnp.exp(lg_b * (chunk - ic))
    zeta_f = jnp.exp(lg_f * (chunk - 1.0 - ic))
    zeta_b = jnp.exp(lg_b * ic)
    gc_f = jnp.exp(lg_f * chunk)
    gc_b = jnp.exp(lg_b * chunk)
    ri = lax.broadcasted_iota(jnp.int32, (chunk, chunk), 0)
    ci = lax.broadcasted_iota(jnp.int32, (chunk, chunk), 1)
    dist = (ri - ci).astype(F32)
    dmat = jnp.where(ri >= ci, jnp.exp(lg_f * jnp.maximum(dist, 0.0)), jnp.exp(lg_b * jnp.maximum(-dist, 0.0)))

    im = col_iota(lc)
    kctx = k_ref[pl.ds(l, lc), :].astype(F32)
    vctx = v_ref[pl.ds(l, lc), :]
    r0_f = lax.dot_general((kctx * jnp.exp(lg_f * (lc - 1.0 - im))).astype(BF16), vctx, rdim,
                           preferred_element_type=F32)
    r0_b = lax.dot_general((kctx * jnp.exp(lg_b * im)).astype(BF16), vctx, rdim, preferred_element_type=F32)

    def chunk_kv(rows, zeta):
        kz = (k_ref[rows, :].astype(F32) * zeta).astype(BF16)
        return lax.dot_general(kz, v_ref[rows, :], rdim, preferred_element_type=F32)

    def scan_step(j, carry):
        s_f, s_b = carry
        rows_f = pl.ds(pl.multiple_of(j * chunk, chunk), chunk)
        qc = q_ref[rows_f, :]
        scores = lax.dot_general(qc, k_ref[rows_f, :], cdim, preferred_element_type=F32) * dmat
        o_f = (jnp.dot(scores.astype(BF16), v_ref[rows_f, :], preferred_element_type=F32)
               + jnp.dot(qc, s_f.astype(BF16), preferred_element_type=F32) * xi_f)
        s_f = gc_f * s_f + chunk_kv(rows_f, zeta_f)
        rows_b = pl.ds(pl.multiple_of((nc - 1 - j) * chunk, chunk), chunk)
        o_b = jnp.dot(q_ref[rows_b, :], s_b.astype(BF16), preferred_element_type=F32) * xi_b
        s_b = gc_b * s_b + chunk_kv(rows_b, zeta_b)
        return (s_f, s_b), (rows_f, o_f), (rows_b, o_b)

    def finish(rows, o):
        y = o * lax.rsqrt(jnp.mean(o * o, axis=-1, keepdims=True) + EPS) * nw_ref[...]
        g = g_ref[rows, :].astype(F32)
        o_ref[rows, :] = (y * (g * jax.nn.sigmoid(g))).astype(o_ref.dtype)

    half = nc // 2
    upper = lambda c: pl.ds(pl.multiple_of((c - half) * chunk, chunk), chunk)

    def first_half(j, carry):
        carry, (rows_f, o_f), (_, o_b) = scan_step(j, carry)
        of_sc[rows_f, :] = o_f
        ob_sc[upper(nc - 1 - j), :] = o_b
        return carry

    def second_half(j, carry):
        carry, (rows_f, o_f), (rows_b, o_b) = scan_step(j, carry)
        finish(rows_f, o_f + ob_sc[upper(j), :])
        finish(rows_b, o_b + of_sc[rows_b, :])
        return carry

    carry = lax.fori_loop(0, half, first_half, (r0_f, r0_b), unroll=RET_UNROLL)
    lax.fori_loop(half, nc, second_half, carry, unroll=RET_UNROLL)


def _ret_call(p_rope, p_plain, decs, ret_norm_w, l):
    _, b, t_all, _ = p_rope.shape
    h = RET_HEADS
    chunk = _pick(l, (256, 128))
    assert (l // chunk) % 2 == 0, "the two scan directions meet in the middle of an even chunk count"
    return pl.pallas_call(
        functools.partial(_ret_kernel, chunk=chunk),
        grid=(b, h),
        in_specs=[pl.BlockSpec((None, 2, QK_DIM), lambda bb, hh: (hh, 0, 0)),
                  pl.BlockSpec((None, None, l, QK_DIM), lambda bb, hh: (hh, bb, 0, 0)),
                  pl.BlockSpec((None, None, t_all, QK_DIM), lambda bb, hh: (h + hh, bb, 0, 0)),
                  pl.BlockSpec((None, None, t_all, V_DIM), lambda bb, hh: (2 * DIFF_HEADS + hh, bb, 0, 0)),
                  pl.BlockSpec((None, None, l, V_DIM), lambda bb, hh: (2 * DIFF_HEADS + h + hh, bb, 0, 0)),
                  pl.BlockSpec((1, V_DIM), lambda bb, hh: (0, 0))],
        out_specs=pl.BlockSpec((None, l, V_DIM), lambda bb, hh: (bb, 0, hh)),
        out_shape=jax.ShapeDtypeStruct((b, l, RET_WIDTH), BF16),
        scratch_shapes=[pltpu.VMEM((l // 2, V_DIM), F32),
                        pltpu.VMEM((l // 2, V_DIM), F32)],
        compiler_params=_params(ARB, ARB),
        name="retention",
    )(decs, p_rope, p_rope, p_plain, p_plain, ret_norm_w.reshape(1, V_DIM))


def _outproj_kernel(ad_ref, ar_ref, wd_ref, wr_ref, x_ref, gate_ref, o_ref):
    y = jnp.dot(ad_ref[...], wd_ref[...], preferred_element_type=F32)
    y = y + jnp.dot(ar_ref[...], wr_ref[...], preferred_element_type=F32)
    o_ref[...] = x_ref[...] + gate_ref[...] * y


def _outproj_call(o_d, o_r, w_out, x, gate):
    b, l, d = x.shape
    kd, kr = o_d.shape[-1], o_r.shape[-1]
    tm = _pick(l, (1024, 512, 256, 128))
    tn = _pick(d, (512, 256, 128))
    per_b = l // tm
    return pl.pallas_call(
        _outproj_kernel,
        grid=(b * per_b, d // tn),
        in_specs=[pl.BlockSpec((tm, kd), lambda i, j: (i, 0)),
                  pl.BlockSpec((tm, kr), lambda i, j: (i, 0)),
                  pl.BlockSpec((kd, tn), lambda i, j: (0, j)),
                  pl.BlockSpec((kr, tn), lambda i, j: (kd // kr, j)),
                  pl.BlockSpec((tm, tn), lambda i, j: (i, j)),
                  pl.BlockSpec((None, 1, tn), lambda i, j: (i // per_b, 0, j))],
        out_specs=pl.BlockSpec((tm, tn), lambda i, j: (i, j)),
        out_shape=jax.ShapeDtypeStruct((b * l, d), F32),
        compiler_params=_params(ARB, ARB),
        name="outproj",
    )(o_d.reshape(b * l, kd), o_r.reshape(b * l, kr), w_out, w_out, x.reshape(b * l, d),
      gate.reshape(b, 1, d)).reshape(b, l, d)


def _rope_tables(l, lc):
    row, col = jnp.meshgrid(jnp.arange(l // GRID_W), jnp.arange(GRID_W), indexing="ij")
    row = row.reshape(-1).astype(F32)
    col = col.reshape(-1).astype(F32)
    half = QK_DIM // 2
    inv_freq = ROPE_BASE ** (-jnp.arange(0, half, 2, dtype=F32) / half)
    ang_r = row[:, None] * inv_freq
    ang_c = col[:, None] * inv_freq
    ang = jnp.concatenate([ang_r, ang_r, ang_c, ang_c], axis=-1)
    cos, sin = jnp.cos(ang), jnp.sin(ang)
    lo = (jnp.arange(QK_DIM) % half) < (half // 2)
    cos = jnp.concatenate([cos, jnp.ones((lc, QK_DIM), F32)], axis=0)
    sin_lo = jnp.concatenate([jnp.where(lo, -sin, 0.0), jnp.zeros((lc, QK_DIM), F32)], axis=0)
    sin_hi = jnp.concatenate([jnp.where(lo, 0.0, sin), jnp.zeros((lc, QK_DIM), F32)], axis=0)
    return cos, sin_lo, sin_hi


def _layer(x, ctx, c, c_ctx, norm_w, ada_w, ada_b, w_in, q_norm_w, k_norm_w, lam_params, subln_w,
           dec_f, dec_b, ret_norm_w, w_out):
    b, l, d = x.shape
    lc = ctx.shape[1]
    t_all = l + lc

    rows = -(-(b + 1) // 8) * 8
    cc = jnp.zeros((rows, d), F32).at[:b].set(c).at[b].set(c_ctx)
    mod = _ada_call(cc, ada_w, ada_b)
    shift, scale, gate = mod[:b, :d], mod[:b, d:2 * d], mod[:b, 2 * d:]
    shift_c = jnp.broadcast_to(mod[b, :d], (b, d))
    scale_c = jnp.broadcast_to(mod[b, d:2 * d], (b, d))
    mods = jnp.stack([scale, shift, scale_c, shift_c], axis=1)

    h_all = _norm_call(x, ctx, norm_w, mods).reshape(b * t_all, d)

    d_in = w_in.shape[1]
    w_qk = _cast_to_bf16(w_in, O_DQ, O_DV)
    tables = _rope_tables(l, lc)
    blocks = lambda lo, hi, base: list(range((lo - base) // INPROJ_TN, (hi - base) // INPROJ_TN))
    qk_vecs = jnp.stack([q_norm_w, k_norm_w]).reshape(2, 1, QK_DIM)
    ret_vecs = jnp.stack([jnp.ones((QK_DIM,), F32), jnp.full((QK_DIM,), QK_DIM ** -0.5, F32)]).reshape(2, 1, QK_DIM)
    call = functools.partial(_inproj_call, h_all, tables=tables, rows_per_batch=t_all)
    p_nr, w_rest = call(w=w_qk, vecs=qk_vecs, mode="normrope", col_blocks=blocks(O_DQ, O_DV, O_DQ), out_w=V_DIM,
                        cast=(w_in, O_DV, d_in - O_DV))
    p_rope, w_out_b = call(w=w_rest, vecs=ret_vecs, mode="rope", col_blocks=blocks(O_RQ, O_RV, O_DV), out_w=QK_DIM,
                           cast=(w_out, 0, w_out.shape[1]))
    p_plain = call(w=w_rest, vecs=None, mode="plain",
                   col_blocks=blocks(O_DV, O_RQ, O_DV) + blocks(O_RV, d_in, O_DV), out_w=V_DIM)
    p_nr = p_nr.reshape(-1, b, t_all, V_DIM)
    p_rope = p_rope.reshape(-1, b, t_all, QK_DIM)
    p_plain = p_plain.reshape(-1, b, t_all, V_DIM)

    o_d = _attn_call(p_nr, p_plain, lam_params, subln_w, l)
    decs = jnp.broadcast_to(jnp.stack([dec_f, dec_b], axis=1)[:, :, None], (RET_HEADS, 2, QK_DIM)).astype(F32)
    o_r = _ret_call(p_rope, p_plain, decs, ret_norm_w, l)

    return _outproj_call(o_d, o_r, w_out_b, x, gate)


def kernel(x, c, ctx, c_ctx, norm_w, ada_w, ada_b, w_in, diff_q_norm_w, diff_k_norm_w, diff_lambda_q1,
           diff_lambda_k1, diff_lambda_q2, diff_lambda_k2, diff_subln_w, ret_decay_fwd, ret_decay_bwd,
           ret_norm_w, w_out):
    depth = norm_w.shape[0]
    assert depth == 1, "LAM_INIT is the layer-0 value"
    for layer in range(depth):
        lam_params = jnp.stack([diff_lambda_q1[layer], diff_lambda_k1[layer],
                                diff_lambda_q2[layer], diff_lambda_k2[layer]]).astype(F32)
        x = _layer(x, ctx, c, c_ctx, norm_w[layer], ada_w[layer], ada_b[layer], w_in[layer],
                   diff_q_norm_w[layer], diff_k_norm_w[layer], lam_params, diff_subln_w[layer],
                   ret_decay_fwd[layer], ret_decay_bwd[layer], ret_norm_w[layer], w_out[layer])
    return x
```

```python
import functools
import math

import jax
import jax.numpy as jnp
from jax import lax
from jax.experimental import pallas as pl
from jax.experimental.pallas import tpu as pltpu

F32 = jnp.float32
BF16 = jnp.bfloat16

GRID_W = 64
DIFF_HEADS = 8
QK_DIM = 128
V_DIM = 256
RET_HEADS = 8
ROPE_BASE = 10000.0
EPS = 1e-6
LAM_INIT = 0.8 - 0.6 * math.exp(-0.3 * 0)

DIFF_QK_W = DIFF_HEADS * 2 * QK_DIM
DIFF_WIDTH = DIFF_HEADS * V_DIM
RET_QK_W = RET_HEADS * QK_DIM
RET_WIDTH = RET_HEADS * V_DIM
O_DQ = 0
O_DK = O_DQ + DIFF_QK_W
O_DV = O_DK + DIFF_QK_W
O_DG = O_DV + DIFF_WIDTH
O_RQ = O_DG + DIFF_WIDTH
O_RK = O_RQ + RET_QK_W
O_RV = O_RK + RET_QK_W
O_RG = O_RV + RET_WIDTH

V7X_VMEM_LIMIT_BYTES = 56 * 1024 * 1024
V7X_MXU_DIM = 256
ARB = "arbitrary"
RET_UNROLL = 8
INPROJ_TN = 1024
ATTN_TK = 1024
CAST_ROWS, CAST_COLS = 256, 2048


def _pick(n, candidates):
    for c in candidates:
        if n % c == 0:
            return c
    raise ValueError(f"no tile in {candidates} divides {n}")


def _params(*sem):
    return pltpu.CompilerParams(dimension_semantics=sem, vmem_limit_bytes=V7X_VMEM_LIMIT_BYTES)


def _ada_kernel(c_ref, w_ref, b_ref, o_ref):
    c = c_ref[...]
    a = (c * jax.nn.sigmoid(c)).astype(BF16)
    o_ref[...] = jnp.dot(a, w_ref[...].astype(BF16), preferred_element_type=F32) + b_ref[...]


def _ada_call(cc, ada_w, ada_b):
    rows, d = cc.shape
    n = ada_w.shape[1]
    tn = _pick(n, (512, 256, 128))
    return pl.pallas_call(
        _ada_kernel,
        grid=(n // tn,),
        in_specs=[pl.BlockSpec((rows, d), lambda j: (0, 0)),
                  pl.BlockSpec((d, tn), lambda j: (0, j)),
                  pl.BlockSpec((1, tn), lambda j: (0, j))],
        out_specs=pl.BlockSpec((rows, tn), lambda j: (0, j)),
        out_shape=jax.ShapeDtypeStruct((rows, n), F32),
        compiler_params=_params(ARB),
        name="ada_mod",
    )(cc, ada_w, ada_b.reshape(1, n))


def _norm_kernel(x_ref, ctx_ref, nw_ref, mod_ref, o_ref, *, n_lat):
    i = pl.program_id(1)

    def norm_mod(xv, scale, shift):
        ms = jnp.mean(xv * xv, axis=-1, keepdims=True)
        gain = nw_ref[...] * (1.0 + scale)
        return (xv * lax.rsqrt(ms + EPS) * gain + shift).astype(o_ref.dtype)

    @pl.when(i < n_lat)
    def _():
        o_ref[...] = norm_mod(x_ref[...], mod_ref[0:1, :], mod_ref[1:2, :])

    @pl.when(i >= n_lat)
    def _():
        o_ref[...] = norm_mod(ctx_ref[...], mod_ref[2:3, :], mod_ref[3:4, :])


def _norm_call(x, ctx, norm_w, mods):
    b, l, d = x.shape
    lc = ctx.shape[1]
    tr = _pick(math.gcd(l, lc), (256, 128, 64, 32, 16))
    n_lat, n_ctx = l // tr, lc // tr
    return pl.pallas_call(
        functools.partial(_norm_kernel, n_lat=n_lat),
        grid=(b, n_lat + n_ctx),
        in_specs=[pl.BlockSpec((None, tr, d), lambda bb, i: (bb, jnp.minimum(i, n_lat - 1), 0)),
                  pl.BlockSpec((None, tr, d), lambda bb, i: (bb, jnp.maximum(i - n_lat, 0), 0)),
                  pl.BlockSpec((1, d), lambda bb, i: (0, 0)),
                  pl.BlockSpec((None, 4, d), lambda bb, i: (bb, 0, 0))],
        out_specs=pl.BlockSpec((None, tr, d), lambda bb, i: (bb, i, 0)),
        out_shape=jax.ShapeDtypeStruct((b, l + lc, d), BF16),
        compiler_params=_params(ARB, ARB),
        name="norm_mod",
    )(x, ctx, norm_w.reshape(1, d), mods)


def _rope(xh, cos, sin_lo, sin_hi):
    return xh * cos + pltpu.roll(xh, 96, 1) * sin_lo + pltpu.roll(xh, 32, 1) * sin_hi


def _inproj_plain_kernel(cb_ref, a_ref, w_ref, o_ref, *, out_w):
    del cb_ref
    acc = jnp.dot(a_ref[...], w_ref[...], preferred_element_type=F32)
    for c in range(acc.shape[1] // out_w):
        o_ref[c] = acc[:, c * out_w:(c + 1) * out_w].astype(o_ref.dtype)


def _inproj_rope_kernel(cb_ref, a_ref, w_ref, cos_ref, slo_ref, shi_ref, vec_ref, *rest, norm, out_w, cast_steps):
    del cb_ref
    if cast_steps:
        cast_in_ref, o_ref, cast_out_ref, acc_sc = rest

        @pl.when(pl.program_id(0) < cast_steps)
        def _():
            cast_out_ref[...] = cast_in_ref[...].astype(cast_out_ref.dtype)
    else:
        o_ref, acc_sc = rest

    @pl.when(pl.program_id(0) == 0)
    def _():
        acc_sc[...] = jnp.zeros_like(acc_sc)

    cos, slo, shi, vec = cos_ref[...], slo_ref[...], shi_ref[...], vec_ref[...]
    per = out_w // QK_DIM
    for t in range(acc_sc.shape[1] // QK_DIM):
        xh = acc_sc[:, t * QK_DIM:(t + 1) * QK_DIM]
        if norm:
            xh = xh * lax.rsqrt(jnp.mean(xh * xh, axis=-1, keepdims=True) + EPS) * vec
            r = _rope(xh, cos, slo, shi)
        else:
            r = _rope(xh, cos, slo, shi) * vec
        lane0 = (t % per) * QK_DIM
        o_ref[t // per, :, lane0:lane0 + QK_DIM] = r.astype(o_ref.dtype)
    acc_sc[...] = jnp.dot(a_ref[...], w_ref[...], preferred_element_type=F32)


def _cast_to_bf16(src, col0, ncols):
    return src[:, col0:col0 + ncols].astype(BF16)


def _inproj_call(a2d, w, tables, vecs, *, mode, col_blocks, out_w, rows_per_batch, cast=None):
    m, d = a2d.shape
    tn = INPROJ_TN
    tm = _pick(rows_per_batch, (768, 512, 384, 256, 128))
    n_i, n_j = m // tm, len(col_blocks)
    n_out = n_j * tn // out_w
    col_tab = jnp.asarray(col_blocks, jnp.int32)
    out_shape = jax.ShapeDtypeStruct((n_out, m, out_w), BF16)
    if mode == "plain":
        return pl.pallas_call(
            functools.partial(_inproj_plain_kernel, out_w=out_w),
            grid_spec=pltpu.PrefetchScalarGridSpec(
                num_scalar_prefetch=1,
                grid=(n_i, n_j),
                in_specs=[pl.BlockSpec((tm, d), lambda i, j, cb: (i, 0)),
                          pl.BlockSpec((d, tn), lambda i, j, cb: (0, cb[j]))],
                out_specs=pl.BlockSpec((tn // out_w, tm, out_w), lambda i, j, cb: (j, i, 0)),
            ),
            out_shape=out_shape,
            compiler_params=_params(ARB, ARB),
            name="inproj_plain",
        )(col_tab, a2d, w)

    steps = n_i * n_j
    t_blocks = rows_per_batch // tm
    j_per_vec = n_j // vecs.shape[0]
    cur = lambda s: jnp.minimum(s, steps - 1)
    prev = lambda s: jnp.maximum(s - 1, 0)
    tab_spec = pl.BlockSpec((tm, QK_DIM), lambda s, cb: ((prev(s) // n_j) % t_blocks, 0))
    in_specs = [pl.BlockSpec((tm, d), lambda s, cb: (cur(s) // n_j, 0)),
                pl.BlockSpec((d, tn), lambda s, cb: (0, cb[cur(s) % n_j])),
                tab_spec, tab_spec, tab_spec,
                pl.BlockSpec((None, 1, QK_DIM), lambda s, cb: ((prev(s) % n_j) // j_per_vec, 0, 0))]
    out_specs = [pl.BlockSpec((tn // out_w, tm, out_w), lambda s, cb: (prev(s) % n_j, prev(s) // n_j, 0))]
    out_shapes = [out_shape]
    args = [a2d, w, *tables, vecs]

    cast_src, cast_col0, cast_ncols = cast
    cast_rows = cast_src.shape[0]
    n_c = cast_ncols // CAST_COLS
    cast_steps = (cast_rows // CAST_ROWS) * n_c
    ride = (cast_ncols % CAST_COLS == 0 and cast_col0 % CAST_COLS == 0 and cast_rows % CAST_ROWS == 0
            and cast_steps <= steps)
    if ride:
        blk = lambda s: jnp.minimum(s, cast_steps - 1)
        c0 = cast_col0 // CAST_COLS
        in_specs.append(pl.BlockSpec((CAST_ROWS, CAST_COLS), lambda s, cb: (blk(s) // n_c, c0 + blk(s) % n_c)))
        out_specs.append(pl.BlockSpec((CAST_ROWS, CAST_COLS), lambda s, cb: (blk(s) // n_c, blk(s) % n_c)))
        out_shapes.append(jax.ShapeDtypeStruct((cast_rows, cast_ncols), BF16))
        args.append(cast_src)
    res = pl.pallas_call(
        functools.partial(_inproj_rope_kernel, norm=(mode == "normrope"), out_w=out_w,
                          cast_steps=cast_steps if ride else 0),
        grid_spec=pltpu.PrefetchScalarGridSpec(
            num_scalar_prefetch=1,
            grid=(steps + 1,),
            in_specs=in_specs,
            out_specs=out_specs,
            scratch_shapes=[pltpu.VMEM((tm, tn), F32)],
        ),
        out_shape=out_shapes,
        compiler_params=_params(ARB),
        name="inproj_" + mode,
    )(col_tab, *args)
    if ride:
        return res[0], res[1]
    return res[0], _cast_to_bf16(cast_src, cast_col0, cast_ncols)


def _attn_kernel(lamp_ref, q_ref, k_ref, v_ref, g_ref, sw_ref, o_ref, vt_sc, vt_tail_sc, s_sc, p_sc, acc_sc, *, tk):
    t_all = k_ref.shape[0]
    tq = q_ref.shape[0]
    n_big, tail = divmod(t_all, tk)
    n = n_big + (1 if tail else 0)
    size = lambda c: tk if c < n_big else tail
    c_exp = (QK_DIM ** -0.5) * math.log2(math.e)

    def transposed(rows):
        return v_ref[rows, :].astype(F32).T.astype(BF16)

    @pl.when(pl.program_id(2) == 0)
    def _():
        def transpose_chunk(c, carry):
            vt_sc[c] = transposed(pl.ds(pl.multiple_of(c * tk, tk), tk))
            return carry
        lax.fori_loop(0, n_big, transpose_chunk, 0)
        if tail:
            vt_tail_sc[...] = transposed(pl.ds(n_big * tk, tail))

    qt = q_ref[...].astype(F32).T.astype(BF16)
    qts = (qt[:QK_DIM], qt[QK_DIM:])
    acc_sc[...] = jnp.zeros_like(acc_sc)

    def stage_qk(c, rows):
        start = c * tk if isinstance(c, int) else pl.multiple_of(c * tk, tk)
        kc = k_ref[pl.ds(start, rows), :]
        cms = []
        for sub in range(2):
            s = jnp.dot(kc[:, sub * QK_DIM:(sub + 1) * QK_DIM], qts[sub],
                        preferred_element_type=F32)
            s_sc[sub, :rows] = s
            cms.append(jnp.max(s, axis=0, keepdims=True))
        return tuple(cms)

    def stage_softmax(rows, cms, ms, ls):
        new_m, new_l, alphas = [], [], []
        for sub in range(2):
            m_new = jnp.maximum(ms[sub], cms[sub])
            alpha = jnp.exp2((ms[sub] - m_new) * c_exp)
            p = jnp.exp2(s_sc[sub, :rows] * c_exp - m_new * c_exp)
            new_l.append(alpha * ls[sub] + jnp.sum(p.reshape(rows // 8, 8, tq), axis=0))
            p_sc[sub, :rows] = p.astype(BF16)
            new_m.append(m_new)
            alphas.append(alpha)
        return tuple(new_m), tuple(new_l), tuple(alphas)

    def stage_pv(c, rows, alphas):
        vt = vt_sc[c] if rows == tk else vt_tail_sc[...]
        for sub in range(2):
            acc_sc[sub] = alphas[sub] * acc_sc[sub] + jnp.dot(vt, p_sc[sub, :rows], preferred_element_type=F32)

    def tick(t, carry):
        cms, ms, ls, alphas = carry
        static = isinstance(t, int)
        if not static or 0 <= t - 2 < n:
            stage_pv(t - 2, size(t - 2) if static else tk, alphas)
        if not static or 0 <= t - 1 < n:
            ms, ls, alphas = stage_softmax(size(t - 1) if static else tk, cms, ms, ls)
        if not static or t < n:
            cms = stage_qk(t, size(t) if static else tk)
        return cms, ms, ls, alphas

    neg = jnp.full((1, tq), -jnp.inf, F32)
    zero = jnp.zeros((1, tq), F32)
    carry = ((neg, neg), (neg, neg), (jnp.zeros((8, tq), F32),) * 2, (zero, zero))
    loop_lo, loop_hi = 2, max(2, n_big)
    for t in range(loop_lo):
        carry = tick(t, carry)
    carry = lax.fori_loop(loop_lo, loop_hi, tick, carry)
    for t in range(loop_hi, n + 2):
        carry = tick(t, carry)
    _, ms, ls, _ = carry

    lp = lamp_ref[...]
    lam = (jnp.exp(jnp.sum(lp[0:1] * lp[1:2], axis=1, keepdims=True))
           - jnp.exp(jnp.sum(lp[2:3] * lp[3:4], axis=1, keepdims=True)) + LAM_INIT)
    inv1 = 1.0 / jnp.sum(ls[0], axis=0, keepdims=True)
    inv2 = lam / jnp.sum(ls[1], axis=0, keepdims=True)
    o = (acc_sc[0] * inv1 - acc_sc[1] * inv2).T
    y = o * lax.rsqrt(jnp.mean(o * o, axis=-1, keepdims=True) + EPS) * sw_ref[...] * (1.0 - LAM_INIT)
    g = g_ref[...].astype(F32)
    o_ref[...] = (y * (g * jax.nn.sigmoid(g))).astype(o_ref.dtype)


def _attn_call(p_nr, p_plain, lam_params, subln_w, l):
    _, b, t_all, _ = p_nr.shape
    tq = _pick(l, (1024, 512, 256, 128))
    tk = ATTN_TK
    n_big, tail = divmod(t_all, tk)
    assert n_big >= 2 and tail % V7X_MXU_DIM == 0, "needs two full key chunks and a tail of whole MXU passes"
    h = DIFF_HEADS
    return pl.pallas_call(
        functools.partial(_attn_kernel, tk=tk),
        grid=(b, h, l // tq),
        in_specs=[pl.BlockSpec((4, QK_DIM), lambda bb, hh, i: (0, 0)),
                  pl.BlockSpec((None, None, tq, V_DIM), lambda bb, hh, i: (hh, bb, i, 0)),
                  pl.BlockSpec((None, None, t_all, V_DIM), lambda bb, hh, i: (h + hh, bb, 0, 0)),
                  pl.BlockSpec((None, None, t_all, V_DIM), lambda bb, hh, i: (hh, bb, 0, 0)),
                  pl.BlockSpec((None, None, tq, V_DIM), lambda bb, hh, i: (h + hh, bb, i, 0)),
                  pl.BlockSpec((1, V_DIM), lambda bb, hh, i: (0, 0))],
        out_specs=pl.BlockSpec((None, tq, V_DIM), lambda bb, hh, i: (bb, i, hh)),
        out_shape=jax.ShapeDtypeStruct((b, l, DIFF_WIDTH), BF16),
        scratch_shapes=[pltpu.VMEM((n_big, V_DIM, tk), BF16),
                        pltpu.VMEM((V_DIM, max(tail, V7X_MXU_DIM)), BF16),
                        pltpu.VMEM((2, tk, tq), F32),
                        pltpu.VMEM((2, tk, tq), BF16),
                        pltpu.VMEM((2, V_DIM, tq), F32)],
        compiler_params=_params(ARB, ARB, ARB),
        name="diff_attn",
    )(lam_params, p_nr, p_nr, p_plain, p_plain, subln_w.reshape(1, V_DIM))


def _ret_kernel(dec_ref, q_ref, k_ref, v_ref, g_ref, nw_ref, o_ref, of_sc, ob_sc, *, chunk):
    l = q_ref.shape[0]
    lc = k_ref.shape[0] - l
    nc = l // chunk
    cdim = (((1,), (1,)), ((), ()))
    rdim = (((0,), (0,)), ((), ()))

    dec = dec_ref[...]
    lg = jnp.log1p(-jnp.exp2(-dec))
    lg_f, lg_b = lg[0:1, 0:1], lg[1:2, 0:1]

    def col_iota(n):
        return lax.broadcasted_iota(jnp.int32, (n, 1), 0).astype(F32)

    ic = col_iota(chunk)
    xi_f = jnp.exp(lg_f * (ic + 1.0))
    xi_b = jnp.exp(lg_b * (chunk - ic))
    zeta_f = jnp.exp(lg_f * (chunk - 1.0 - ic))
    zeta_b = jnp.exp(lg_b * ic)
    gc_f = jnp.exp(lg_f * chunk)
    gc_b = jnp.exp(lg_b * chunk)
    ri = lax.broadcasted_iota(jnp.int32, (chunk, chunk), 0)
    ci = lax.broadcasted_iota(jnp.int32, (chunk, chunk), 1)
    dist = (ri - ci).astype(F32)
    dmat = jnp.where(ri >= ci, jnp.exp(lg_f * jnp.maximum(dist, 0.0)), jnp.exp(lg_b * jnp.maximum(-dist, 0.0)))

    im = col_iota(lc)
    kctx = k_ref[pl.ds(l, lc), :].astype(F32)
    vctx = v_ref[pl.ds(l, lc), :]
    r0_f = lax.dot_general((kctx * jnp.exp(lg_f * (lc - 1.0 - im))).astype(BF16), vctx, rdim,
                           preferred_element_type=F32)
    r0_b = lax.dot_general((kctx * jnp.exp(lg_b * im)).astype(BF16), vctx, rdim, preferred_element_type=F32)

    def chunk_kv(rows, zeta):
        kz = (k_ref[rows, :].astype(F32) * zeta).astype(BF16)
        return lax.dot_general(kz, v_ref[rows, :], rdim, preferred_element_type=F32)

    def scan_step(j, carry):
        s_f, s_b = carry
        rows_f = pl.ds(pl.multiple_of(j * chunk, chunk), chunk)
        qc = q_ref[rows_f, :]
        scores = lax.dot_general(qc, k_ref[rows_f, :], cdim, preferred_element_type=F32) * dmat
        o_f = (jnp.dot(scores.astype(BF16), v_ref[rows_f, :], preferred_element_type=F32)
               + jnp.dot(qc, s_f.astype(BF16), preferred_element_type=F32) * xi_f)
        s_f = gc_f * s_f + chunk_kv(rows_f, zeta_f)
        rows_b = pl.ds(pl.multiple_of((nc - 1 - j) * chunk, chunk), chunk)
        o_b = jnp.dot(q_ref[rows_b, :], s_b.astype(BF16), preferred_element_type=F32) * xi_b
        s_b = gc_b * s_b + chunk_kv(rows_b, zeta_b)
        return (s_f, s_b), (rows_f, o_f), (rows_b, o_b)

    def finish(rows, o):
        y = o * lax.rsqrt(jnp.mean(o * o, axis=-1, keepdims=True) + EPS) * nw_ref[...]
        g = g_ref[rows, :].astype(F32)
        o_ref[rows, :] = (y * (g * jax.nn.sigmoid(g))).astype(o_ref.dtype)

    half = nc // 2
    upper = lambda c: pl.ds(pl.multiple_of((c - half) * chunk, chunk), chunk)

    def first_half(j, carry):
        carry, (rows_f, o_f), (_, o_b) = scan_step(j, carry)
        of_sc[rows_f, :] = o_f
        ob_sc[upper(nc - 1 - j), :] = o_b
        return carry

    def second_half(j, carry):
        carry, (rows_f, o_f), (rows_b, o_b) = scan_step(j, carry)
        finish(rows_f, o_f + ob_sc[upper(j), :])
        finish(rows_b, o_b + of_sc[rows_b, :])
        return carry

    carry = lax.fori_loop(0, half, first_half, (r0_f, r0_b), unroll=RET_UNROLL)
    lax.fori_loop(half, nc, second_half, carry, unroll=RET_UNROLL)


def _ret_call(p_rope, p_plain, decs, ret_norm_w, l):
    _, b, t_all, _ = p_rope.shape
    h = RET_HEADS
    chunk = _pick(l, (256, 128))
    assert (l // chunk) % 2 == 0, "the two scan directions meet in the middle of an even chunk count"
    return pl.pallas_call(
        functools.partial(_ret_kernel, chunk=chunk),
        grid=(b, h),
        in_specs=[pl.BlockSpec((None, 2, QK_DIM), lambda bb, hh: (hh, 0, 0)),
                  pl.BlockSpec((None, None, l, QK_DIM), lambda bb, hh: (hh, bb, 0, 0)),
                  pl.BlockSpec((None, None, t_all, QK_DIM), lambda bb, hh: (h + hh, bb, 0, 0)),
                  pl.BlockSpec((None, None, t_all, V_DIM), lambda bb, hh: (2 * DIFF_HEADS + hh, bb, 0, 0)),
                  pl.BlockSpec((None, None, l, V_DIM), lambda bb, hh: (2 * DIFF_HEADS + h + hh, bb, 0, 0)),
                  pl.BlockSpec((1, V_DIM), lambda bb, hh: (0, 0))],
        out_specs=pl.BlockSpec((None, l, V_DIM), lambda bb, hh: (bb, 0, hh)),
        out_shape=jax.ShapeDtypeStruct((b, l, RET_WIDTH), BF16),
        scratch_shapes=[pltpu.VMEM((l // 2, V_DIM), F32),
                        pltpu.VMEM((l // 2, V_DIM), F32)],
        compiler_params=_params(ARB, ARB),
        name="retention",
    )(decs, p_rope, p_rope, p_plain, p_plain, ret_norm_w.reshape(1, V_DIM))


def _outproj_kernel(ad_ref, ar_ref, wd_ref, wr_ref, x_ref, gate_ref, o_ref):
    y = jnp.dot(ad_ref[...], wd_ref[...], preferred_element_type=F32)
    y = y + jnp.dot(ar_ref[...], wr_ref[...], preferred_element_type=F32)
    o_ref[...] = x_ref[...] + gate_ref[...] * y


def _outproj_call(o_d, o_r, w_out, x, gate):
    b, l, d = x.shape
    kd, kr = o_d.shape[-1], o_r.shape[-1]
    tm = _pick(l, (1024, 512, 256, 128))
    tn = _pick(d, (512, 256, 128))
    per_b = l // tm
    return pl.pallas_call(
        _outproj_kernel,
        grid=(b * per_b, d // tn),
        in_specs=[pl.BlockSpec((tm, kd), lambda i, j: (i, 0)),
                  pl.BlockSpec((tm, kr), lambda i, j: (i, 0)),
                  pl.BlockSpec((kd, tn), lambda i, j: (0, j)),
                  pl.BlockSpec((kr, tn), lambda i, j: (kd // kr, j)),
                  pl.BlockSpec((tm, tn), lambda i, j: (i, j)),
                  pl.BlockSpec((None, 1, tn), lambda i, j: (i // per_b, 0, j))],
        out_specs=pl.BlockSpec((tm, tn), lambda i, j: (i, j)),
        out_shape=jax.ShapeDtypeStruct((b * l, d), F32),
        compiler_params=_params(ARB, ARB),
        name="outproj",
    )(o_d.reshape(b * l, kd), o_r.reshape(b * l, kr), w_out, w_out, x.reshape(b * l, d),
      gate.reshape(b, 1, d)).reshape(b, l, d)


def _rope_tables(l, lc):
    n_rows = l // GRID_W
    half = QK_DIM // 2
    inv_freq = ROPE_BASE ** (-jnp.arange(0, half, 2, dtype=F32) / half)
    ang_r = jnp.arange(n_rows, dtype=F32)[:, None] * inv_freq
    ang_c = jnp.arange(GRID_W, dtype=F32)[:, None] * inv_freq

    def spread(fn):
        by_row = jnp.broadcast_to(fn(ang_r)[:, None, :], (n_rows, GRID_W, half // 2)).reshape(l, half // 2)
        by_col = jnp.broadcast_to(fn(ang_c)[None, :, :], (n_rows, GRID_W, half // 2)).reshape(l, half // 2)
        return jnp.concatenate([by_row, by_row, by_col, by_col], axis=-1)

    cos, sin = spread(jnp.cos), spread(jnp.sin)
    lo = (jnp.arange(QK_DIM) % half) < (half // 2)
    cos = jnp.concatenate([cos, jnp.ones((lc, QK_DIM), F32)], axis=0)
    sin_lo = jnp.concatenate([jnp.where(lo, -sin, 0.0), jnp.zeros((lc, QK_DIM), F32)], axis=0)
    sin_hi = jnp.concatenate([jnp.where(lo, 0.0, sin), jnp.zeros((lc, QK_DIM), F32)], axis=0)
    return cos, sin_lo, sin_hi


def _layer(x, ctx, c, c_ctx, norm_w, ada_w, ada_b, w_in, q_norm_w, k_norm_w, lam_params, subln_w,
           dec_f, dec_b, ret_norm_w, w_out):
    b, l, d = x.shape
    lc = ctx.shape[1]
    t_all = l + lc

    rows = -(-(b + 1) // 8) * 8
    cc = jnp.zeros((rows, d), F32).at[:b].set(c).at[b].set(c_ctx)
    mod = _ada_call(cc, ada_w, ada_b)
    shift, scale, gate = mod[:b, :d], mod[:b, d:2 * d], mod[:b, 2 * d:]
    shift_c = jnp.broadcast_to(mod[b, :d], (b, d))
    scale_c = jnp.broadcast_to(mod[b, d:2 * d], (b, d))
    mods = jnp.stack([scale, shift, scale_c, shift_c], axis=1)

    h_all = _norm_call(x, ctx, norm_w, mods).reshape(b * t_all, d)

    d_in = w_in.shape[1]
    w_qk = _cast_to_bf16(w_in, O_DQ, O_DV)
    tables = _rope_tables(l, lc)
    blocks = lambda lo, hi, base: list(range((lo - base) // INPROJ_TN, (hi - base) // INPROJ_TN))
    qk_vecs = jnp.stack([q_norm_w, k_norm_w]).reshape(2, 1, QK_DIM)
    ret_vecs = jnp.stack([jnp.ones((QK_DIM,), F32), jnp.full((QK_DIM,), QK_DIM ** -0.5, F32)]).reshape(2, 1, QK_DIM)
    call = functools.partial(_inproj_call, h_all, tables=tables, rows_per_batch=t_all)
    p_nr, w_rest = call(w=w_qk, vecs=qk_vecs, mode="normrope", col_blocks=blocks(O_DQ, O_DV, O_DQ), out_w=V_DIM,
                        cast=(w_in, O_DV, d_in - O_DV))
    p_rope, w_out_b = call(w=w_rest, vecs=ret_vecs, mode="rope", col_blocks=blocks(O_RQ, O_RV, O_DV), out_w=QK_DIM,
                           cast=(w_out, 0, w_out.shape[1]))
    p_plain = call(w=w_rest, vecs=None, mode="plain",
                   col_blocks=blocks(O_DV, O_RQ, O_DV) + blocks(O_RV, d_in, O_DV), out_w=V_DIM)
    p_nr = p_nr.reshape(-1, b, t_all, V_DIM)
    p_rope = p_rope.reshape(-1, b, t_all, QK_DIM)
    p_plain = p_plain.reshape(-1, b, t_all, V_DIM)

    o_d = _attn_call(p_nr, p_plain, lam_params, subln_w, l)
    decs = jnp.broadcast_to(jnp.stack([dec_f, dec_b], axis=1)[:, :, None], (RET_HEADS, 2, QK_DIM)).astype(F32)
    o_r = _ret_call(p_rope, p_plain, decs, ret_norm_w, l)

    return _outproj_call(o_d, o_r, w_out_b, x, gate)


def kernel(x, c, ctx, c_ctx, norm_w, ada_w, ada_b, w_in, diff_q_norm_w, diff_k_norm_w, diff_lambda_q1,
           diff_lambda_k1, diff_lambda_q2, diff_lambda_k2, diff_subln_w, ret_decay_fwd, ret_decay_bwd,
           ret_norm_w, w_out):
    depth = norm_w.shape[0]
    assert depth == 1, "LAM_INIT is the layer-0 value"
    for layer in range(depth):
        lam_params = jnp.stack([diff_lambda_q1[layer], diff_lambda_k1[layer],
                                diff_lambda_q2[layer], diff_lambda_k2[layer]]).astype(F32)
        x = _layer(x, ctx, c, c_ctx, norm_w[layer], ada_w[layer], ada_b[layer], w_in[layer],
                   diff_q_norm_w[layer], diff_k_norm_w[layer], lam_params, diff_subln_w[layer],
                   ret_decay_fwd[layer], ret_decay_bwd[layer], ret_norm_w[layer], w_out[layer])
    return x
```

```python
import functools
import math

import jax
import jax.numpy as jnp
from jax import lax
from jax.experimental import pallas as pl
from jax.experimental.pallas import tpu as pltpu

F32 = jnp.float32
BF16 = jnp.bfloat16

GRID_W = 64
DIFF_HEADS = 8
QK_DIM = 128
V_DIM = 256
RET_HEADS = 8
ROPE_BASE = 10000.0
EPS = 1e-6
LAM_INIT = 0.8 - 0.6 * math.exp(-0.3 * 0)

DIFF_QK_W = DIFF_HEADS * 2 * QK_DIM
DIFF_WIDTH = DIFF_HEADS * V_DIM
RET_QK_W = RET_HEADS * QK_DIM
RET_WIDTH = RET_HEADS * V_DIM
O_DQ = 0
O_DK = O_DQ + DIFF_QK_W
O_DV = O_DK + DIFF_QK_W
O_DG = O_DV + DIFF_WIDTH
O_RQ = O_DG + DIFF_WIDTH
O_RK = O_RQ + RET_QK_W
O_RV = O_RK + RET_QK_W
O_RG = O_RV + RET_WIDTH

V7X_VMEM_LIMIT_BYTES = 56 * 1024 * 1024
V7X_MXU_DIM = 256
ARB = "arbitrary"
RET_UNROLL = 8
INPROJ_TN = 1024
ATTN_TK = 1024
CAST_ROWS, CAST_COLS = 256, 2048


def _pick(n, candidates):
    for c in candidates:
        if n % c == 0:
            return c
    raise ValueError(f"no tile in {candidates} divides {n}")


def _params(*sem):
    return pltpu.CompilerParams(dimension_semantics=sem, vmem_limit_bytes=V7X_VMEM_LIMIT_BYTES)


def _ada_kernel(c_ref, w_ref, b_ref, o_ref):
    c = c_ref[...]
    a = (c * jax.nn.sigmoid(c)).astype(BF16)
    o_ref[...] = jnp.dot(a, w_ref[...].astype(BF16), preferred_element_type=F32) + b_ref[...]


def _ada_call(cc, ada_w, ada_b):
    rows, d = cc.shape
    n = ada_w.shape[1]
    tn = _pick(n, (512, 256, 128))
    return pl.pallas_call(
        _ada_kernel,
        grid=(n // tn,),
        in_specs=[pl.BlockSpec((rows, d), lambda j: (0, 0)),
                  pl.BlockSpec((d, tn), lambda j: (0, j)),
                  pl.BlockSpec((1, tn), lambda j: (0, j))],
        out_specs=pl.BlockSpec((rows, tn), lambda j: (0, j)),
        out_shape=jax.ShapeDtypeStruct((rows, n), F32),
        compiler_params=_params(ARB),
        name="ada_mod",
    )(cc, ada_w, ada_b.reshape(1, n))


def _norm_kernel(x_ref, ctx_ref, nw_ref, mod_ref, o_ref, *, n_lat):
    i = pl.program_id(1)

    def norm_mod(xv, scale, shift):
        ms = jnp.mean(xv * xv, axis=-1, keepdims=True)
        gain = nw_ref[...] * (1.0 + scale)
        return (xv * lax.rsqrt(ms + EPS) * gain + shift).astype(o_ref.dtype)

    @pl.when(i < n_lat)
    def _():
        o_ref[...] = norm_mod(x_ref[...], mod_ref[0:1, :], mod_ref[1:2, :])

    @pl.when(i >= n_lat)
    def _():
        o_ref[...] = norm_mod(ctx_ref[...], mod_ref[2:3, :], mod_ref[3:4, :])


def _norm_call(x, ctx, norm_w, mods):
    b, l, d = x.shape
    lc = ctx.shape[1]
    tr = _pick(math.gcd(l, lc), (256, 128, 64, 32, 16))
    n_lat, n_ctx = l // tr, lc // tr
    return pl.pallas_call(
        functools.partial(_norm_kernel, n_lat=n_lat),
        grid=(b, n_lat + n_ctx),
        in_specs=[pl.BlockSpec((None, tr, d), lambda bb, i: (bb, jnp.minimum(i, n_lat - 1), 0)),
                  pl.BlockSpec((None, tr, d), lambda bb, i: (bb, jnp.maximum(i - n_lat, 0), 0)),
                  pl.BlockSpec((1, d), lambda bb, i: (0, 0)),
                  pl.BlockSpec((None, 4, d), lambda bb, i: (bb, 0, 0))],
        out_specs=pl.BlockSpec((None, tr, d), lambda bb, i: (bb, i, 0)),
        out_shape=jax.ShapeDtypeStruct((b, l + lc, d), BF16),
        compiler_params=_params(ARB, ARB),
        name="norm_mod",
    )(x, ctx, norm_w.reshape(1, d), mods)


def _rope(xh, cos, sin_lo, sin_hi):
    return xh * cos + pltpu.roll(xh, 96, 1) * sin_lo + pltpu.roll(xh, 32, 1) * sin_hi


def _inproj_plain_kernel(cb_ref, a_ref, w_ref, o_ref, *, out_w):
    del cb_ref
    acc = jnp.dot(a_ref[...], w_ref[...], preferred_element_type=F32)
    for c in range(acc.shape[1] // out_w):
        o_ref[c] = acc[:, c * out_w:(c + 1) * out_w].astype(o_ref.dtype)


def _inproj_rope_kernel(cb_ref, a_ref, w_ref, cos_ref, slo_ref, shi_ref, vec_ref, *rest, norm, out_w, cast_steps):
    del cb_ref
    if cast_steps:
        cast_in_ref, o_ref, cast_out_ref, acc_sc = rest

        @pl.when(pl.program_id(0) < cast_steps)
        def _():
            cast_out_ref[...] = cast_in_ref[...].astype(cast_out_ref.dtype)
    else:
        o_ref, acc_sc = rest

    @pl.when(pl.program_id(0) == 0)
    def _():
        acc_sc[...] = jnp.zeros_like(acc_sc)

    cos, slo, shi, vec = cos_ref[...], slo_ref[...], shi_ref[...], vec_ref[...]
    per = out_w // QK_DIM
    for t in range(acc_sc.shape[1] // QK_DIM):
        xh = acc_sc[:, t * QK_DIM:(t + 1) * QK_DIM]
        if norm:
            xh = xh * lax.rsqrt(jnp.mean(xh * xh, axis=-1, keepdims=True) + EPS) * vec
            r = _rope(xh, cos, slo, shi)
        else:
            r = _rope(xh, cos, slo, shi) * vec
        lane0 = (t % per) * QK_DIM
        o_ref[t // per, :, lane0:lane0 + QK_DIM] = r.astype(o_ref.dtype)
    acc_sc[...] = jnp.dot(a_ref[...], w_ref[...], preferred_element_type=F32)


def _cast_to_bf16(src, col0, ncols):
    return src[:, col0:col0 + ncols].astype(BF16)


def _inproj_call(a2d, w, tables, vecs, *, mode, col_blocks, out_w, rows_per_batch, cast=None):
    m, d = a2d.shape
    tn = INPROJ_TN
    tm = _pick(rows_per_batch, (768, 512, 384, 256, 128))
    n_i, n_j = m // tm, len(col_blocks)
    n_out = n_j * tn // out_w
    col_tab = jnp.asarray(col_blocks, jnp.int32)
    out_shape = jax.ShapeDtypeStruct((n_out, m, out_w), BF16)
    if mode == "plain":
        return pl.pallas_call(
            functools.partial(_inproj_plain_kernel, out_w=out_w),
            grid_spec=pltpu.PrefetchScalarGridSpec(
                num_scalar_prefetch=1,
                grid=(n_i, n_j),
                in_specs=[pl.BlockSpec((tm, d), lambda i, j, cb: (i, 0)),
                          pl.BlockSpec((d, tn), lambda i, j, cb: (0, cb[j]))],
                out_specs=pl.BlockSpec((tn // out_w, tm, out_w), lambda i, j, cb: (j, i, 0)),
            ),
            out_shape=out_shape,
            compiler_params=_params(ARB, ARB),
            name="inproj_plain",
        )(col_tab, a2d, w)

    steps = n_i * n_j
    t_blocks = rows_per_batch // tm
    j_per_vec = n_j // vecs.shape[0]
    cur = lambda s: jnp.minimum(s, steps - 1)
    prev = lambda s: jnp.maximum(s - 1, 0)
    tab_spec = pl.BlockSpec((tm, QK_DIM), lambda s, cb: ((prev(s) // n_j) % t_blocks, 0))
    in_specs = [pl.BlockSpec((tm, d), lambda s, cb: (cur(s) // n_j, 0)),
                pl.BlockSpec((d, tn), lambda s, cb: (0, cb[cur(s) % n_j])),
                tab_spec, tab_spec, tab_spec,
                pl.BlockSpec((None, 1, QK_DIM), lambda s, cb: ((prev(s) % n_j) // j_per_vec, 0, 0))]
    out_specs = [pl.BlockSpec((tn // out_w, tm, out_w), lambda s, cb: (prev(s) % n_j, prev(s) // n_j, 0))]
    out_shapes = [out_shape]
    args = [a2d, w, *tables, vecs]

    cast_src, cast_col0, cast_ncols = cast
    cast_rows = cast_src.shape[0]
    n_c = cast_ncols // CAST_COLS
    cast_steps = (cast_rows // CAST_ROWS) * n_c
    ride = (cast_ncols % CAST_COLS == 0 and cast_col0 % CAST_COLS == 0 and cast_rows % CAST_ROWS == 0
            and cast_steps <= steps)
    if ride:
        blk = lambda s: jnp.minimum(s, cast_steps - 1)
        c0 = cast_col0 // CAST_COLS
        in_specs.append(pl.BlockSpec((CAST_ROWS, CAST_COLS), lambda s, cb: (blk(s) // n_c, c0 + blk(s) % n_c)))
        out_specs.append(pl.BlockSpec((CAST_ROWS, CAST_COLS), lambda s, cb: (blk(s) // n_c, blk(s) % n_c)))
        out_shapes.append(jax.ShapeDtypeStruct((cast_rows, cast_ncols), BF16))
        args.append(cast_src)
    res = pl.pallas_call(
        functools.partial(_inproj_rope_kernel, norm=(mode == "normrope"), out_w=out_w,
                          cast_steps=cast_steps if ride else 0),
        grid_spec=pltpu.PrefetchScalarGridSpec(
            num_scalar_prefetch=1,
            grid=(steps + 1,),
            in_specs=in_specs,
            out_specs=out_specs,
            scratch_shapes=[pltpu.VMEM((tm, tn), F32)],
        ),
        out_shape=out_shapes,
        compiler_params=_params(ARB),
        name="inproj_" + mode,
    )(col_tab, *args)
    if ride:
        return res[0], res[1]
    return res[0], _cast_to_bf16(cast_src, cast_col0, cast_ncols)


def _attn_kernel(lamp_ref, q_ref, k_ref, v_ref, g_ref, sw_ref, o_ref, vt_sc, vt_tail_sc, s_sc, p_sc, acc_sc, *, tk):
    t_all = k_ref.shape[0]
    tq = q_ref.shape[0]
    n_big, tail = divmod(t_all, tk)
    first = 1 if tail else 0
    n = n_big + first
    size = lambda c: tail if c < first else tk
    c_exp = (QK_DIM ** -0.5) * math.log2(math.e)

    def transposed(rows):
        return v_ref[rows, :].astype(F32).T.astype(BF16)

    @pl.when(pl.program_id(2) == 0)
    def _():
        def transpose_chunk(c, carry):
            vt_sc[c] = transposed(pl.ds(pl.multiple_of(c * tk, tk), tk))
            return carry
        lax.fori_loop(0, n_big, transpose_chunk, 0)
        if tail:
            vt_tail_sc[...] = transposed(pl.ds(n_big * tk, tail))

    qt = q_ref[...].astype(F32).T.astype(BF16)
    qts = (qt[:QK_DIM], qt[QK_DIM:])
    acc_sc[...] = jnp.zeros_like(acc_sc)

    def stage_qk(c, rows):
        if rows != tk:
            start = n_big * tk
        else:
            start = (c - first) * tk if isinstance(c, int) else pl.multiple_of((c - first) * tk, tk)
        kc = k_ref[pl.ds(start, rows), :]
        cms = []
        for sub in range(2):
            s = jnp.dot(kc[:, sub * QK_DIM:(sub + 1) * QK_DIM], qts[sub],
                        preferred_element_type=F32)
            s_sc[sub, :rows] = s
            cms.append(jnp.max(s, axis=0, keepdims=True))
        return tuple(cms)

    def stage_softmax(rows, cms, ms, ls):
        new_m, new_l, alphas = [], [], []
        for sub in range(2):
            m_new = jnp.maximum(ms[sub], cms[sub])
            alpha = jnp.exp2((ms[sub] - m_new) * c_exp)
            p = jnp.exp2(s_sc[sub, :rows] * c_exp - m_new * c_exp)
            new_l.append(alpha * ls[sub] + jnp.sum(p.reshape(rows // 8, 8, tq), axis=0))
            p_sc[sub, :rows] = p.astype(BF16)
            new_m.append(m_new)
            alphas.append(alpha)
        return tuple(new_m), tuple(new_l), tuple(alphas)

    def stage_pv(c, rows, alphas):
        vt = vt_sc[c - first] if rows == tk else vt_tail_sc[...]
        for sub in range(2):
            acc_sc[sub] = alphas[sub] * acc_sc[sub] + jnp.dot(vt, p_sc[sub, :rows], preferred_element_type=F32)

    def tick(t, carry):
        cms, ms, ls, alphas = carry
        static = isinstance(t, int)
        if not static or 0 <= t - 2 < n:
            stage_pv(t - 2, size(t - 2) if static else tk, alphas)
        if not static or 0 <= t - 1 < n:
            ms, ls, alphas = stage_softmax(size(t - 1) if static else tk, cms, ms, ls)
        if not static or t < n:
            cms = stage_qk(t, size(t) if static else tk)
        return cms, ms, ls, alphas

    neg = jnp.full((1, tq), -jnp.inf, F32)
    zero = jnp.zeros((1, tq), F32)
    carry = ((neg, neg), (neg, neg), (jnp.zeros((8, tq), F32),) * 2, (zero, zero))
    loop_lo = first + 2
    loop_hi = max(loop_lo, n)
    for t in range(loop_lo):
        carry = tick(t, carry)
    carry = lax.fori_loop(loop_lo, loop_hi, tick, carry)
    for t in range(loop_hi, n + 2):
        carry = tick(t, carry)
    _, ms, ls, _ = carry

    lp = lamp_ref[...]
    lam = (jnp.exp(jnp.sum(lp[0:1] * lp[1:2], axis=1, keepdims=True))
           - jnp.exp(jnp.sum(lp[2:3] * lp[3:4], axis=1, keepdims=True)) + LAM_INIT)
    inv1 = 1.0 / jnp.sum(ls[0], axis=0, keepdims=True)
    inv2 = lam / jnp.sum(ls[1], axis=0, keepdims=True)
    o = (acc_sc[0] * inv1 - acc_sc[1] * inv2).T
    y = o * lax.rsqrt(jnp.mean(o * o, axis=-1, keepdims=True) + EPS) * sw_ref[...] * (1.0 - LAM_INIT)
    g = g_ref[...].astype(F32)
    o_ref[...] = (y * (g * jax.nn.sigmoid(g))).astype(o_ref.dtype)


def _attn_call(p_nr, p_plain, lam_params, subln_w, l):
    _, b, t_all, _ = p_nr.shape
    tq = _pick(l, (1024, 512, 256, 128))
    tk = ATTN_TK
    n_big, tail = divmod(t_all, tk)
    assert n_big >= 2 and tail % V7X_MXU_DIM == 0, "needs two full key chunks and a tail of whole MXU passes"
    h = DIFF_HEADS
    return pl.pallas_call(
        functools.partial(_attn_kernel, tk=tk),
        grid=(b, h, l // tq),
        in_specs=[pl.BlockSpec((4, QK_DIM), lambda bb, hh, i: (0, 0)),
                  pl.BlockSpec((None, None, tq, V_DIM), lambda bb, hh, i: (hh, bb, i, 0)),
                  pl.BlockSpec((None, None, t_all, V_DIM), lambda bb, hh, i: (h + hh, bb, 0, 0)),
                  pl.BlockSpec((None, None, t_all, V_DIM), lambda bb, hh, i: (hh, bb, 0, 0)),
                  pl.BlockSpec((None, None, tq, V_DIM), lambda bb, hh, i: (h + hh, bb, i, 0)),
                  pl.BlockSpec((1, V_DIM), lambda bb, hh, i: (0, 0))],
        out_specs=pl.BlockSpec((None, tq, V_DIM), lambda bb, hh, i: (bb, i, hh)),
        out_shape=jax.ShapeDtypeStruct((b, l, DIFF_WIDTH), BF16),
        scratch_shapes=[pltpu.VMEM((n_big, V_DIM, tk), BF16),
                        pltpu.VMEM((V_DIM, max(tail, V7X_MXU_DIM)), BF16),
                        pltpu.VMEM((2, tk, tq), F32),
                        pltpu.VMEM((2, tk, tq), BF16),
                        pltpu.VMEM((2, V_DIM, tq), F32)],
        compiler_params=_params(ARB, ARB, ARB),
        name="diff_attn",
    )(lam_params, p_nr, p_nr, p_plain, p_plain, subln_w.reshape(1, V_DIM))


def _ret_kernel(dec_ref, q_ref, k_ref, v_ref, g_ref, nw_ref, o_ref, of_sc, ob_sc, *, chunk):
    l = q_ref.shape[0]
    lc = k_ref.shape[0] - l
    nc = l // chunk
    cdim = (((1,), (1,)), ((), ()))
    rdim = (((0,), (0,)), ((), ()))

    dec = dec_ref[...]
    lg = jnp.log1p(-jnp.exp2(-dec))
    lg_f, lg_b = lg[0:1, 0:1], lg[1:2, 0:1]

    def col_iota(n):
        return lax.broadcasted_iota(jnp.int32, (n, 1), 0).astype(F32)

    ic = col_iota(chunk)
    xi_f = jnp.exp(lg_f * (ic + 1.0))
    xi_b = jnp.exp(lg_b * (chunk - ic))
    zeta_f = jnp.exp(lg_f * (chunk - 1.0 - ic))
    zeta_b = jnp.exp(lg_b * ic)
    gc_f = jnp.exp(lg_f * chunk)
    gc_b = jnp.exp(lg_b * chunk)
    ri = lax.broadcasted_iota(jnp.int32, (chunk, chunk), 0)
    ci = lax.broadcasted_iota(jnp.int32, (chunk, chunk), 1)
    dist = (ri - ci).astype(F32)
    dmat = jnp.where(ri >= ci, jnp.exp(lg_f * jnp.maximum(dist, 0.0)), jnp.exp(lg_b * jnp.maximum(-dist, 0.0)))

    im = col_iota(lc)
    kctx = k_ref[pl.ds(l, lc), :].astype(F32)
    vctx = v_ref[pl.ds(l, lc), :]
    r0_f = lax.dot_general((kctx * jnp.exp(lg_f * (lc - 1.0 - im))).astype(BF16), vctx, rdim,
                           preferred_element_type=F32)
    r0_b = lax.dot_general((kctx * jnp.exp(lg_b * im)).astype(BF16), vctx, rdim, preferred_element_type=F32)

    def chunk_kv(rows, zeta):
        kz = (k_ref[rows, :].astype(F32) * zeta).astype(BF16)
        return lax.dot_general(kz, v_ref[rows, :], rdim, preferred_element_type=F32)

    def scan_step(j, carry):
        s_f, s_b = carry
        rows_f = pl.ds(pl.multiple_of(j * chunk, chunk), chunk)
        qc = q_ref[rows_f, :]
        scores = lax.dot_general(qc, k_ref[rows_f, :], cdim, preferred_element_type=F32) * dmat
        o_f = (jnp.dot(scores.astype(BF16), v_ref[rows_f, :], preferred_element_type=F32)
               + jnp.dot(qc, s_f.astype(BF16), preferred_element_type=F32) * xi_f)
        s_f = gc_f * s_f + chunk_kv(rows_f, zeta_f)
        rows_b = pl.ds(pl.multiple_of((nc - 1 - j) * chunk, chunk), chunk)
        o_b = jnp.dot(q_ref[rows_b, :], s_b.astype(BF16), preferred_element_type=F32) * xi_b
        s_b = gc_b * s_b + chunk_kv(rows_b, zeta_b)
        return (s_f, s_b), (rows_f, o_f), (rows_b, o_b)

    def finish(rows, o):
        y = o * lax.rsqrt(jnp.mean(o * o, axis=-1, keepdims=True) + EPS) * nw_ref[...]
        g = g_ref[rows, :].astype(F32)
        o_ref[rows, :] = (y * (g * jax.nn.sigmoid(g))).astype(o_ref.dtype)

    half = nc // 2
    upper = lambda c: pl.ds(pl.multiple_of((c - half) * chunk, chunk), chunk)

    def first_half(j, carry):
        carry, (rows_f, o_f), (_, o_b) = scan_step(j, carry)
        of_sc[rows_f, :] = o_f
        ob_sc[upper(nc - 1 - j), :] = o_b
        return carry

    def second_half(j, carry):
        carry, (rows_f, o_f), (rows_b, o_b) = scan_step(j, carry)
        finish(rows_f, o_f + ob_sc[upper(j), :])
        finish(rows_b, o_b + of_sc[rows_b, :])
        return carry

    carry = lax.fori_loop(0, half, first_half, (r0_f, r0_b), unroll=RET_UNROLL)
    lax.fori_loop(half, nc, second_half, carry, unroll=RET_UNROLL)


def _ret_call(p_rope, p_plain, decs, ret_norm_w, l):
    _, b, t_all, _ = p_rope.shape
    h = RET_HEADS
    chunk = _pick(l, (256, 128))
    assert (l // chunk) % 2 == 0, "the two scan directions meet in the middle of an even chunk count"
    return pl.pallas_call(
        functools.partial(_ret_kernel, chunk=chunk),
        grid=(b, h),
        in_specs=[pl.BlockSpec((None, 2, QK_DIM), lambda bb, hh: (hh, 0, 0)),
                  pl.BlockSpec((None, None, l, QK_DIM), lambda bb, hh: (hh, bb, 0, 0)),
                  pl.BlockSpec((None, None, t_all, QK_DIM), lambda bb, hh: (h + hh, bb, 0, 0)),
                  pl.BlockSpec((None, None, t_all, V_DIM), lambda bb, hh: (2 * DIFF_HEADS + hh, bb, 0, 0)),
                  pl.BlockSpec((None, None, l, V_DIM), lambda bb, hh: (2 * DIFF_HEADS + h + hh, bb, 0, 0)),
                  pl.BlockSpec((1, V_DIM), lambda bb, hh: (0, 0))],
        out_specs=pl.BlockSpec((None, l, V_DIM), lambda bb, hh: (bb, 0, hh)),
        out_shape=jax.ShapeDtypeStruct((b, l, RET_WIDTH), BF16),
        scratch_shapes=[pltpu.VMEM((l // 2, V_DIM), F32),
                        pltpu.VMEM((l // 2, V_DIM), F32)],
        compiler_params=_params(ARB, ARB),
        name="retention",
    )(decs, p_rope, p_rope, p_plain, p_plain, ret_norm_w.reshape(1, V_DIM))


def _outproj_kernel(ad_ref, ar_ref, wd_ref, wr_ref, x_ref, gate_ref, o_ref):
    y = jnp.dot(ad_ref[...], wd_ref[...], preferred_element_type=F32)
    y = y + jnp.dot(ar_ref[...], wr_ref[...], preferred_element_type=F32)
    o_ref[...] = x_ref[...] + gate_ref[...] * y


def _outproj_call(o_d, o_r, w_out, x, gate):
    b, l, d = x.shape
    kd, kr = o_d.shape[-1], o_r.shape[-1]
    tm = _pick(l, (1024, 512, 256, 128))
    tn = _pick(d, (512, 256, 128))
    per_b = l // tm
    return pl.pallas_call(
        _outproj_kernel,
        grid=(b * per_b, d // tn),
        in_specs=[pl.BlockSpec((tm, kd), lambda i, j: (i, 0)),
                  pl.BlockSpec((tm, kr), lambda i, j: (i, 0)),
                  pl.BlockSpec((kd, tn), lambda i, j: (0, j)),
                  pl.BlockSpec((kr, tn), lambda i, j: (kd // kr, j)),
                  pl.BlockSpec((tm, tn), lambda i, j: (i, j)),
                  pl.BlockSpec((None, 1, tn), lambda i, j: (i // per_b, 0, j))],
        out_specs=pl.BlockSpec((tm, tn), lambda i, j: (i, j)),
        out_shape=jax.ShapeDtypeStruct((b * l, d), F32),
        compiler_params=_params(ARB, ARB),
        name="outproj",
    )(o_d.reshape(b * l, kd), o_r.reshape(b * l, kr), w_out, w_out, x.reshape(b * l, d),
      gate.reshape(b, 1, d)).reshape(b, l, d)


def _rope_tables(l, lc):
    n_rows = l // GRID_W
    half = QK_DIM // 2
    inv_freq = ROPE_BASE ** (-jnp.arange(0, half, 2, dtype=F32) / half)
    ang_r = jnp.arange(n_rows, dtype=F32)[:, None] * inv_freq
    ang_c = jnp.arange(GRID_W, dtype=F32)[:, None] * inv_freq

    def spread(fn):
        by_row = jnp.broadcast_to(fn(ang_r)[:, None, :], (n_rows, GRID_W, half // 2)).reshape(l, half // 2)
        by_col = jnp.broadcast_to(fn(ang_c)[None, :, :], (n_rows, GRID_W, half // 2)).reshape(l, half // 2)
        return jnp.concatenate([by_row, by_row, by_col, by_col], axis=-1)

    cos, sin = spread(jnp.cos), spread(jnp.sin)
    lo = (jnp.arange(QK_DIM) % half) < (half // 2)
    cos = jnp.concatenate([cos, jnp.ones((lc, QK_DIM), F32)], axis=0)
    sin_lo = jnp.concatenate([jnp.where(lo, -sin, 0.0), jnp.zeros((lc, QK_DIM), F32)], axis=0)
    sin_hi = jnp.concatenate([jnp.where(lo, 0.0, sin), jnp.zeros((lc, QK_DIM), F32)], axis=0)
    return cos, sin_lo, sin_hi


def _layer(x, ctx, c, c_ctx, norm_w, ada_w, ada_b, w_in, q_norm_w, k_norm_w, lam_params, subln_w,
           dec_f, dec_b, ret_norm_w, w_out):
    b, l, d = x.shape
    lc = ctx.shape[1]
    t_all = l + lc

    rows = -(-(b + 1) // 8) * 8
    cc = jnp.zeros((rows, d), F32).at[:b].set(c).at[b].set(c_ctx)
    mod = _ada_call(cc, ada_w, ada_b)
    shift, scale, gate = mod[:b, :d], mod[:b, d:2 * d], mod[:b, 2 * d:]
    shift_c = jnp.broadcast_to(mod[b, :d], (b, d))
    scale_c = jnp.broadcast_to(mod[b, d:2 * d], (b, d))
    mods = jnp.stack([scale, shift, scale_c, shift_c], axis=1)

    h_all = _norm_call(x, ctx, norm_w, mods).reshape(b * t_all, d)

    d_in = w_in.shape[1]
    w_qk = _cast_to_bf16(w_in, O_DQ, O_DV)
    tables = _rope_tables(l, lc)
    blocks = lambda lo, hi, base: list(range((lo - base) // INPROJ_TN, (hi - base) // INPROJ_TN))
    qk_vecs = jnp.stack([q_norm_w, k_norm_w]).reshape(2, 1, QK_DIM)
    ret_vecs = jnp.stack([jnp.ones((QK_DIM,), F32), jnp.full((QK_DIM,), QK_DIM ** -0.5, F32)]).reshape(2, 1, QK_DIM)
    call = functools.partial(_inproj_call, h_all, tables=tables, rows_per_batch=t_all)
    p_nr, w_rest = call(w=w_qk, vecs=qk_vecs, mode="normrope", col_blocks=blocks(O_DQ, O_DV, O_DQ), out_w=V_DIM,
                        cast=(w_in, O_DV, d_in - O_DV))
    p_rope, w_out_b = call(w=w_rest, vecs=ret_vecs, mode="rope", col_blocks=blocks(O_RQ, O_RV, O_DV), out_w=QK_DIM,
                           cast=(w_out, 0, w_out.shape[1]))
    p_plain = call(w=w_rest, vecs=None, mode="plain",
                   col_blocks=blocks(O_DV, O_RQ, O_DV) + blocks(O_RV, d_in, O_DV), out_w=V_DIM)
    p_nr = p_nr.reshape(-1, b, t_all, V_DIM)
    p_rope = p_rope.reshape(-1, b, t_all, QK_DIM)
    p_plain = p_plain.reshape(-1, b, t_all, V_DIM)

    o_d = _attn_call(p_nr, p_plain, lam_params, subln_w, l)
    decs = jnp.broadcast_to(jnp.stack([dec_f, dec_b], axis=1)[:, :, None], (RET_HEADS, 2, QK_DIM)).astype(F32)
    o_r = _ret_call(p_rope, p_plain, decs, ret_norm_w, l)

    return _outproj_call(o_d, o_r, w_out_b, x, gate)


def kernel(x, c, ctx, c_ctx, norm_w, ada_w, ada_b, w_in, diff_q_norm_w, diff_k_norm_w, diff_lambda_q1,
           diff_lambda_k1, diff_lambda_q2, diff_lambda_k2, diff_subln_w, ret_decay_fwd, ret_decay_bwd,
           ret_norm_w, w_out):
    depth = norm_w.shape[0]
    assert depth == 1, "LAM_INIT is the layer-0 value"
    for layer in range(depth):
        lam_params = jnp.stack([diff_lambda_q1[layer], diff_lambda_k1[layer],
                                diff_lambda_q2[layer], diff_lambda_k2[layer]]).astype(F32)
        x = _layer(x, ctx, c, c_ctx, norm_w[layer], ada_w[layer], ada_b[layer], w_in[layer],
                   diff_q_norm_w[layer], diff_k_norm_w[layer], lam_params, diff_subln_w[layer],
                   ret_decay_fwd[layer], ret_decay_bwd[layer], ret_norm_w[layer], w_out[layer])
    return x
```

```python
import functools
import math

import jax
import jax.numpy as jnp
from jax import lax
from jax.experimental import pallas as pl
from jax.experimental.pallas import tpu as pltpu

F32 = jnp.float32
BF16 = jnp.bfloat16

GRID_W = 64
DIFF_HEADS = 8
QK_DIM = 128
V_DIM = 256
RET_HEADS = 8
ROPE_BASE = 10000.0
EPS = 1e-6
LAM_INIT = 0.8 - 0.6 * math.exp(-0.3 * 0)

DIFF_QK_W = DIFF_HEADS * 2 * QK_DIM
DIFF_WIDTH = DIFF_HEADS * V_DIM
RET_QK_W = RET_HEADS * QK_DIM
RET_WIDTH = RET_HEADS * V_DIM
O_DQ = 0
O_DK = O_DQ + DIFF_QK_W
O_DV = O_DK + DIFF_QK_W
O_DG = O_DV + DIFF_WIDTH
O_RQ = O_DG + DIFF_WIDTH
O_RK = O_RQ + RET_QK_W
O_RV = O_RK + RET_QK_W
O_RG = O_RV + RET_WIDTH

V7X_VMEM_LIMIT_BYTES = 56 * 1024 * 1024
V7X_MXU_DIM = 256
ARB = "arbitrary"
RET_UNROLL = 8
INPROJ_TN = 1024
ATTN_TK = 1024
CAST_ROWS, CAST_COLS = 256, 2048


def _pick(n, candidates):
    for c in candidates:
        if n % c == 0:
            return c
    raise ValueError(f"no tile in {candidates} divides {n}")


def _params(*sem):
    return pltpu.CompilerParams(dimension_semantics=sem, vmem_limit_bytes=V7X_VMEM_LIMIT_BYTES)


def _ada_kernel(c_ref, w_ref, b_ref, o_ref):
    c = c_ref[...]
    a = (c * jax.nn.sigmoid(c)).astype(BF16)
    o_ref[...] = jnp.dot(a, w_ref[...].astype(BF16), preferred_element_type=F32) + b_ref[...]


def _ada_call(cc, ada_w, ada_b):
    rows, d = cc.shape
    n = ada_w.shape[1]
    tn = _pick(n, (512, 256, 128))
    return pl.pallas_call(
        _ada_kernel,
        grid=(n // tn,),
        in_specs=[pl.BlockSpec((rows, d), lambda j: (0, 0)),
                  pl.BlockSpec((d, tn), lambda j: (0, j)),
                  pl.BlockSpec((1, tn), lambda j: (0, j))],
        out_specs=pl.BlockSpec((rows, tn), lambda j: (0, j)),
        out_shape=jax.ShapeDtypeStruct((rows, n), F32),
        compiler_params=_params(ARB),
        name="ada_mod",
    )(cc, ada_w, ada_b.reshape(1, n))


def _norm_kernel(x_ref, ctx_ref, nw_ref, mod_ref, o_ref, *, n_lat):
    i = pl.program_id(1)

    def norm_mod(xv, scale, shift):
        ms = jnp.mean(xv * xv, axis=-1, keepdims=True)
        gain = nw_ref[...] * (1.0 + scale)
        return (xv * lax.rsqrt(ms + EPS) * gain + shift).astype(o_ref.dtype)

    @pl.when(i < n_lat)
    def _():
        o_ref[...] = norm_mod(x_ref[...], mod_ref[0:1, :], mod_ref[1:2, :])

    @pl.when(i >= n_lat)
    def _():
        o_ref[...] = norm_mod(ctx_ref[...], mod_ref[2:3, :], mod_ref[3:4, :])


def _norm_call(x, ctx, norm_w, mods):
    b, l, d = x.shape
    lc = ctx.shape[1]
    tr = _pick(math.gcd(l, lc), (256, 128, 64, 32, 16))
    n_lat, n_ctx = l // tr, lc // tr
    return pl.pallas_call(
        functools.partial(_norm_kernel, n_lat=n_lat),
        grid=(b, n_lat + n_ctx),
        in_specs=[pl.BlockSpec((None, tr, d), lambda bb, i: (bb, jnp.minimum(i, n_lat - 1), 0)),
                  pl.BlockSpec((None, tr, d), lambda bb, i: (bb, jnp.maximum(i - n_lat, 0), 0)),
                  pl.BlockSpec((1, d), lambda bb, i: (0, 0)),
                  pl.BlockSpec((None, 4, d), lambda bb, i: (bb, 0, 0))],
        out_specs=pl.BlockSpec((None, tr, d), lambda bb, i: (bb, i, 0)),
        out_shape=jax.ShapeDtypeStruct((b, l + lc, d), BF16),
        compiler_params=_params(ARB, ARB),
        name="norm_mod",
    )(x, ctx, norm_w.reshape(1, d), mods)


def _rope(xh, cos, sin_lo, sin_hi):
    return xh * cos + pltpu.roll(xh, 96, 1) * sin_lo + pltpu.roll(xh, 32, 1) * sin_hi


def _inproj_plain_kernel(cb_ref, a_ref, w_ref, o_ref, *, out_w):
    del cb_ref
    acc = jnp.dot(a_ref[...], w_ref[...], preferred_element_type=F32)
    for c in range(acc.shape[1] // out_w):
        o_ref[c] = acc[:, c * out_w:(c + 1) * out_w].astype(o_ref.dtype)


def _inproj_rope_kernel(cb_ref, a_ref, w_ref, cos_ref, slo_ref, shi_ref, vec_ref, *rest, norm, out_w, cast_steps):
    del cb_ref
    if cast_steps:
        cast_in_ref, o_ref, cast_out_ref, acc_sc = rest
    else:
        o_ref, acc_sc = rest

    @pl.when(pl.program_id(0) == 0)
    def _():
        acc_sc[...] = jnp.zeros_like(acc_sc)


    cos, slo, shi, vec = cos_ref[...], slo_ref[...], shi_ref[...], vec_ref[...]
    per = out_w // QK_DIM
    for t in range(acc_sc.shape[1] // QK_DIM):
        xh = acc_sc[:, t * QK_DIM:(t + 1) * QK_DIM]
        if norm:
            xh = xh * lax.rsqrt(jnp.mean(xh * xh, axis=-1, keepdims=True) + EPS) * vec
            r = _rope(xh, cos, slo, shi)
        else:
            r = _rope(xh, cos, slo, shi) * vec
        lane0 = (t % per) * QK_DIM
        o_ref[t // per, :, lane0:lane0 + QK_DIM] = r.astype(o_ref.dtype)
    acc_sc[...] = jnp.dot(a_ref[...], w_ref[...], preferred_element_type=F32)
    if cast_steps:
        cast_out_ref[...] = cast_in_ref[...].astype(cast_out_ref.dtype)


def _cast_to_bf16(src, col0, ncols):
    return src[:, col0:col0 + ncols].astype(BF16)


def _inproj_call(a2d, w, tables, vecs, *, mode, col_blocks, out_w, rows_per_batch, cast=None):
    m, d = a2d.shape
    tn = INPROJ_TN
    tm = _pick(rows_per_batch, (768, 512, 384, 256, 128))
    n_i, n_j = m // tm, len(col_blocks)
    n_out = n_j * tn // out_w
    col_tab = jnp.asarray(col_blocks, jnp.int32)
    out_shape = jax.ShapeDtypeStruct((n_out, m, out_w), BF16)
    if mode == "plain":
        return pl.pallas_call(
            functools.partial(_inproj_plain_kernel, out_w=out_w),
            grid_spec=pltpu.PrefetchScalarGridSpec(
                num_scalar_prefetch=1,
                grid=(n_i, n_j),
                in_specs=[pl.BlockSpec((tm, d), lambda i, j, cb: (i, 0)),
                          pl.BlockSpec((d, tn), lambda i, j, cb: (0, cb[j]))],
                out_specs=pl.BlockSpec((tn // out_w, tm, out_w), lambda i, j, cb: (j, i, 0)),
            ),
            out_shape=out_shape,
            compiler_params=_params(ARB, ARB),
            name="inproj_plain",
        )(col_tab, a2d, w)

    steps = n_i * n_j
    t_blocks = rows_per_batch // tm
    j_per_vec = n_j // vecs.shape[0]
    cur = lambda s: jnp.minimum(s, steps - 1)
    prev = lambda s: jnp.maximum(s - 1, 0)
    tab_spec = pl.BlockSpec((tm, QK_DIM), lambda s, cb: ((prev(s) // n_j) % t_blocks, 0))
    in_specs = [pl.BlockSpec((tm, d), lambda s, cb: (cur(s) // n_j, 0)),
                pl.BlockSpec((d, tn), lambda s, cb: (0, cb[cur(s) % n_j])),
                tab_spec, tab_spec, tab_spec,
                pl.BlockSpec((None, 1, QK_DIM), lambda s, cb: ((prev(s) % n_j) // j_per_vec, 0, 0))]
    out_specs = [pl.BlockSpec((tn // out_w, tm, out_w), lambda s, cb: (prev(s) % n_j, prev(s) // n_j, 0))]
    out_shapes = [out_shape]
    args = [a2d, w, *tables, vecs]

    cast_src, cast_col0, cast_ncols = cast
    cast_rows = cast_src.shape[0]
    n_c = cast_ncols // CAST_COLS
    cast_steps = (cast_rows // CAST_ROWS) * n_c
    ride = (cast_ncols % CAST_COLS == 0 and cast_col0 % CAST_COLS == 0 and cast_rows % CAST_ROWS == 0
            and cast_steps <= steps)
    if ride:
        blk = lambda s: jnp.minimum(s, cast_steps - 1)
        c0 = cast_col0 // CAST_COLS
        in_specs.append(pl.BlockSpec((CAST_ROWS, CAST_COLS), lambda s, cb: (blk(s) // n_c, c0 + blk(s) % n_c)))
        out_specs.append(pl.BlockSpec((CAST_ROWS, CAST_COLS), lambda s, cb: (blk(s) // n_c, blk(s) % n_c)))
        out_shapes.append(jax.ShapeDtypeStruct((cast_rows, cast_ncols), BF16))
        args.append(cast_src)
    res = pl.pallas_call(
        functools.partial(_inproj_rope_kernel, norm=(mode == "normrope"), out_w=out_w,
                          cast_steps=cast_steps if ride else 0),
        grid_spec=pltpu.PrefetchScalarGridSpec(
            num_scalar_prefetch=1,
            grid=(steps + 1,),
            in_specs=in_specs,
            out_specs=out_specs,
            scratch_shapes=[pltpu.VMEM((tm, tn), F32)],
        ),
        out_shape=out_shapes,
        compiler_params=_params(ARB),
        name="inproj_" + mode,
    )(col_tab, *args)
    if ride:
        return res[0], res[1]
    return res[0], _cast_to_bf16(cast_src, cast_col0, cast_ncols)


def _attn_kernel(lamp_ref, q_ref, k_ref, v_ref, g_ref, sw_ref, o_ref, vt_sc, vt_tail_sc, s_sc, p_sc, acc_sc, *, tk):
    t_all = k_ref.shape[0]
    tq = q_ref.shape[0]
    n_big, tail = divmod(t_all, tk)
    first = 1 if tail else 0
    n = n_big + first
    size = lambda c: tail if c < first else tk
    c_exp = (QK_DIM ** -0.5) * math.log2(math.e)

    def transposed(rows):
        return v_ref[rows, :].astype(F32).T.astype(BF16)

    @pl.when(pl.program_id(2) == 0)
    def _():
        def transpose_chunk(c, carry):
            vt_sc[c] = transposed(pl.ds(pl.multiple_of(c * tk, tk), tk))
            return carry
        lax.fori_loop(0, n_big, transpose_chunk, 0)
        if tail:
            vt_tail_sc[...] = transposed(pl.ds(n_big * tk, tail))

    qt = q_ref[...].astype(F32).T.astype(BF16)
    qts = (qt[:QK_DIM], qt[QK_DIM:])
    acc_sc[...] = jnp.zeros_like(acc_sc)

    def stage_qk(c, rows):
        if rows != tk:
            start = n_big * tk
        else:
            start = (c - first) * tk if isinstance(c, int) else pl.multiple_of((c - first) * tk, tk)
        kc = k_ref[pl.ds(start, rows), :]
        cms = []
        for sub in range(2):
            s = jnp.dot(kc[:, sub * QK_DIM:(sub + 1) * QK_DIM], qts[sub],
                        preferred_element_type=F32)
            s_sc[sub, :rows] = s
            cms.append(jnp.max(s, axis=0, keepdims=True))
        return tuple(cms)

    def stage_softmax(rows, cms, ms, ls):
        new_m, new_l, alphas = [], [], []
        for sub in range(2):
            m_new = jnp.maximum(ms[sub], cms[sub])
            alpha = jnp.exp2((ms[sub] - m_new) * c_exp)
            p = jnp.exp2(s_sc[sub, :rows] * c_exp - m_new * c_exp)
            new_l.append(alpha * ls[sub] + jnp.sum(p.reshape(rows // 8, 8, tq), axis=0))
            p_sc[sub, :rows] = p.astype(BF16)
            new_m.append(m_new)
            alphas.append(alpha)
        return tuple(new_m), tuple(new_l), tuple(alphas)

    def stage_pv(c, rows, alphas):
        vt = vt_sc[c - first] if rows == tk else vt_tail_sc[...]
        for sub in range(2):
            acc_sc[sub] = alphas[sub] * acc_sc[sub] + jnp.dot(vt, p_sc[sub, :rows], preferred_element_type=F32)

    def tick(t, carry):
        cms, ms, ls, alphas = carry
        static = isinstance(t, int)
        if not static or 0 <= t - 2 < n:
            stage_pv(t - 2, size(t - 2) if static else tk, alphas)
        if not static or 0 <= t - 1 < n:
            ms, ls, alphas = stage_softmax(size(t - 1) if static else tk, cms, ms, ls)
        if not static or t < n:
            cms = stage_qk(t, size(t) if static else tk)
        return cms, ms, ls, alphas

    neg = jnp.full((1, tq), -jnp.inf, F32)
    zero = jnp.zeros((1, tq), F32)
    carry = ((neg, neg), (neg, neg), (jnp.zeros((8, tq), F32),) * 2, (zero, zero))
    loop_lo = first + 2
    loop_hi = max(loop_lo, n)
    for t in range(loop_lo):
        carry = tick(t, carry)
    carry = lax.fori_loop(loop_lo, loop_hi, tick, carry)
    for t in range(loop_hi, n + 2):
        carry = tick(t, carry)
    _, ms, ls, _ = carry

    lp = lamp_ref[...]
    lam = (jnp.exp(jnp.sum(lp[0:1] * lp[1:2], axis=1, keepdims=True))
           - jnp.exp(jnp.sum(lp[2:3] * lp[3:4], axis=1, keepdims=True)) + LAM_INIT)
    inv1 = 1.0 / jnp.sum(ls[0], axis=0, keepdims=True)
    inv2 = lam / jnp.sum(ls[1], axis=0, keepdims=True)
    o = (acc_sc[0] * inv1 - acc_sc[1] * inv2).T
    y = o * lax.rsqrt(jnp.mean(o * o, axis=-1, keepdims=True) + EPS) * sw_ref[...] * (1.0 - LAM_INIT)
    g = g_ref[...].astype(F32)
    o_ref[...] = (y * (g * jax.nn.sigmoid(g))).astype(o_ref.dtype)


def _attn_call(p_nr, p_plain, lam_params, subln_w, l):
    _, b, t_all, _ = p_nr.shape
    tq = _pick(l, (1024, 512, 256, 128))
    tk = ATTN_TK
    n_big, tail = divmod(t_all, tk)
    assert n_big >= 2 and tail % V7X_MXU_DIM == 0, "needs two full key chunks and a tail of whole MXU passes"
    h = DIFF_HEADS
    return pl.pallas_call(
        functools.partial(_attn_kernel, tk=tk),
        grid=(b, h, l // tq),
        in_specs=[pl.BlockSpec((4, QK_DIM), lambda bb, hh, i: (0, 0)),
                  pl.BlockSpec((None, None, tq, V_DIM), lambda bb, hh, i: (hh, bb, i, 0)),
                  pl.BlockSpec((None, None, t_all, V_DIM), lambda bb, hh, i: (h + hh, bb, 0, 0)),
                  pl.BlockSpec((None, None, t_all, V_DIM), lambda bb, hh, i: (hh, bb, 0, 0)),
                  pl.BlockSpec((None, None, tq, V_DIM), lambda bb, hh, i: (h + hh, bb, i, 0)),
                  pl.BlockSpec((1, V_DIM), lambda bb, hh, i: (0, 0))],
        out_specs=pl.BlockSpec((None, tq, V_DIM), lambda bb, hh, i: (bb, i, hh)),
        out_shape=jax.ShapeDtypeStruct((b, l, DIFF_WIDTH), BF16),
        scratch_shapes=[pltpu.VMEM((n_big, V_DIM, tk), BF16),
                        pltpu.VMEM((V_DIM, max(tail, V7X_MXU_DIM)), BF16),
                        pltpu.VMEM((2, tk, tq), F32),
                        pltpu.VMEM((2, tk, tq), BF16),
                        pltpu.VMEM((2, V_DIM, tq), F32)],
        compiler_params=_params(ARB, ARB, ARB),
        name="diff_attn",
    )(lam_params, p_nr, p_nr, p_plain, p_plain, subln_w.reshape(1, V_DIM))


def _ret_kernel(dec_ref, q_ref, k_ref, v_ref, g_ref, nw_ref, o_ref, of_sc, ob_sc, *, chunk):
    l = q_ref.shape[0]
    lc = k_ref.shape[0] - l
    nc = l // chunk
    cdim = (((1,), (1,)), ((), ()))
    rdim = (((0,), (0,)), ((), ()))

    dec = dec_ref[...]
    lg = jnp.log1p(-jnp.exp2(-dec))
    lg_f, lg_b = lg[0:1, 0:1], lg[1:2, 0:1]

    def col_iota(n):
        return lax.broadcasted_iota(jnp.int32, (n, 1), 0).astype(F32)

    ic = col_iota(chunk)
    xi_f = jnp.exp(lg_f * (ic + 1.0))
    xi_b = jnp.exp(lg_b * (chunk - ic))
    zeta_f = jnp.exp(lg_f * (chunk - 1.0 - ic))
    zeta_b = jnp.exp(lg_b * ic)
    gc_f = jnp.exp(lg_f * chunk)
    gc_b = jnp.exp(lg_b * chunk)
    ri = lax.broadcasted_iota(jnp.int32, (chunk, chunk), 0)
    ci = lax.broadcasted_iota(jnp.int32, (chunk, chunk), 1)
    dist = (ri - ci).astype(F32)
    dmat = jnp.where(ri >= ci, jnp.exp(lg_f * jnp.maximum(dist, 0.0)), jnp.exp(lg_b * jnp.maximum(-dist, 0.0)))

    im = col_iota(lc)
    kctx = k_ref[pl.ds(l, lc), :].astype(F32)
    vctx = v_ref[pl.ds(l, lc), :]
    r0_f = lax.dot_general((kctx * jnp.exp(lg_f * (lc - 1.0 - im))).astype(BF16), vctx, rdim,
                           preferred_element_type=F32)
    r0_b = lax.dot_general((kctx * jnp.exp(lg_b * im)).astype(BF16), vctx, rdim, preferred_element_type=F32)

    def chunk_kv(rows, zeta):
        kz = (k_ref[rows, :].astype(F32) * zeta).astype(BF16)
        return lax.dot_general(kz, v_ref[rows, :], rdim, preferred_element_type=F32)

    def scan_step(j, carry):
        s_f, s_b = carry
        rows_f = pl.ds(pl.multiple_of(j * chunk, chunk), chunk)
        qc = q_ref[rows_f, :]
        scores = lax.dot_general(qc, k_ref[rows_f, :], cdim, preferred_element_type=F32) * dmat
        o_f = (jnp.dot(scores.astype(BF16), v_ref[rows_f, :], preferred_element_type=F32)
               + jnp.dot(qc, s_f.astype(BF16), preferred_element_type=F32) * xi_f)
        s_f = gc_f * s_f + chunk_kv(rows_f, zeta_f)
        rows_b = pl.ds(pl.multiple_of((nc - 1 - j) * chunk, chunk), chunk)
        o_b = jnp.dot(q_ref[rows_b, :], s_b.astype(BF16), preferred_element_type=F32) * xi_b
        s_b = gc_b * s_b + chunk_kv(rows_b, zeta_b)
        return (s_f, s_b), (rows_f, o_f), (rows_b, o_b)

    def finish(rows, o):
        y = o * lax.rsqrt(jnp.mean(o * o, axis=-1, keepdims=True) + EPS) * nw_ref[...]
        g = g_ref[rows, :].astype(F32)
        o_ref[rows, :] = (y * (g * jax.nn.sigmoid(g))).astype(o_ref.dtype)

    half = nc // 2
    upper = lambda c: pl.ds(pl.multiple_of((c - half) * chunk, chunk), chunk)

    def first_half(j, carry):
        carry, (rows_f, o_f), (_, o_b) = scan_step(j, carry)
        of_sc[rows_f, :] = o_f
        ob_sc[upper(nc - 1 - j), :] = o_b
        return carry

    def second_half(j, carry):
        carry, (rows_f, o_f), (rows_b, o_b) = scan_step(j, carry)
        finish(rows_f, o_f + ob_sc[upper(j), :])
        finish(rows_b, o_b + of_sc[rows_b, :])
        return carry

    carry = lax.fori_loop(0, half, first_half, (r0_f, r0_b), unroll=RET_UNROLL)
    lax.fori_loop(half, nc, second_half, carry, unroll=RET_UNROLL)


def _ret_call(p_rope, p_plain, decs, ret_norm_w, l):
    _, b, t_all, _ = p_rope.shape
    h = RET_HEADS
    chunk = _pick(l, (256, 128))
    assert (l // chunk) % 2 == 0, "the two scan directions meet in the middle of an even chunk count"
    return pl.pallas_call(
        functools.partial(_ret_kernel, chunk=chunk),
        grid=(b, h),
        in_specs=[pl.BlockSpec((None, 2, QK_DIM), lambda bb, hh: (hh, 0, 0)),
                  pl.BlockSpec((None, None, l, QK_DIM), lambda bb, hh: (hh, bb, 0, 0)),
                  pl.BlockSpec((None, None, t_all, QK_DIM), lambda bb, hh: (h + hh, bb, 0, 0)),
                  pl.BlockSpec((None, None, t_all, V_DIM), lambda bb, hh: (2 * DIFF_HEADS + hh, bb, 0, 0)),
                  pl.BlockSpec((None, None, l, V_DIM), lambda bb, hh: (2 * DIFF_HEADS + h + hh, bb, 0, 0)),
                  pl.BlockSpec((1, V_DIM), lambda bb, hh: (0, 0))],
        out_specs=pl.BlockSpec((None, l, V_DIM), lambda bb, hh: (bb, 0, hh)),
        out_shape=jax.ShapeDtypeStruct((b, l, RET_WIDTH), BF16),
        scratch_shapes=[pltpu.VMEM((l // 2, V_DIM), F32),
                        pltpu.VMEM((l // 2, V_DIM), F32)],
        compiler_params=_params(ARB, ARB),
        name="retention",
    )(decs, p_rope, p_rope, p_plain, p_plain, ret_norm_w.reshape(1, V_DIM))


def _outproj_kernel(ad_ref, ar_ref, wd_ref, wr_ref, x_ref, gate_ref, o_ref):
    y = jnp.dot(ad_ref[...], wd_ref[...], preferred_element_type=F32)
    y = y + jnp.dot(ar_ref[...], wr_ref[...], preferred_element_type=F32)
    o_ref[...] = x_ref[...] + gate_ref[...] * y


def _outproj_call(o_d, o_r, w_out, x, gate):
    b, l, d = x.shape
    kd, kr = o_d.shape[-1], o_r.shape[-1]
    tm = _pick(l, (1024, 512, 256, 128))
    tn = _pick(d, (512, 256, 128))
    per_b = l // tm
    return pl.pallas_call(
        _outproj_kernel,
        grid=(b * per_b, d // tn),
        in_specs=[pl.BlockSpec((tm, kd), lambda i, j: (i, 0)),
                  pl.BlockSpec((tm, kr), lambda i, j: (i, 0)),
                  pl.BlockSpec((kd, tn), lambda i, j: (0, j)),
                  pl.BlockSpec((kr, tn), lambda i, j: (kd // kr, j)),
                  pl.BlockSpec((tm, tn), lambda i, j: (i, j)),
                  pl.BlockSpec((None, 1, tn), lambda i, j: (i // per_b, 0, j))],
        out_specs=pl.BlockSpec((tm, tn), lambda i, j: (i, j)),
        out_shape=jax.ShapeDtypeStruct((b * l, d), F32),
        compiler_params=_params(ARB, ARB),
        name="outproj",
    )(o_d.reshape(b * l, kd), o_r.reshape(b * l, kr), w_out, w_out, x.reshape(b * l, d),
      gate.reshape(b, 1, d)).reshape(b, l, d)


def _rope_tables(l, lc):
    n_rows = l // GRID_W
    half = QK_DIM // 2
    inv_freq = ROPE_BASE ** (-jnp.arange(0, half, 2, dtype=F32) / half)
    ang_r = jnp.arange(n_rows, dtype=F32)[:, None] * inv_freq
    ang_c = jnp.arange(GRID_W, dtype=F32)[:, None] * inv_freq

    def spread(fn):
        by_row = jnp.broadcast_to(fn(ang_r)[:, None, :], (n_rows, GRID_W, half // 2)).reshape(l, half // 2)
        by_col = jnp.broadcast_to(fn(ang_c)[None, :, :], (n_rows, GRID_W, half // 2)).reshape(l, half // 2)
        return jnp.concatenate([by_row, by_row, by_col, by_col], axis=-1)

    cos, sin = spread(jnp.cos), spread(jnp.sin)
    lo = (jnp.arange(QK_DIM) % half) < (half // 2)
    cos = jnp.concatenate([cos, jnp.ones((lc, QK_DIM), F32)], axis=0)
    sin_lo = jnp.concatenate([jnp.where(lo, -sin, 0.0), jnp.zeros((lc, QK_DIM), F32)], axis=0)
    sin_hi = jnp.concatenate([jnp.where(lo, 0.0, sin), jnp.zeros((lc, QK_DIM), F32)], axis=0)
    return cos, sin_lo, sin_hi


def _layer(x, ctx, c, c_ctx, norm_w, ada_w, ada_b, w_in, q_norm_w, k_norm_w, lam_params, subln_w,
           dec_f, dec_b, ret_norm_w, w_out):
    b, l, d = x.shape
    lc = ctx.shape[1]
    t_all = l + lc

    rows = -(-(b + 1) // 8) * 8
    cc = jnp.zeros((rows, d), F32).at[:b].set(c).at[b].set(c_ctx)
    mod = _ada_call(cc, ada_w, ada_b)
    shift, scale, gate = mod[:b, :d], mod[:b, d:2 * d], mod[:b, 2 * d:]
    shift_c = jnp.broadcast_to(mod[b, :d], (b, d))
    scale_c = jnp.broadcast_to(mod[b, d:2 * d], (b, d))
    mods = jnp.stack([scale, shift, scale_c, shift_c], axis=1)

    h_all = _norm_call(x, ctx, norm_w, mods).reshape(b * t_all, d)

    d_in = w_in.shape[1]
    w_qk = _cast_to_bf16(w_in, O_DQ, O_DV)
    tables = _rope_tables(l, lc)
    blocks = lambda lo, hi, base: list(range((lo - base) // INPROJ_TN, (hi - base) // INPROJ_TN))
    qk_vecs = jnp.stack([q_norm_w, k_norm_w]).reshape(2, 1, QK_DIM)
    ret_vecs = jnp.stack([jnp.ones((QK_DIM,), F32), jnp.full((QK_DIM,), QK_DIM ** -0.5, F32)]).reshape(2, 1, QK_DIM)
    call = functools.partial(_inproj_call, h_all, tables=tables, rows_per_batch=t_all)
    p_nr, w_rest = call(w=w_qk, vecs=qk_vecs, mode="normrope", col_blocks=blocks(O_DQ, O_DV, O_DQ), out_w=V_DIM,
                        cast=(w_in, O_DV, d_in - O_DV))
    p_rope, w_out_b = call(w=w_rest, vecs=ret_vecs, mode="rope", col_blocks=blocks(O_RQ, O_RV, O_DV), out_w=QK_DIM,
                           cast=(w_out, 0, w_out.shape[1]))
    p_plain = call(w=w_rest, vecs=None, mode="plain",
                   col_blocks=blocks(O_DV, O_RQ, O_DV) + blocks(O_RV, d_in, O_DV), out_w=V_DIM)
    p_nr = p_nr.reshape(-1, b, t_all, V_DIM)
    p_rope = p_rope.reshape(-1, b, t_all, QK_DIM)
    p_plain = p_plain.reshape(-1, b, t_all, V_DIM)

    o_d = _attn_call(p_nr, p_plain, lam_params, subln_w, l)
    decs = jnp.broadcast_to(jnp.stack([dec_f, dec_b], axis=1)[:, :, None], (RET_HEADS, 2, QK_DIM)).astype(F32)
    o_r = _ret_call(p_rope, p_plain, decs, ret_norm_w, l)

    return _outproj_call(o_d, o_r, w_out_b, x, gate)


def kernel(x, c, ctx, c_ctx, norm_w, ada_w, ada_b, w_in, diff_q_norm_w, diff_k_norm_w, diff_lambda_q1,
           diff_lambda_k1, diff_lambda_q2, diff_lambda_k2, diff_subln_w, ret_decay_fwd, ret_decay_bwd,
           ret_norm_w, w_out):
    depth = norm_w.shape[0]
    assert depth == 1, "LAM_INIT is the layer-0 value"
    for layer in range(depth):
        lam_params = jnp.stack([diff_lambda_q1[layer], diff_lambda_k1[layer],
                                diff_lambda_q2[layer], diff_lambda_k2[layer]]).astype(F32)
        x = _layer(x, ctx, c, c_ctx, norm_w[layer], ada_w[layer], ada_b[layer], w_in[layer],
                   diff_q_norm_w[layer], diff_k_norm_w[layer], lam_params, diff_subln_w[layer],
                   ret_decay_fwd[layer], ret_decay_bwd[layer], ret_norm_w[layer], w_out[layer])
    return x
```

```python
import functools
import math

import jax
import jax.numpy as jnp
from jax import lax
from jax.experimental import pallas as pl
from jax.experimental.pallas import tpu as pltpu

F32 = jnp.float32
BF16 = jnp.bfloat16

GRID_W = 64
DIFF_HEADS = 8
QK_DIM = 128
V_DIM = 256
RET_HEADS = 8
ROPE_BASE = 10000.0
EPS = 1e-6
LAM_INIT = 0.8 - 0.6 * math.exp(-0.3 * 0)

DIFF_QK_W = DIFF_HEADS * 2 * QK_DIM
DIFF_WIDTH = DIFF_HEADS * V_DIM
RET_QK_W = RET_HEADS * QK_DIM
RET_WIDTH = RET_HEADS * V_DIM
O_DQ = 0
O_DK = O_DQ + DIFF_QK_W
O_DV = O_DK + DIFF_QK_W
O_DG = O_DV + DIFF_WIDTH
O_RQ = O_DG + DIFF_WIDTH
O_RK = O_RQ + RET_QK_W
O_RV = O_RK + RET_QK_W
O_RG = O_RV + RET_WIDTH

V7X_VMEM_LIMIT_BYTES = 56 * 1024 * 1024
V7X_MXU_DIM = 256
ARB = "arbitrary"
RET_UNROLL = 8
INPROJ_TN = 1024
ATTN_TK = 2048
CAST_ROWS, CAST_COLS = 256, 2048


def _pick(n, candidates):
    for c in candidates:
        if n % c == 0:
            return c
    raise ValueError(f"no tile in {candidates} divides {n}")


def _params(*sem):
    return pltpu.CompilerParams(dimension_semantics=sem, vmem_limit_bytes=V7X_VMEM_LIMIT_BYTES)


def _ada_kernel(c_ref, w_ref, b_ref, o_ref):
    c = c_ref[...]
    a = (c * jax.nn.sigmoid(c)).astype(BF16)
    o_ref[...] = jnp.dot(a, w_ref[...].astype(BF16), preferred_element_type=F32) + b_ref[...]


def _ada_call(cc, ada_w, ada_b):
    rows, d = cc.shape
    n = ada_w.shape[1]
    tn = _pick(n, (512, 256, 128))
    return pl.pallas_call(
        _ada_kernel,
        grid=(n // tn,),
        in_specs=[pl.BlockSpec((rows, d), lambda j: (0, 0)),
                  pl.BlockSpec((d, tn), lambda j: (0, j)),
                  pl.BlockSpec((1, tn), lambda j: (0, j))],
        out_specs=pl.BlockSpec((rows, tn), lambda j: (0, j)),
        out_shape=jax.ShapeDtypeStruct((rows, n), F32),
        compiler_params=_params(ARB),
        name="ada_mod",
    )(cc, ada_w, ada_b.reshape(1, n))


def _norm_kernel(x_ref, ctx_ref, nw_ref, mod_ref, o_ref, *, n_lat):
    i = pl.program_id(1)

    def norm_mod(xv, scale, shift):
        ms = jnp.mean(xv * xv, axis=-1, keepdims=True)
        gain = nw_ref[...] * (1.0 + scale)
        return (xv * lax.rsqrt(ms + EPS) * gain + shift).astype(o_ref.dtype)

    @pl.when(i < n_lat)
    def _():
        o_ref[...] = norm_mod(x_ref[...], mod_ref[0:1, :], mod_ref[1:2, :])

    @pl.when(i >= n_lat)
    def _():
        o_ref[...] = norm_mod(ctx_ref[...], mod_ref[2:3, :], mod_ref[3:4, :])


def _norm_call(x, ctx, norm_w, mods):
    b, l, d = x.shape
    lc = ctx.shape[1]
    tr = _pick(math.gcd(l, lc), (256, 128, 64, 32, 16))
    n_lat, n_ctx = l // tr, lc // tr
    return pl.pallas_call(
        functools.partial(_norm_kernel, n_lat=n_lat),
        grid=(b, n_lat + n_ctx),
        in_specs=[pl.BlockSpec((None, tr, d), lambda bb, i: (bb, jnp.minimum(i, n_lat - 1), 0)),
                  pl.BlockSpec((None, tr, d), lambda bb, i: (bb, jnp.maximum(i - n_lat, 0), 0)),
                  pl.BlockSpec((1, d), lambda bb, i: (0, 0)),
                  pl.BlockSpec((None, 4, d), lambda bb, i: (bb, 0, 0))],
        out_specs=pl.BlockSpec((None, tr, d), lambda bb, i: (bb, i, 0)),
        out_shape=jax.ShapeDtypeStruct((b, l + lc, d), BF16),
        compiler_params=_params(ARB, ARB),
        name="norm_mod",
    )(x, ctx, norm_w.reshape(1, d), mods)


def _rope(xh, cos, sin_lo, sin_hi):
    return xh * cos + pltpu.roll(xh, 96, 1) * sin_lo + pltpu.roll(xh, 32, 1) * sin_hi


def _inproj_plain_kernel(cb_ref, a_ref, w_ref, o_ref, *, out_w):
    del cb_ref
    acc = jnp.dot(a_ref[...], w_ref[...], preferred_element_type=F32)
    for c in range(acc.shape[1] // out_w):
        o_ref[c] = acc[:, c * out_w:(c + 1) * out_w].astype(o_ref.dtype)


def _inproj_rope_kernel(cb_ref, a_ref, w_ref, cos_ref, slo_ref, shi_ref, vec_ref, *rest, norm, out_w, cast_steps):
    del cb_ref
    if cast_steps:
        cast_in_ref, o_ref, cast_out_ref, acc_sc = rest

        @pl.when(pl.program_id(0) < cast_steps)
        def _():
            cast_out_ref[...] = cast_in_ref[...].astype(cast_out_ref.dtype)
    else:
        o_ref, acc_sc = rest

    @pl.when(pl.program_id(0) == 0)
    def _():
        acc_sc[...] = jnp.zeros_like(acc_sc)

    cos, slo, shi, vec = cos_ref[...], slo_ref[...], shi_ref[...], vec_ref[...]
    per = out_w // QK_DIM
    for t in range(acc_sc.shape[1] // QK_DIM):
        xh = acc_sc[:, t * QK_DIM:(t + 1) * QK_DIM]
        if norm:
            xh = xh * lax.rsqrt(jnp.mean(xh * xh, axis=-1, keepdims=True) + EPS) * vec
            r = _rope(xh, cos, slo, shi)
        else:
            r = _rope(xh, cos, slo, shi) * vec
        lane0 = (t % per) * QK_DIM
        o_ref[t // per, :, lane0:lane0 + QK_DIM] = r.astype(o_ref.dtype)
    acc_sc[...] = jnp.dot(a_ref[...], w_ref[...], preferred_element_type=F32)


def _cast_to_bf16(src, col0, ncols):
    return src[:, col0:col0 + ncols].astype(BF16)


def _inproj_call(a2d, w, tables, vecs, *, mode, col_blocks, out_w, rows_per_batch, cast=None):
    m, d = a2d.shape
    tn = INPROJ_TN
    tm = _pick(rows_per_batch, (768, 512, 384, 256, 128))
    n_i, n_j = m // tm, len(col_blocks)
    n_out = n_j * tn // out_w
    col_tab = jnp.asarray(col_blocks, jnp.int32)
    out_shape = jax.ShapeDtypeStruct((n_out, m, out_w), BF16)
    if mode == "plain":
        return pl.pallas_call(
            functools.partial(_inproj_plain_kernel, out_w=out_w),
            grid_spec=pltpu.PrefetchScalarGridSpec(
                num_scalar_prefetch=1,
                grid=(n_i, n_j),
                in_specs=[pl.BlockSpec((tm, d), lambda i, j, cb: (i, 0)),
                          pl.BlockSpec((d, tn), lambda i, j, cb: (0, cb[j]))],
                out_specs=pl.BlockSpec((tn // out_w, tm, out_w), lambda i, j, cb: (j, i, 0)),
            ),
            out_shape=out_shape,
            compiler_params=_params(ARB, ARB),
            name="inproj_plain",
        )(col_tab, a2d, w)

    steps = n_i * n_j
    t_blocks = rows_per_batch // tm
    j_per_vec = n_j // vecs.shape[0]
    cur = lambda s: jnp.minimum(s, steps - 1)
    prev = lambda s: jnp.maximum(s - 1, 0)
    tab_spec = pl.BlockSpec((tm, QK_DIM), lambda s, cb: ((prev(s) // n_j) % t_blocks, 0))
    in_specs = [pl.BlockSpec((tm, d), lambda s, cb: (cur(s) // n_j, 0)),
                pl.BlockSpec((d, tn), lambda s, cb: (0, cb[cur(s) % n_j])),
                tab_spec, tab_spec, tab_spec,
                pl.BlockSpec((None, 1, QK_DIM), lambda s, cb: ((prev(s) % n_j) // j_per_vec, 0, 0))]
    out_specs = [pl.BlockSpec((tn // out_w, tm, out_w), lambda s, cb: (prev(s) % n_j, prev(s) // n_j, 0))]
    out_shapes = [out_shape]
    args = [a2d, w, *tables, vecs]

    cast_src, cast_col0, cast_ncols = cast
    cast_rows = cast_src.shape[0]
    n_c = cast_ncols // CAST_COLS
    cast_steps = (cast_rows // CAST_ROWS) * n_c
    ride = (cast_ncols % CAST_COLS == 0 and cast_col0 % CAST_COLS == 0 and cast_rows % CAST_ROWS == 0
            and cast_steps <= steps)
    if ride:
        blk = lambda s: jnp.minimum(s, cast_steps - 1)
        c0 = cast_col0 // CAST_COLS
        in_specs.append(pl.BlockSpec((CAST_ROWS, CAST_COLS), lambda s, cb: (blk(s) // n_c, c0 + blk(s) % n_c)))
        out_specs.append(pl.BlockSpec((CAST_ROWS, CAST_COLS), lambda s, cb: (blk(s) // n_c, blk(s) % n_c)))
        out_shapes.append(jax.ShapeDtypeStruct((cast_rows, cast_ncols), BF16))
        args.append(cast_src)
    res = pl.pallas_call(
        functools.partial(_inproj_rope_kernel, norm=(mode == "normrope"), out_w=out_w,
                          cast_steps=cast_steps if ride else 0),
        grid_spec=pltpu.PrefetchScalarGridSpec(
            num_scalar_prefetch=1,
            grid=(steps + 1,),
            in_specs=in_specs,
            out_specs=out_specs,
            scratch_shapes=[pltpu.VMEM((tm, tn), F32)],
        ),
        out_shape=out_shapes,
        compiler_params=_params(ARB),
        name="inproj_" + mode,
    )(col_tab, *args)
    if ride:
        return res[0], res[1]
    return res[0], _cast_to_bf16(cast_src, cast_col0, cast_ncols)


def _attn_kernel(lamp_ref, q_ref, k_ref, v_ref, g_ref, sw_ref, o_ref, vt_sc, vt_tail_sc, s_sc, p_sc, acc_sc, *, tk):
    t_all = k_ref.shape[0]
    tq = q_ref.shape[0]
    n_big, tail = divmod(t_all, tk)
    first = 1 if tail else 0
    n = n_big + first
    size = lambda c: tail if c < first else tk
    c_exp = (QK_DIM ** -0.5) * math.log2(math.e)

    def transposed(rows):
        return v_ref[rows, :].astype(F32).T.astype(BF16)

    @pl.when(pl.program_id(2) == 0)
    def _():
        def transpose_chunk(c, carry):
            vt_sc[c] = transposed(pl.ds(pl.multiple_of(c * tk, tk), tk))
            return carry
        lax.fori_loop(0, n_big, transpose_chunk, 0)
        if tail:
            vt_tail_sc[...] = transposed(pl.ds(n_big * tk, tail))

    qt = q_ref[...].astype(F32).T.astype(BF16)
    qts = (qt[:QK_DIM], qt[QK_DIM:])
    acc_sc[...] = jnp.zeros_like(acc_sc)

    def stage_qk(c, rows):
        if rows != tk:
            start = n_big * tk
        else:
            start = (c - first) * tk if isinstance(c, int) else pl.multiple_of((c - first) * tk, tk)
        kc = k_ref[pl.ds(start, rows), :]
        cms = []
        for sub in range(2):
            s = jnp.dot(kc[:, sub * QK_DIM:(sub + 1) * QK_DIM], qts[sub],
                        preferred_element_type=F32)
            s_sc[sub, :rows] = s
            cms.append(jnp.max(s, axis=0, keepdims=True))
        return tuple(cms)

    def stage_softmax(rows, cms, ms, ls):
        new_m, new_l, alphas = [], [], []
        for sub in range(2):
            m_new = jnp.maximum(ms[sub], cms[sub])
            alpha = jnp.exp2((ms[sub] - m_new) * c_exp)
            p = jnp.exp2(s_sc[sub, :rows] * c_exp - m_new * c_exp)
            new_l.append(alpha * ls[sub] + jnp.sum(p.reshape(rows // 8, 8, tq), axis=0))
            p_sc[sub, :rows] = p.astype(BF16)
            new_m.append(m_new)
            alphas.append(alpha)
        return tuple(new_m), tuple(new_l), tuple(alphas)

    def stage_pv(c, rows, alphas):
        vt = vt_sc[c - first] if rows == tk else vt_tail_sc[...]
        for sub in range(2):
            acc_sc[sub] = alphas[sub] * acc_sc[sub] + jnp.dot(vt, p_sc[sub, :rows], preferred_element_type=F32)

    def tick(t, carry):
        cms, ms, ls, alphas = carry
        static = isinstance(t, int)
        if not static or 0 <= t - 2 < n:
            stage_pv(t - 2, size(t - 2) if static else tk, alphas)
        if not static or 0 <= t - 1 < n:
            ms, ls, alphas = stage_softmax(size(t - 1) if static else tk, cms, ms, ls)
        if not static or t < n:
            cms = stage_qk(t, size(t) if static else tk)
        return cms, ms, ls, alphas

    neg = jnp.full((1, tq), -jnp.inf, F32)
    zero = jnp.zeros((1, tq), F32)
    carry = ((neg, neg), (neg, neg), (jnp.zeros((8, tq), F32),) * 2, (zero, zero))
    loop_lo = first + 2
    loop_hi = max(loop_lo, n)
    for t in range(loop_lo):
        carry = tick(t, carry)
    carry = lax.fori_loop(loop_lo, loop_hi, tick, carry)
    for t in range(loop_hi, n + 2):
        carry = tick(t, carry)
    _, ms, ls, _ = carry

    lp = lamp_ref[...]
    lam = (jnp.exp(jnp.sum(lp[0:1] * lp[1:2], axis=1, keepdims=True))
           - jnp.exp(jnp.sum(lp[2:3] * lp[3:4], axis=1, keepdims=True)) + LAM_INIT)
    inv1 = 1.0 / jnp.sum(ls[0], axis=0, keepdims=True)
    inv2 = lam / jnp.sum(ls[1], axis=0, keepdims=True)
    o = (acc_sc[0] * inv1 - acc_sc[1] * inv2).T
    y = o * lax.rsqrt(jnp.mean(o * o, axis=-1, keepdims=True) + EPS) * sw_ref[...] * (1.0 - LAM_INIT)
    g = g_ref[...].astype(F32)
    o_ref[...] = (y * (g * jax.nn.sigmoid(g))).astype(o_ref.dtype)


def _attn_call(p_nr, p_plain, lam_params, subln_w, l):
    _, b, t_all, _ = p_nr.shape
    tq = _pick(l, (1024, 512, 256, 128))
    tk = ATTN_TK
    n_big, tail = divmod(t_all, tk)
    assert n_big >= 2 and tail % V7X_MXU_DIM == 0, "needs two full key chunks and a tail of whole MXU passes"
    h = DIFF_HEADS
    return pl.pallas_call(
        functools.partial(_attn_kernel, tk=tk),
        grid=(b, h, l // tq),
        in_specs=[pl.BlockSpec((4, QK_DIM), lambda bb, hh, i: (0, 0)),
                  pl.BlockSpec((None, None, tq, V_DIM), lambda bb, hh, i: (hh, bb, i, 0)),
                  pl.BlockSpec((None, None, t_all, V_DIM), lambda bb, hh, i: (h + hh, bb, 0, 0)),
                  pl.BlockSpec((None, None, t_all, V_DIM), lambda bb, hh, i: (hh, bb, 0, 0)),
                  pl.BlockSpec((None, None, tq, V_DIM), lambda bb, hh, i: (h + hh, bb, i, 0)),
                  pl.BlockSpec((1, V_DIM), lambda bb, hh, i: (0, 0))],
        out_specs=pl.BlockSpec((None, tq, V_DIM), lambda bb, hh, i: (bb, i, hh)),
        out_shape=jax.ShapeDtypeStruct((b, l, DIFF_WIDTH), BF16),
        scratch_shapes=[pltpu.VMEM((n_big, V_DIM, tk), BF16),
                        pltpu.VMEM((V_DIM, max(tail, V7X_MXU_DIM)), BF16),
                        pltpu.VMEM((2, tk, tq), F32),
                        pltpu.VMEM((2, tk, tq), BF16),
                        pltpu.VMEM((2, V_DIM, tq), F32)],
        compiler_params=_params(ARB, ARB, ARB),
        name="diff_attn",
    )(lam_params, p_nr, p_nr, p_plain, p_plain, subln_w.reshape(1, V_DIM))


def _ret_kernel(dec_ref, q_ref, k_ref, v_ref, g_ref, nw_ref, o_ref, of_sc, ob_sc, *, chunk):
    l = q_ref.shape[0]
    lc = k_ref.shape[0] - l
    nc = l // chunk
    cdim = (((1,), (1,)), ((), ()))
    rdim = (((0,), (0,)), ((), ()))

    dec = dec_ref[...]
    lg = jnp.log1p(-jnp.exp2(-dec))
    lg_f, lg_b = lg[0:1, 0:1], lg[1:2, 0:1]

    def col_iota(n):
        return lax.broadcasted_iota(jnp.int32, (n, 1), 0).astype(F32)

    ic = col_iota(chunk)
    xi_f = jnp.exp(lg_f * (ic + 1.0))
    xi_b = jnp.exp(lg_b * (chunk - ic))
    zeta_f = jnp.exp(lg_f * (chunk - 1.0 - ic))
    zeta_b = jnp.exp(lg_b * ic)
    gc_f = jnp.exp(lg_f * chunk)
    gc_b = jnp.exp(lg_b * chunk)
    ri = lax.broadcasted_iota(jnp.int32, (chunk, chunk), 0)
    ci = lax.broadcasted_iota(jnp.int32, (chunk, chunk), 1)
    dist = (ri - ci).astype(F32)
    dmat = jnp.where(ri >= ci, jnp.exp(lg_f * jnp.maximum(dist, 0.0)), jnp.exp(lg_b * jnp.maximum(-dist, 0.0)))

    im = col_iota(lc)
    kctx = k_ref[pl.ds(l, lc), :].astype(F32)
    vctx = v_ref[pl.ds(l, lc), :]
    r0_f = lax.dot_general((kctx * jnp.exp(lg_f * (lc - 1.0 - im))).astype(BF16), vctx, rdim,
                           preferred_element_type=F32)
    r0_b = lax.dot_general((kctx * jnp.exp(lg_b * im)).astype(BF16), vctx, rdim, preferred_element_type=F32)

    def chunk_kv(rows, zeta):
        kz = (k_ref[rows, :].astype(F32) * zeta).astype(BF16)
        return lax.dot_general(kz, v_ref[rows, :], rdim, preferred_element_type=F32)

    def scan_step(j, carry):
        s_f, s_b = carry
        rows_f = pl.ds(pl.multiple_of(j * chunk, chunk), chunk)
        qc = q_ref[rows_f, :]
        scores = lax.dot_general(qc, k_ref[rows_f, :], cdim, preferred_element_type=F32) * dmat
        o_f = (jnp.dot(scores.astype(BF16), v_ref[rows_f, :], preferred_element_type=F32)
               + jnp.dot(qc, s_f.astype(BF16), preferred_element_type=F32) * xi_f)
        s_f = gc_f * s_f + chunk_kv(rows_f, zeta_f)
        rows_b = pl.ds(pl.multiple_of((nc - 1 - j) * chunk, chunk), chunk)
        o_b = jnp.dot(q_ref[rows_b, :], s_b.astype(BF16), preferred_element_type=F32) * xi_b
        s_b = gc_b * s_b + chunk_kv(rows_b, zeta_b)
        return (s_f, s_b), (rows_f, o_f), (rows_b, o_b)

    def finish(rows, o):
        y = o * lax.rsqrt(jnp.mean(o * o, axis=-1, keepdims=True) + EPS) * nw_ref[...]
        g = g_ref[rows, :].astype(F32)
        o_ref[rows, :] = (y * (g * jax.nn.sigmoid(g))).astype(o_ref.dtype)

    half = nc // 2
    upper = lambda c: pl.ds(pl.multiple_of((c - half) * chunk, chunk), chunk)

    def first_half(j, carry):
        carry, (rows_f, o_f), (_, o_b) = scan_step(j, carry)
        of_sc[rows_f, :] = o_f
        ob_sc[upper(nc - 1 - j), :] = o_b
        return carry

    def second_half(j, carry):
        carry, (rows_f, o_f), (rows_b, o_b) = scan_step(j, carry)
        finish(rows_f, o_f + ob_sc[upper(j), :])
        finish(rows_b, o_b + of_sc[rows_b, :])
        return carry

    carry = lax.fori_loop(0, half, first_half, (r0_f, r0_b), unroll=RET_UNROLL)
    lax.fori_loop(half, nc, second_half, carry, unroll=RET_UNROLL)


def _ret_call(p_rope, p_plain, decs, ret_norm_w, l):
    _, b, t_all, _ = p_rope.shape
    h = RET_HEADS
    chunk = _pick(l, (256, 128))
    assert (l // chunk) % 2 == 0, "the two scan directions meet in the middle of an even chunk count"
    return pl.pallas_call(
        functools.partial(_ret_kernel, chunk=chunk),
        grid=(b, h),
        in_specs=[pl.BlockSpec((None, 2, QK_DIM), lambda bb, hh: (hh, 0, 0)),
                  pl.BlockSpec((None, None, l, QK_DIM), lambda bb, hh: (hh, bb, 0, 0)),
                  pl.BlockSpec((None, None, t_all, QK_DIM), lambda bb, hh: (h + hh, bb, 0, 0)),
                  pl.BlockSpec((None, None, t_all, V_DIM), lambda bb, hh: (2 * DIFF_HEADS + hh, bb, 0, 0)),
                  pl.BlockSpec((None, None, l, V_DIM), lambda bb, hh: (2 * DIFF_HEADS + h + hh, bb, 0, 0)),
                  pl.BlockSpec((1, V_DIM), lambda bb, hh: (0, 0))],
        out_specs=pl.BlockSpec((None, l, V_DIM), lambda bb, hh: (bb, 0, hh)),
        out_shape=jax.ShapeDtypeStruct((b, l, RET_WIDTH), BF16),
        scratch_shapes=[pltpu.VMEM((l // 2, V_DIM), F32),
                        pltpu.VMEM((l // 2, V_DIM), F32)],
        compiler_params=_params(ARB, ARB),
        name="retention",
    )(decs, p_rope, p_rope, p_plain, p_plain, ret_norm_w.reshape(1, V_DIM))


def _outproj_kernel(ad_ref, ar_ref, wd_ref, wr_ref, x_ref, gate_ref, o_ref):
    y = jnp.dot(ad_ref[...], wd_ref[...], preferred_element_type=F32)
    y = y + jnp.dot(ar_ref[...], wr_ref[...], preferred_element_type=F32)
    o_ref[...] = x_ref[...] + gate_ref[...] * y


def _outproj_call(o_d, o_r, w_out, x, gate):
    b, l, d = x.shape
    kd, kr = o_d.shape[-1], o_r.shape[-1]
    tm = _pick(l, (1024, 512, 256, 128))
    tn = _pick(d, (512, 256, 128))
    per_b = l // tm
    return pl.pallas_call(
        _outproj_kernel,
        grid=(b * per_b, d // tn),
        in_specs=[pl.BlockSpec((tm, kd), lambda i, j: (i, 0)),
                  pl.BlockSpec((tm, kr), lambda i, j: (i, 0)),
                  pl.BlockSpec((kd, tn), lambda i, j: (0, j)),
                  pl.BlockSpec((kr, tn), lambda i, j: (kd // kr, j)),
                  pl.BlockSpec((tm, tn), lambda i, j: (i, j)),
                  pl.BlockSpec((None, 1, tn), lambda i, j: (i // per_b, 0, j))],
        out_specs=pl.BlockSpec((tm, tn), lambda i, j: (i, j)),
        out_shape=jax.ShapeDtypeStruct((b * l, d), F32),
        compiler_params=_params(ARB, ARB),
        name="outproj",
    )(o_d.reshape(b * l, kd), o_r.reshape(b * l, kr), w_out, w_out, x.reshape(b * l, d),
      gate.reshape(b, 1, d)).reshape(b, l, d)


def _rope_tables(l, lc):
    n_rows = l // GRID_W
    half = QK_DIM // 2
    inv_freq = ROPE_BASE ** (-jnp.arange(0, half, 2, dtype=F32) / half)
    ang_r = jnp.arange(n_rows, dtype=F32)[:, None] * inv_freq
    ang_c = jnp.arange(GRID_W, dtype=F32)[:, None] * inv_freq

    def spread(fn):
        by_row = jnp.broadcast_to(fn(ang_r)[:, None, :], (n_rows, GRID_W, half // 2)).reshape(l, half // 2)
        by_col = jnp.broadcast_to(fn(ang_c)[None, :, :], (n_rows, GRID_W, half // 2)).reshape(l, half // 2)
        return jnp.concatenate([by_row, by_row, by_col, by_col], axis=-1)

    cos, sin = spread(jnp.cos), spread(jnp.sin)
    lo = (jnp.arange(QK_DIM) % half) < (half // 2)
    cos = jnp.concatenate([cos, jnp.ones((lc, QK_DIM), F32)], axis=0)
    sin_lo = jnp.concatenate([jnp.where(lo, -sin, 0.0), jnp.zeros((lc, QK_DIM), F32)], axis=0)
    sin_hi = jnp.concatenate([jnp.where(lo, 0.0, sin), jnp.zeros((lc, QK_DIM), F32)], axis=0)
    return cos, sin_lo, sin_hi


def _layer(x, ctx, c, c_ctx, norm_w, ada_w, ada_b, w_in, q_norm_w, k_norm_w, lam_params, subln_w,
           dec_f, dec_b, ret_norm_w, w_out):
    b, l, d = x.shape
    lc = ctx.shape[1]
    t_all = l + lc

    rows = -(-(b + 1) // 8) * 8
    cc = jnp.zeros((rows, d), F32).at[:b].set(c).at[b].set(c_ctx)
    mod = _ada_call(cc, ada_w, ada_b)
    shift, scale, gate = mod[:b, :d], mod[:b, d:2 * d], mod[:b, 2 * d:]
    shift_c = jnp.broadcast_to(mod[b, :d], (b, d))
    scale_c = jnp.broadcast_to(mod[b, d:2 * d], (b, d))
    mods = jnp.stack([scale, shift, scale_c, shift_c], axis=1)

    h_all = _norm_call(x, ctx, norm_w, mods).reshape(b * t_all, d)

    d_in = w_in.shape[1]
    w_qk = _cast_to_bf16(w_in, O_DQ, O_DV)
    tables = _rope_tables(l, lc)
    blocks = lambda lo, hi, base: list(range((lo - base) // INPROJ_TN, (hi - base) // INPROJ_TN))
    qk_vecs = jnp.stack([q_norm_w, k_norm_w]).reshape(2, 1, QK_DIM)
    ret_vecs = jnp.stack([jnp.ones((QK_DIM,), F32), jnp.full((QK_DIM,), QK_DIM ** -0.5, F32)]).reshape(2, 1, QK_DIM)
    call = functools.partial(_inproj_call, h_all, tables=tables, rows_per_batch=t_all)
    p_nr, w_rest = call(w=w_qk, vecs=qk_vecs, mode="normrope", col_blocks=blocks(O_DQ, O_DV, O_DQ), out_w=V_DIM,
                        cast=(w_in, O_DV, d_in - O_DV))
    p_rope, w_out_b = call(w=w_rest, vecs=ret_vecs, mode="rope", col_blocks=blocks(O_RQ, O_RV, O_DV), out_w=QK_DIM,
                           cast=(w_out, 0, w_out.shape[1]))
    p_plain = call(w=w_rest, vecs=None, mode="plain",
                   col_blocks=blocks(O_DV, O_RQ, O_DV) + blocks(O_RV, d_in, O_DV), out_w=V_DIM)
    p_nr = p_nr.reshape(-1, b, t_all, V_DIM)
    p_rope = p_rope.reshape(-1, b, t_all, QK_DIM)
    p_plain = p_plain.reshape(-1, b, t_all, V_DIM)

    o_d = _attn_call(p_nr, p_plain, lam_params, subln_w, l)
    decs = jnp.broadcast_to(jnp.stack([dec_f, dec_b], axis=1)[:, :, None], (RET_HEADS, 2, QK_DIM)).astype(F32)
    o_r = _ret_call(p_rope, p_plain, decs, ret_norm_w, l)

    return _outproj_call(o_d, o_r, w_out_b, x, gate)


def kernel(x, c, ctx, c_ctx, norm_w, ada_w, ada_b, w_in, diff_q_norm_w, diff_k_norm_w, diff_lambda_q1,
           diff_lambda_k1, diff_lambda_q2, diff_lambda_k2, diff_subln_w, ret_decay_fwd, ret_decay_bwd,
           ret_norm_w, w_out):
    depth = norm_w.shape[0]
    assert depth == 1, "LAM_INIT is the layer-0 value"
    for layer in range(depth):
        lam_params = jnp.stack([diff_lambda_q1[layer], diff_lambda_k1[layer],
                                diff_lambda_q2[layer], diff_lambda_k2[layer]]).astype(F32)
        x = _layer(x, ctx, c, c_ctx, norm_w[layer], ada_w[layer], ada_b[layer], w_in[layer],
                   diff_q_norm_w[layer], diff_k_norm_w[layer], lam_params, diff_subln_w[layer],
                   ret_decay_fwd[layer], ret_decay_bwd[layer], ret_norm_w[layer], w_out[layer])
    return x
```

```python
import functools
import math

import jax
import jax.numpy as jnp
from jax import lax
from jax.experimental import pallas as pl
from jax.experimental.pallas import tpu as pltpu

F32 = jnp.float32
BF16 = jnp.bfloat16

GRID_W = 64
DIFF_HEADS = 8
QK_DIM = 128
V_DIM = 256
RET_HEADS = 8
ROPE_BASE = 10000.0
EPS = 1e-6
LAM_INIT = 0.8 - 0.6 * math.exp(-0.3 * 0)

DIFF_QK_W = DIFF_HEADS * 2 * QK_DIM
DIFF_WIDTH = DIFF_HEADS * V_DIM
RET_QK_W = RET_HEADS * QK_DIM
RET_WIDTH = RET_HEADS * V_DIM
O_DQ = 0
O_DK = O_DQ + DIFF_QK_W
O_DV = O_DK + DIFF_QK_W
O_DG = O_DV + DIFF_WIDTH
O_RQ = O_DG + DIFF_WIDTH
O_RK = O_RQ + RET_QK_W
O_RV = O_RK + RET_QK_W
O_RG = O_RV + RET_WIDTH

V7X_VMEM_LIMIT_BYTES = 56 * 1024 * 1024
V7X_MXU_DIM = 256
ARB = "arbitrary"
RET_UNROLL = 8
INPROJ_TN = 1024
ATTN_TK = 2048
CAST_ROWS, CAST_COLS = 256, 2048


def _pick(n, candidates):
    for c in candidates:
        if n % c == 0:
            return c
    raise ValueError(f"no tile in {candidates} divides {n}")


def _params(*sem):
    return pltpu.CompilerParams(dimension_semantics=sem, vmem_limit_bytes=V7X_VMEM_LIMIT_BYTES)


def _ada_kernel(c_ref, w_ref, b_ref, o_ref):
    c = c_ref[...]
    a = (c * jax.nn.sigmoid(c)).astype(BF16)
    o_ref[...] = jnp.dot(a, w_ref[...].astype(BF16), preferred_element_type=F32) + b_ref[...]


def _ada_call(cc, ada_w, ada_b):
    rows, d = cc.shape
    n = ada_w.shape[1]
    tn = _pick(n, (512, 256, 128))
    return pl.pallas_call(
        _ada_kernel,
        grid=(n // tn,),
        in_specs=[pl.BlockSpec((rows, d), lambda j: (0, 0)),
                  pl.BlockSpec((d, tn), lambda j: (0, j)),
                  pl.BlockSpec((1, tn), lambda j: (0, j))],
        out_specs=pl.BlockSpec((rows, tn), lambda j: (0, j)),
        out_shape=jax.ShapeDtypeStruct((rows, n), F32),
        compiler_params=_params(ARB),
        name="ada_mod",
    )(cc, ada_w, ada_b.reshape(1, n))


def _norm_kernel(x_ref, ctx_ref, nw_ref, mod_ref, o_ref, *, n_lat):
    i = pl.program_id(1)

    def norm_mod(xv, scale, shift):
        ms = jnp.mean(xv * xv, axis=-1, keepdims=True)
        gain = nw_ref[...] * (1.0 + scale)
        return (xv * lax.rsqrt(ms + EPS) * gain + shift).astype(o_ref.dtype)

    @pl.when(i < n_lat)
    def _():
        o_ref[...] = norm_mod(x_ref[...], mod_ref[0:1, :], mod_ref[1:2, :])

    @pl.when(i >= n_lat)
    def _():
        o_ref[...] = norm_mod(ctx_ref[...], mod_ref[2:3, :], mod_ref[3:4, :])


def _norm_call(x, ctx, norm_w, mods):
    b, l, d = x.shape
    lc = ctx.shape[1]
    tr = _pick(math.gcd(l, lc), (256, 128, 64, 32, 16))
    n_lat, n_ctx = l // tr, lc // tr
    return pl.pallas_call(
        functools.partial(_norm_kernel, n_lat=n_lat),
        grid=(b, n_lat + n_ctx),
        in_specs=[pl.BlockSpec((None, tr, d), lambda bb, i: (bb, jnp.minimum(i, n_lat - 1), 0)),
                  pl.BlockSpec((None, tr, d), lambda bb, i: (bb, jnp.maximum(i - n_lat, 0), 0)),
                  pl.BlockSpec((1, d), lambda bb, i: (0, 0)),
                  pl.BlockSpec((None, 4, d), lambda bb, i: (bb, 0, 0))],
        out_specs=pl.BlockSpec((None, tr, d), lambda bb, i: (bb, i, 0)),
        out_shape=jax.ShapeDtypeStruct((b, l + lc, d), BF16),
        compiler_params=_params(ARB, ARB),
        name="norm_mod",
    )(x, ctx, norm_w.reshape(1, d), mods)


def _rope(xh, cos, sin_lo, sin_hi):
    return xh * cos + pltpu.roll(xh, 96, 1) * sin_lo + pltpu.roll(xh, 32, 1) * sin_hi


def _inproj_plain_kernel(cb_ref, a_ref, w_ref, o_ref, *, out_w):
    del cb_ref
    acc = jnp.dot(a_ref[...], w_ref[...], preferred_element_type=F32)
    for c in range(acc.shape[1] // out_w):
        o_ref[c] = acc[:, c * out_w:(c + 1) * out_w].astype(o_ref.dtype)


def _inproj_rope_kernel(cb_ref, a_ref, w_ref, cos_ref, slo_ref, shi_ref, vec_ref, *rest, norm, out_w, cast_steps):
    del cb_ref
    if cast_steps:
        cast_in_ref, o_ref, cast_out_ref, acc_sc = rest

        @pl.when(pl.program_id(0) < cast_steps)
        def _():
            cast_out_ref[...] = cast_in_ref[...].astype(cast_out_ref.dtype)
    else:
        o_ref, acc_sc = rest

    @pl.when(pl.program_id(0) == 0)
    def _():
        acc_sc[...] = jnp.zeros_like(acc_sc)

    cos, slo, shi, vec = cos_ref[...], slo_ref[...], shi_ref[...], vec_ref[...]
    per = out_w // QK_DIM
    for t in range(acc_sc.shape[1] // QK_DIM):
        xh = acc_sc[:, t * QK_DIM:(t + 1) * QK_DIM]
        if norm:
            xh = xh * lax.rsqrt(jnp.mean(xh * xh, axis=-1, keepdims=True) + EPS) * vec
            r = _rope(xh, cos, slo, shi)
        else:
            r = _rope(xh, cos, slo, shi) * vec
        lane0 = (t % per) * QK_DIM
        o_ref[t // per, :, lane0:lane0 + QK_DIM] = r.astype(o_ref.dtype)
    acc_sc[...] = jnp.dot(a_ref[...], w_ref[...], preferred_element_type=F32)


def _cast_to_bf16(src, col0, ncols):
    return src[:, col0:col0 + ncols].astype(BF16)


def _inproj_call(a2d, w, tables, vecs, *, mode, col_blocks, out_w, rows_per_batch, cast=None):
    m, d = a2d.shape
    tn = INPROJ_TN
    tm = _pick(rows_per_batch, (768, 512, 384, 256, 128))
    n_i, n_j = m // tm, len(col_blocks)
    n_out = n_j * tn // out_w
    col_tab = jnp.asarray(col_blocks, jnp.int32)
    out_shape = jax.ShapeDtypeStruct((n_out, m, out_w), BF16)
    if mode == "plain":
        return pl.pallas_call(
            functools.partial(_inproj_plain_kernel, out_w=out_w),
            grid_spec=pltpu.PrefetchScalarGridSpec(
                num_scalar_prefetch=1,
                grid=(n_i, n_j),
                in_specs=[pl.BlockSpec((tm, d), lambda i, j, cb: (i, 0)),
                          pl.BlockSpec((d, tn), lambda i, j, cb: (0, cb[j]))],
                out_specs=pl.BlockSpec((tn // out_w, tm, out_w), lambda i, j, cb: (j, i, 0)),
            ),
            out_shape=out_shape,
            compiler_params=_params(ARB, ARB),
            name="inproj_plain",
        )(col_tab, a2d, w)

    steps = n_i * n_j
    t_blocks = rows_per_batch // tm
    j_per_vec = n_j // vecs.shape[0]
    cur = lambda s: jnp.minimum(s, steps - 1)
    prev = lambda s: jnp.maximum(s - 1, 0)
    tab_spec = pl.BlockSpec((tm, QK_DIM), lambda s, cb: ((prev(s) // n_j) % t_blocks, 0))
    in_specs = [pl.BlockSpec((tm, d), lambda s, cb: (cur(s) // n_j, 0)),
                pl.BlockSpec((d, tn), lambda s, cb: (0, cb[cur(s) % n_j])),
                tab_spec, tab_spec, tab_spec,
                pl.BlockSpec((None, 1, QK_DIM), lambda s, cb: ((prev(s) % n_j) // j_per_vec, 0, 0))]
    out_specs = [pl.BlockSpec((tn // out_w, tm, out_w), lambda s, cb: (prev(s) % n_j, prev(s) // n_j, 0))]
    out_shapes = [out_shape]
    args = [a2d, w, *tables, vecs]

    cast_src, cast_col0, cast_ncols = cast
    cast_rows = cast_src.shape[0]
    n_c = cast_ncols // CAST_COLS
    cast_steps = (cast_rows // CAST_ROWS) * n_c
    ride = (cast_ncols % CAST_COLS == 0 and cast_col0 % CAST_COLS == 0 and cast_rows % CAST_ROWS == 0
            and cast_steps <= steps)
    if ride:
        blk = lambda s: jnp.minimum(s, cast_steps - 1)
        c0 = cast_col0 // CAST_COLS
        in_specs.append(pl.BlockSpec((CAST_ROWS, CAST_COLS), lambda s, cb: (blk(s) // n_c, c0 + blk(s) % n_c)))
        out_specs.append(pl.BlockSpec((CAST_ROWS, CAST_COLS), lambda s, cb: (blk(s) // n_c, blk(s) % n_c)))
        out_shapes.append(jax.ShapeDtypeStruct((cast_rows, cast_ncols), BF16))
        args.append(cast_src)
    res = pl.pallas_call(
        functools.partial(_inproj_rope_kernel, norm=(mode == "normrope"), out_w=out_w,
                          cast_steps=cast_steps if ride else 0),
        grid_spec=pltpu.PrefetchScalarGridSpec(
            num_scalar_prefetch=1,
            grid=(steps + 1,),
            in_specs=in_specs,
            out_specs=out_specs,
            scratch_shapes=[pltpu.VMEM((tm, tn), F32)],
        ),
        out_shape=out_shapes,
        compiler_params=_params(ARB),
        name="inproj_" + mode,
    )(col_tab, *args)
    if ride:
        return res[0], res[1]
    return res[0], _cast_to_bf16(cast_src, cast_col0, cast_ncols)


def _attn_kernel(lamp_ref, q_ref, k_ref, v_ref, g_ref, sw_ref, o_ref, vt_sc, vt_tail_sc, s_sc, p_sc, acc_sc, *, tk):
    t_all = k_ref.shape[0]
    tq = q_ref.shape[0]
    n_big, tail = divmod(t_all, tk)
    first = 1 if tail else 0
    n = n_big + first
    size = lambda c: tail if c < first else tk
    c_exp = (QK_DIM ** -0.5) * math.log2(math.e)

    def transposed(rows):
        return v_ref[rows, :].astype(F32).T.astype(BF16)

    @pl.when(pl.program_id(2) == 0)
    def _():
        def transpose_chunk(c, carry):
            vt_sc[c] = transposed(pl.ds(pl.multiple_of(c * tk, tk), tk))
            return carry
        lax.fori_loop(0, n_big, transpose_chunk, 0)
        if tail:
            vt_tail_sc[...] = transposed(pl.ds(n_big * tk, tail))

    qt = q_ref[...].astype(F32).T.astype(BF16)
    qts = (qt[:QK_DIM], qt[QK_DIM:])
    acc_sc[...] = jnp.zeros_like(acc_sc)

    def stage_qk(c, rows):
        if rows != tk:
            start = n_big * tk
        else:
            start = (c - first) * tk if isinstance(c, int) else pl.multiple_of((c - first) * tk, tk)
        kc = k_ref[pl.ds(start, rows), :]
        cms = []
        for sub in range(2):
            s = jnp.dot(kc[:, sub * QK_DIM:(sub + 1) * QK_DIM], qts[sub],
                        preferred_element_type=F32)
            s_sc[sub, :rows] = s
            cms.append(jnp.max(s, axis=0, keepdims=True))
        return tuple(cms)

    def stage_softmax(rows, cms, ms, ls):
        new_m, new_l, alphas = [], [], []
        for sub in range(2):
            m_new = jnp.maximum(ms[sub], cms[sub])
            alpha = jnp.exp2((ms[sub] - m_new) * c_exp)
            p = jnp.exp2(s_sc[sub, :rows] * c_exp - m_new * c_exp)
            new_l.append(alpha * ls[sub] + jnp.sum(p.reshape(rows // 8, 8, tq), axis=0))
            p_sc[sub, :rows] = p.astype(BF16)
            new_m.append(m_new)
            alphas.append(alpha)
        return tuple(new_m), tuple(new_l), tuple(alphas)

    def stage_pv(c, rows, alphas):
        vt = vt_sc[c - first] if rows == tk else vt_tail_sc[...]
        for sub in range(2):
            acc_sc[sub] = alphas[sub] * acc_sc[sub] + jnp.dot(vt, p_sc[sub, :rows], preferred_element_type=F32)

    def tick(t, carry):
        cms, ms, ls, alphas = carry
        static = isinstance(t, int)
        if not static or 0 <= t - 2 < n:
            stage_pv(t - 2, size(t - 2) if static else tk, alphas)
        if not static or 0 <= t - 1 < n:
            ms, ls, alphas = stage_softmax(size(t - 1) if static else tk, cms, ms, ls)
        if not static or t < n:
            cms = stage_qk(t, size(t) if static else tk)
        return cms, ms, ls, alphas

    neg = jnp.full((1, tq), -jnp.inf, F32)
    zero = jnp.zeros((1, tq), F32)
    carry = ((neg, neg), (neg, neg), (jnp.zeros((8, tq), F32),) * 2, (zero, zero))
    loop_lo = first + 2
    loop_hi = max(loop_lo, n)
    for t in range(loop_lo):
        carry = tick(t, carry)
    carry = lax.fori_loop(loop_lo, loop_hi, tick, carry)
    for t in range(loop_hi, n + 2):
        carry = tick(t, carry)
    _, ms, ls, _ = carry

    lp = lamp_ref[...]
    lam = (jnp.exp(jnp.sum(lp[0:1] * lp[1:2], axis=1, keepdims=True))
           - jnp.exp(jnp.sum(lp[2:3] * lp[3:4], axis=1, keepdims=True)) + LAM_INIT)
    inv1 = 1.0 / jnp.sum(ls[0], axis=0, keepdims=True)
    inv2 = lam / jnp.sum(ls[1], axis=0, keepdims=True)
    o = (acc_sc[0] * inv1 - acc_sc[1] * inv2).T
    y = o * lax.rsqrt(jnp.mean(o * o, axis=-1, keepdims=True) + EPS) * sw_ref[...] * (1.0 - LAM_INIT)
    g = g_ref[...].astype(F32)
    o_ref[...] = (y * (g * jax.nn.sigmoid(g))).astype(o_ref.dtype)


def _attn_call(p_nr, p_plain, lam_params, subln_w, l):
    _, b, t_all, _ = p_nr.shape
    tq = _pick(l, (1024, 512, 256, 128))
    tk = ATTN_TK
    n_big, tail = divmod(t_all, tk)
    assert n_big >= 2 and tail % V7X_MXU_DIM == 0, "needs two full key chunks and a tail of whole MXU passes"
    h = DIFF_HEADS
    return pl.pallas_call(
        functools.partial(_attn_kernel, tk=tk),
        grid=(b, h, l // tq),
        in_specs=[pl.BlockSpec((4, QK_DIM), lambda bb, hh, i: (0, 0)),
                  pl.BlockSpec((None, None, tq, V_DIM), lambda bb, hh, i: (hh, bb, i, 0)),
                  pl.BlockSpec((None, None, t_all, V_DIM), lambda bb, hh, i: (h + hh, bb, 0, 0)),
                  pl.BlockSpec((None, None, t_all, V_DIM), lambda bb, hh, i: (hh, bb, 0, 0)),
                  pl.BlockSpec((None, None, tq, V_DIM), lambda bb, hh, i: (h + hh, bb, i, 0)),
                  pl.BlockSpec((1, V_DIM), lambda bb, hh, i: (0, 0))],
        out_specs=pl.BlockSpec((None, tq, V_DIM), lambda bb, hh, i: (bb, i, hh)),
        out_shape=jax.ShapeDtypeStruct((b, l, DIFF_WIDTH), BF16),
        scratch_shapes=[pltpu.VMEM((n_big, V_DIM, tk), BF16),
                        pltpu.VMEM((V_DIM, max(tail, V7X_MXU_DIM)), BF16),
                        pltpu.VMEM((2, tk, tq), F32),
                        pltpu.VMEM((2, tk, tq), BF16),
                        pltpu.VMEM((2, V_DIM, tq), F32)],
        compiler_params=_params(ARB, ARB, ARB),
        name="diff_attn",
    )(lam_params, p_nr, p_nr, p_plain, p_plain, subln_w.reshape(1, V_DIM))


def _ret_kernel(dec_ref, q_ref, k_ref, v_ref, g_ref, nw_ref, o_ref, of_sc, ob_sc, *, chunk):
    l = q_ref.shape[0]
    lc = k_ref.shape[0] - l
    nc = l // chunk
    cdim = (((1,), (1,)), ((), ()))
    rdim = (((0,), (0,)), ((), ()))

    dec = dec_ref[...]
    lg = jnp.log1p(-jnp.exp2(-dec))
    lg_f, lg_b = lg[0:1, 0:1], lg[1:2, 0:1]

    def col_iota(n):
        return lax.broadcasted_iota(jnp.int32, (n, 1), 0).astype(F32)

    ic = col_iota(chunk)
    xi_f = jnp.exp(lg_f * (ic + 1.0))
    xi_b = jnp.exp(lg_b * (chunk - ic))
    zeta_f = jnp.exp(lg_f * (chunk - 1.0 - ic))
    zeta_b = jnp.exp(lg_b * ic)
    gc_f = jnp.exp(lg_f * chunk)
    gc_b = jnp.exp(lg_b * chunk)
    ri = lax.broadcasted_iota(jnp.int32, (chunk, chunk), 0)
    ci = lax.broadcasted_iota(jnp.int32, (chunk, chunk), 1)
    dist = (ri - ci).astype(F32)
    dmat = jnp.where(ri >= ci, jnp.exp(lg_f * jnp.maximum(dist, 0.0)), jnp.exp(lg_b * jnp.maximum(-dist, 0.0)))

    im = col_iota(lc)
    kctx = k_ref[pl.ds(l, lc), :].astype(F32)
    vctx = v_ref[pl.ds(l, lc), :]
    r0_f = lax.dot_general((kctx * jnp.exp(lg_f * (lc - 1.0 - im))).astype(BF16), vctx, rdim,
                           preferred_element_type=F32)
    r0_b = lax.dot_general((kctx * jnp.exp(lg_b * im)).astype(BF16), vctx, rdim, preferred_element_type=F32)

    def chunk_kv(rows, zeta):
        kz = (k_ref[rows, :].astype(F32) * zeta).astype(BF16)
        return lax.dot_general(kz, v_ref[rows, :], rdim, preferred_element_type=F32)

    def scan_step(j, carry):
        s_f, s_b = carry
        rows_f = pl.ds(pl.multiple_of(j * chunk, chunk), chunk)
        qc = q_ref[rows_f, :]
        scores = lax.dot_general(qc, k_ref[rows_f, :], cdim, preferred_element_type=F32) * dmat
        o_f = (jnp.dot(scores.astype(BF16), v_ref[rows_f, :], preferred_element_type=F32)
               + jnp.dot(qc, s_f.astype(BF16), preferred_element_type=F32) * xi_f)
        s_f = gc_f * s_f + chunk_kv(rows_f, zeta_f)
        rows_b = pl.ds(pl.multiple_of((nc - 1 - j) * chunk, chunk), chunk)
        o_b = jnp.dot(q_ref[rows_b, :], s_b.astype(BF16), preferred_element_type=F32) * xi_b
        s_b = gc_b * s_b + chunk_kv(rows_b, zeta_b)
        return (s_f, s_b), (rows_f, o_f), (rows_b, o_b)

    def finish(rows, o):
        y = o * lax.rsqrt(jnp.mean(o * o, axis=-1, keepdims=True) + EPS) * nw_ref[...]
        g = g_ref[rows, :].astype(F32)
        o_ref[rows, :] = (y * (g * jax.nn.sigmoid(g))).astype(o_ref.dtype)

    half = nc // 2
    upper = lambda c: pl.ds(pl.multiple_of((c - half) * chunk, chunk), chunk)

    def first_half(j, carry):
        carry, (rows_f, o_f), (_, o_b) = scan_step(j, carry)
        of_sc[rows_f, :] = o_f
        ob_sc[upper(nc - 1 - j), :] = o_b
        return carry

    def second_half(j, carry):
        carry, (rows_f, o_f), (rows_b, o_b) = scan_step(j, carry)
        finish(rows_f, o_f + ob_sc[upper(j), :])
        finish(rows_b, o_b + of_sc[rows_b, :])
        return carry

    carry = lax.fori_loop(0, half, first_half, (r0_f, r0_b), unroll=RET_UNROLL)
    lax.fori_loop(half, nc, second_half, carry, unroll=RET_UNROLL)


def _ret_call(p_rope, p_plain, decs, ret_norm_w, l):
    _, b, t_all, _ = p_rope.shape
    h = RET_HEADS
    chunk = _pick(l, (256, 128))
    assert (l // chunk) % 2 == 0, "the two scan directions meet in the middle of an even chunk count"
    return pl.pallas_call(
        functools.partial(_ret_kernel, chunk=chunk),
        grid=(b, h),
        in_specs=[pl.BlockSpec((None, 2, QK_DIM), lambda bb, hh: (hh, 0, 0)),
                  pl.BlockSpec((None, None, l, QK_DIM), lambda bb, hh: (hh, bb, 0, 0)),
                  pl.BlockSpec((None, None, t_all, QK_DIM), lambda bb, hh: (h + hh, bb, 0, 0)),
                  pl.BlockSpec((None, None, t_all, V_DIM), lambda bb, hh: (2 * DIFF_HEADS + hh, bb, 0, 0)),
                  pl.BlockSpec((None, None, l, V_DIM), lambda bb, hh: (2 * DIFF_HEADS + h + hh, bb, 0, 0)),
                  pl.BlockSpec((1, V_DIM), lambda bb, hh: (0, 0))],
        out_specs=pl.BlockSpec((None, l, V_DIM), lambda bb, hh: (bb, 0, hh)),
        out_shape=jax.ShapeDtypeStruct((b, l, RET_WIDTH), BF16),
        scratch_shapes=[pltpu.VMEM((l // 2, V_DIM), F32),
                        pltpu.VMEM((l // 2, V_DIM), F32)],
        compiler_params=_params(ARB, ARB),
        name="retention",
    )(decs, p_rope, p_rope, p_plain, p_plain, ret_norm_w.reshape(1, V_DIM))


def _outproj_kernel(ad_ref, ar_ref, wd_ref, wr_ref, x_ref, gate_ref, o_ref):
    y = jnp.dot(ad_ref[...], wd_ref[...], preferred_element_type=F32)
    y = y + jnp.dot(ar_ref[...], wr_ref[...], preferred_element_type=F32)
    o_ref[...] = x_ref[...] + gate_ref[...] * y


def _outproj_call(o_d, o_r, w_out, x, gate):
    b, l, d = x.shape
    kd, kr = o_d.shape[-1], o_r.shape[-1]
    tm = _pick(l, (1024, 512, 256, 128))
    tn = _pick(d, (1024, 512, 256, 128))
    per_b = l // tm
    return pl.pallas_call(
        _outproj_kernel,
        grid=(b * per_b, d // tn),
        in_specs=[pl.BlockSpec((tm, kd), lambda i, j: (i, 0)),
                  pl.BlockSpec((tm, kr), lambda i, j: (i, 0)),
                  pl.BlockSpec((kd, tn), lambda i, j: (0, j)),
                  pl.BlockSpec((kr, tn), lambda i, j: (kd // kr, j)),
                  pl.BlockSpec((tm, tn), lambda i, j: (i, j)),
                  pl.BlockSpec((None, 1, tn), lambda i, j: (i // per_b, 0, j))],
        out_specs=pl.BlockSpec((tm, tn), lambda i, j: (i, j)),
        out_shape=jax.ShapeDtypeStruct((b * l, d), F32),
        compiler_params=_params(ARB, ARB),
        name="outproj",
    )(o_d.reshape(b * l, kd), o_r.reshape(b * l, kr), w_out, w_out, x.reshape(b * l, d),
      gate.reshape(b, 1, d)).reshape(b, l, d)


def _rope_tables(l, lc):
    n_rows = l // GRID_W
    half = QK_DIM // 2
    inv_freq = ROPE_BASE ** (-jnp.arange(0, half, 2, dtype=F32) / half)
    ang_r = jnp.arange(n_rows, dtype=F32)[:, None] * inv_freq
    ang_c = jnp.arange(GRID_W, dtype=F32)[:, None] * inv_freq

    def spread(fn):
        by_row = jnp.broadcast_to(fn(ang_r)[:, None, :], (n_rows, GRID_W, half // 2)).reshape(l, half // 2)
        by_col = jnp.broadcast_to(fn(ang_c)[None, :, :], (n_rows, GRID_W, half // 2)).reshape(l, half // 2)
        return jnp.concatenate([by_row, by_row, by_col, by_col], axis=-1)

    cos, sin = spread(jnp.cos), spread(jnp.sin)
    lo = (jnp.arange(QK_DIM) % half) < (half // 2)
    cos = jnp.concatenate([cos, jnp.ones((lc, QK_DIM), F32)], axis=0)
    sin_lo = jnp.concatenate([jnp.where(lo, -sin, 0.0), jnp.zeros((lc, QK_DIM), F32)], axis=0)
    sin_hi = jnp.concatenate([jnp.where(lo, 0.0, sin), jnp.zeros((lc, QK_DIM), F32)], axis=0)
    return cos, sin_lo, sin_hi


def _layer(x, ctx, c, c_ctx, norm_w, ada_w, ada_b, w_in, q_norm_w, k_norm_w, lam_params, subln_w,
           dec_f, dec_b, ret_norm_w, w_out):
    b, l, d = x.shape
    lc = ctx.shape[1]
    t_all = l + lc

    rows = -(-(b + 1) // 8) * 8
    cc = jnp.zeros((rows, d), F32).at[:b].set(c).at[b].set(c_ctx)
    mod = _ada_call(cc, ada_w, ada_b)
    shift, scale, gate = mod[:b, :d], mod[:b, d:2 * d], mod[:b, 2 * d:]
    shift_c = jnp.broadcast_to(mod[b, :d], (b, d))
    scale_c = jnp.broadcast_to(mod[b, d:2 * d], (b, d))
    mods = jnp.stack([scale, shift, scale_c, shift_c], axis=1)

    h_all = _norm_call(x, ctx, norm_w, mods).reshape(b * t_all, d)

    d_in = w_in.shape[1]
    w_qk = _cast_to_bf16(w_in, O_DQ, O_DV)
    tables = _rope_tables(l, lc)
    blocks = lambda lo, hi, base: list(range((lo - base) // INPROJ_TN, (hi - base) // INPROJ_TN))
    qk_vecs = jnp.stack([q_norm_w, k_norm_w]).reshape(2, 1, QK_DIM)
    ret_vecs = jnp.stack([jnp.ones((QK_DIM,), F32), jnp.full((QK_DIM,), QK_DIM ** -0.5, F32)]).reshape(2, 1, QK_DIM)
    call = functools.partial(_inproj_call, h_all, tables=tables, rows_per_batch=t_all)
    p_nr, w_rest = call(w=w_qk, vecs=qk_vecs, mode="normrope", col_blocks=blocks(O_DQ, O_DV, O_DQ), out_w=V_DIM,
                        cast=(w_in, O_DV, d_in - O_DV))
    p_rope, w_out_b = call(w=w_rest, vecs=ret_vecs, mode="rope", col_blocks=blocks(O_RQ, O_RV, O_DV), out_w=QK_DIM,
                           cast=(w_out, 0, w_out.shape[1]))
    p_plain = call(w=w_rest, vecs=None, mode="plain",
                   col_blocks=blocks(O_DV, O_RQ, O_DV) + blocks(O_RV, d_in, O_DV), out_w=V_DIM)
    p_nr = p_nr.reshape(-1, b, t_all, V_DIM)
    p_rope = p_rope.reshape(-1, b, t_all, QK_DIM)
    p_plain = p_plain.reshape(-1, b, t_all, V_DIM)

    o_d = _attn_call(p_nr, p_plain, lam_params, subln_w, l)
    decs = jnp.broadcast_to(jnp.stack([dec_f, dec_b], axis=1)[:, :, None], (RET_HEADS, 2, QK_DIM)).astype(F32)
    o_r = _ret_call(p_rope, p_plain, decs, ret_norm_w, l)

    return _outproj_call(o_d, o_r, w_out_b, x, gate)


def kernel(x, c, ctx, c_ctx, norm_w, ada_w, ada_b, w_in, diff_q_norm_w, diff_k_norm_w, diff_lambda_q1,
           diff_lambda_k1, diff_lambda_q2, diff_lambda_k2, diff_subln_w, ret_decay_fwd, ret_decay_bwd,
           ret_norm_w, w_out):
    depth = norm_w.shape[0]
    assert depth == 1, "LAM_INIT is the layer-0 value"
    for layer in range(depth):
        lam_params = jnp.stack([diff_lambda_q1[layer], diff_lambda_k1[layer],
                                diff_lambda_q2[layer], diff_lambda_k2[layer]]).astype(F32)
        x = _layer(x, ctx, c, c_ctx, norm_w[layer], ada_w[layer], ada_b[layer], w_in[layer],
                   diff_q_norm_w[layer], diff_k_norm_w[layer], lam_params, diff_subln_w[layer],
                   ret_decay_fwd[layer], ret_decay_bwd[layer], ret_norm_w[layer], w_out[layer])
    return x
```

```python
import functools
import math

import jax
import jax.numpy as jnp
from jax import lax
from jax.experimental import pallas as pl
from jax.experimental.pallas import tpu as pltpu

F32 = jnp.float32
BF16 = jnp.bfloat16

GRID_W = 64
DIFF_HEADS = 8
QK_DIM = 128
V_DIM = 256
RET_HEADS = 8
ROPE_BASE = 10000.0
EPS = 1e-6
LAM_INIT = 0.8 - 0.6 * math.exp(-0.3 * 0)

DIFF_QK_W = DIFF_HEADS * 2 * QK_DIM
DIFF_WIDTH = DIFF_HEADS * V_DIM
RET_QK_W = RET_HEADS * QK_DIM
RET_WIDTH = RET_HEADS * V_DIM
O_DQ = 0
O_DK = O_DQ + DIFF_QK_W
O_DV = O_DK + DIFF_QK_W
O_DG = O_DV + DIFF_WIDTH
O_RQ = O_DG + DIFF_WIDTH
O_RK = O_RQ + RET_QK_W
O_RV = O_RK + RET_QK_W
O_RG = O_RV + RET_WIDTH

V7X_VMEM_LIMIT_BYTES = 56 * 1024 * 1024
V7X_MXU_DIM = 256
ARB = "arbitrary"
RET_UNROLL = 8
INPROJ_TN = 1024
ATTN_TK = 2048
NORM_GROUP = 3
CAST_ROWS, CAST_COLS = 256, 2048


def _pick(n, candidates):
    for c in candidates:
        if n % c == 0:
            return c
    raise ValueError(f"no tile in {candidates} divides {n}")


def _params(*sem):
    return pltpu.CompilerParams(dimension_semantics=sem, vmem_limit_bytes=V7X_VMEM_LIMIT_BYTES)


def _ada_kernel(c_ref, w_ref, b_ref, o_ref):
    c = c_ref[...]
    a = (c * jax.nn.sigmoid(c)).astype(BF16)
    o_ref[...] = jnp.dot(a, w_ref[...].astype(BF16), preferred_element_type=F32) + b_ref[...]


def _ada_call(cc, ada_w, ada_b):
    rows, d = cc.shape
    n = ada_w.shape[1]
    tn = _pick(n, (512, 256, 128))
    return pl.pallas_call(
        _ada_kernel,
        grid=(n // tn,),
        in_specs=[pl.BlockSpec((rows, d), lambda j: (0, 0)),
                  pl.BlockSpec((d, tn), lambda j: (0, j)),
                  pl.BlockSpec((1, tn), lambda j: (0, j))],
        out_specs=pl.BlockSpec((rows, tn), lambda j: (0, j)),
        out_shape=jax.ShapeDtypeStruct((rows, n), F32),
        compiler_params=_params(ARB),
        name="ada_mod",
    )(cc, ada_w, ada_b.reshape(1, n))


def _norm_kernel(x_ref, ctx_ref, nw_ref, mod_ref, o_ref, *, n_lat):
    i = pl.program_id(1)
    g = o_ref.shape[0]

    def norm_mod(xv, scale, shift):
        ms = jnp.mean(xv * xv, axis=-1, keepdims=True)
        gain = nw_ref[...] * (1.0 + scale)
        return (xv * lax.rsqrt(ms + EPS) * gain + shift).astype(o_ref.dtype)

    for k in range(g - 1):
        o_ref[k] = norm_mod(x_ref[k], mod_ref[0:1, :], mod_ref[1:2, :])

    @pl.when(i * g + g - 1 < n_lat)
    def _():
        o_ref[g - 1] = norm_mod(x_ref[g - 1], mod_ref[0:1, :], mod_ref[1:2, :])

    @pl.when(i * g + g - 1 >= n_lat)
    def _():
        o_ref[g - 1] = norm_mod(ctx_ref[...], mod_ref[2:3, :], mod_ref[3:4, :])


def _norm_call(x, ctx, norm_w, mods):
    b, l, d = x.shape
    lc = ctx.shape[1]
    tr = _pick(math.gcd(l, lc), (256, 128, 64, 32, 16))
    n_lat, n_ctx = l // tr, lc // tr
    n_tot = n_lat + n_ctx
    g = NORM_GROUP if (n_ctx == 1 and n_tot % NORM_GROUP == 0) else 1
    last_x = -(-n_lat // g) - 1
    return pl.pallas_call(
        functools.partial(_norm_kernel, n_lat=n_lat),
        grid=(b, n_tot // g),
        in_specs=[pl.BlockSpec((None, g, tr, d), lambda bb, i: (bb, jnp.minimum(i, last_x), 0, 0)),
                  pl.BlockSpec((None, tr, d), lambda bb, i: (bb, jnp.maximum(i * g + g - 1 - n_lat, 0), 0)),
                  pl.BlockSpec((1, d), lambda bb, i: (0, 0)),
                  pl.BlockSpec((None, 4, d), lambda bb, i: (bb, 0, 0))],
        out_specs=pl.BlockSpec((None, g, tr, d), lambda bb, i: (bb, i, 0, 0)),
        out_shape=jax.ShapeDtypeStruct((b, n_tot, tr, d), BF16),
        compiler_params=_params(ARB, ARB),
        name="norm_mod",
    )(x.reshape(b, n_lat, tr, d), ctx, norm_w.reshape(1, d), mods).reshape(b, l + lc, d)


def _rope(xh, cos, sin_lo, sin_hi):
    return xh * cos + pltpu.roll(xh, 96, 1) * sin_lo + pltpu.roll(xh, 32, 1) * sin_hi


def _inproj_plain_kernel(cb_ref, a_ref, w_ref, o_ref, *, out_w):
    del cb_ref
    acc = jnp.dot(a_ref[...], w_ref[...], preferred_element_type=F32)
    for c in range(acc.shape[1] // out_w):
        o_ref[c] = acc[:, c * out_w:(c + 1) * out_w].astype(o_ref.dtype)


def _inproj_rope_kernel(cb_ref, a_ref, w_ref, cos_ref, slo_ref, shi_ref, vec_ref, *rest, norm, out_w, cast_steps):
    del cb_ref
    if cast_steps:
        cast_in_ref, o_ref, cast_out_ref, acc_sc = rest

        @pl.when(pl.program_id(0) < cast_steps)
        def _():
            cast_out_ref[...] = cast_in_ref[...].astype(cast_out_ref.dtype)
    else:
        o_ref, acc_sc = rest

    @pl.when(pl.program_id(0) == 0)
    def _():
        acc_sc[...] = jnp.zeros_like(acc_sc)

    cos, slo, shi, vec = cos_ref[...], slo_ref[...], shi_ref[...], vec_ref[...]
    per = out_w // QK_DIM
    for t in range(acc_sc.shape[1] // QK_DIM):
        xh = acc_sc[:, t * QK_DIM:(t + 1) * QK_DIM]
        if norm:
            xh = xh * lax.rsqrt(jnp.mean(xh * xh, axis=-1, keepdims=True) + EPS) * vec
            r = _rope(xh, cos, slo, shi)
        else:
            r = _rope(xh, cos, slo, shi) * vec
        lane0 = (t % per) * QK_DIM
        o_ref[t // per, :, lane0:lane0 + QK_DIM] = r.astype(o_ref.dtype)
    acc_sc[...] = jnp.dot(a_ref[...], w_ref[...], preferred_element_type=F32)


def _cast_to_bf16(src, col0, ncols):
    return src[:, col0:col0 + ncols].astype(BF16)


def _inproj_call(a2d, w, tables, vecs, *, mode, col_blocks, out_w, rows_per_batch, cast=None):
    m, d = a2d.shape
    tn = INPROJ_TN
    tm = _pick(rows_per_batch, (768, 512, 384, 256, 128))
    n_i, n_j = m // tm, len(col_blocks)
    n_out = n_j * tn // out_w
    col_tab = jnp.asarray(col_blocks, jnp.int32)
    out_shape = jax.ShapeDtypeStruct((n_out, m, out_w), BF16)
    if mode == "plain":
        return pl.pallas_call(
            functools.partial(_inproj_plain_kernel, out_w=out_w),
            grid_spec=pltpu.PrefetchScalarGridSpec(
                num_scalar_prefetch=1,
                grid=(n_i, n_j),
                in_specs=[pl.BlockSpec((tm, d), lambda i, j, cb: (i, 0)),
                          pl.BlockSpec((d, tn), lambda i, j, cb: (0, cb[j]))],
                out_specs=pl.BlockSpec((tn // out_w, tm, out_w), lambda i, j, cb: (j, i, 0)),
            ),
            out_shape=out_shape,
            compiler_params=_params(ARB, ARB),
            name="inproj_plain",
        )(col_tab, a2d, w)

    steps = n_i * n_j
    t_blocks = rows_per_batch // tm
    j_per_vec = n_j // vecs.shape[0]
    cur = lambda s: jnp.minimum(s, steps - 1)
    prev = lambda s: jnp.maximum(s - 1, 0)
    tab_spec = pl.BlockSpec((tm, QK_DIM), lambda s, cb: ((prev(s) // n_j) % t_blocks, 0))
    in_specs = [pl.BlockSpec((tm, d), lambda s, cb: (cur(s) // n_j, 0)),
                pl.BlockSpec((d, tn), lambda s, cb: (0, cb[cur(s) % n_j])),
                tab_spec, tab_spec, tab_spec,
                pl.BlockSpec((None, 1, QK_DIM), lambda s, cb: ((prev(s) % n_j) // j_per_vec, 0, 0))]
    out_specs = [pl.BlockSpec((tn // out_w, tm, out_w), lambda s, cb: (prev(s) % n_j, prev(s) // n_j, 0))]
    out_shapes = [out_shape]
    args = [a2d, w, *tables, vecs]

    cast_src, cast_col0, cast_ncols = cast
    cast_rows = cast_src.shape[0]
    n_c = cast_ncols // CAST_COLS
    cast_steps = (cast_rows // CAST_ROWS) * n_c
    ride = (cast_ncols % CAST_COLS == 0 and cast_col0 % CAST_COLS == 0 and cast_rows % CAST_ROWS == 0
            and cast_steps <= steps)
    if ride:
        blk = lambda s: jnp.minimum(s, cast_steps - 1)
        c0 = cast_col0 // CAST_COLS
        in_specs.append(pl.BlockSpec((CAST_ROWS, CAST_COLS), lambda s, cb: (blk(s) // n_c, c0 + blk(s) % n_c)))
        out_specs.append(pl.BlockSpec((CAST_ROWS, CAST_COLS), lambda s, cb: (blk(s) // n_c, blk(s) % n_c)))
        out_shapes.append(jax.ShapeDtypeStruct((cast_rows, cast_ncols), BF16))
        args.append(cast_src)
    res = pl.pallas_call(
        functools.partial(_inproj_rope_kernel, norm=(mode == "normrope"), out_w=out_w,
                          cast_steps=cast_steps if ride else 0),
        grid_spec=pltpu.PrefetchScalarGridSpec(
            num_scalar_prefetch=1,
            grid=(steps + 1,),
            in_specs=in_specs,
            out_specs=out_specs,
            scratch_shapes=[pltpu.VMEM((tm, tn), F32)],
        ),
        out_shape=out_shapes,
        compiler_params=_params(ARB),
        name="inproj_" + mode,
    )(col_tab, *args)
    if ride:
        return res[0], res[1]
    return res[0], _cast_to_bf16(cast_src, cast_col0, cast_ncols)


def _attn_kernel(lamp_ref, q_ref, k_ref, v_ref, g_ref, sw_ref, o_ref, vt_sc, vt_tail_sc, s_sc, p_sc, acc_sc, *, tk):
    t_all = k_ref.shape[0]
    tq = q_ref.shape[0]
    n_big, tail = divmod(t_all, tk)
    first = 1 if tail else 0
    n = n_big + first
    size = lambda c: tail if c < first else tk
    c_exp = (QK_DIM ** -0.5) * math.log2(math.e)

    def transposed(rows):
        return v_ref[rows, :].astype(F32).T.astype(BF16)

    @pl.when(pl.program_id(2) == 0)
    def _():
        def transpose_chunk(c, carry):
            vt_sc[c] = transposed(pl.ds(pl.multiple_of(c * tk, tk), tk))
            return carry
        lax.fori_loop(0, n_big, transpose_chunk, 0)
        if tail:
            vt_tail_sc[...] = transposed(pl.ds(n_big * tk, tail))

    qt = q_ref[...].astype(F32).T.astype(BF16)
    qts = (qt[:QK_DIM], qt[QK_DIM:])
    acc_sc[...] = jnp.zeros_like(acc_sc)

    def stage_qk(c, rows):
        if rows != tk:
            start = n_big * tk
        else:
            start = (c - first) * tk if isinstance(c, int) else pl.multiple_of((c - first) * tk, tk)
        kc = k_ref[pl.ds(start, rows), :]
        cms = []
        for sub in range(2):
            s = jnp.dot(kc[:, sub * QK_DIM:(sub + 1) * QK_DIM], qts[sub],
                        preferred_element_type=F32)
            s_sc[sub, :rows] = s
            cms.append(jnp.max(s, axis=0, keepdims=True))
        return tuple(cms)

    def stage_softmax(rows, cms, ms, ls):
        new_m, new_l, alphas = [], [], []
        for sub in range(2):
            m_new = jnp.maximum(ms[sub], cms[sub])
            alpha = jnp.exp2((ms[sub] - m_new) * c_exp)
            p = jnp.exp2(s_sc[sub, :rows] * c_exp - m_new * c_exp)
            new_l.append(alpha * ls[sub] + jnp.sum(p.reshape(rows // 8, 8, tq), axis=0))
            p_sc[sub, :rows] = p.astype(BF16)
            new_m.append(m_new)
            alphas.append(alpha)
        return tuple(new_m), tuple(new_l), tuple(alphas)

    def stage_pv(c, rows, alphas):
        vt = vt_sc[c - first] if rows == tk else vt_tail_sc[...]
        for sub in range(2):
            acc_sc[sub] = alphas[sub] * acc_sc[sub] + jnp.dot(vt, p_sc[sub, :rows], preferred_element_type=F32)

    def tick(t, carry):
        cms, ms, ls, alphas = carry
        static = isinstance(t, int)
        if not static or 0 <= t - 2 < n:
            stage_pv(t - 2, size(t - 2) if static else tk, alphas)
        if not static or 0 <= t - 1 < n:
            ms, ls, alphas = stage_softmax(size(t - 1) if static else tk, cms, ms, ls)
        if not static or t < n:
            cms = stage_qk(t, size(t) if static else tk)
        return cms, ms, ls, alphas

    neg = jnp.full((1, tq), -jnp.inf, F32)
    zero = jnp.zeros((1, tq), F32)
    carry = ((neg, neg), (neg, neg), (jnp.zeros((8, tq), F32),) * 2, (zero, zero))
    loop_lo = first + 2
    loop_hi = max(loop_lo, n)
    for t in range(loop_lo):
        carry = tick(t, carry)
    carry = lax.fori_loop(loop_lo, loop_hi, tick, carry)
    for t in range(loop_hi, n + 2):
        carry = tick(t, carry)
    _, ms, ls, _ = carry

    lp = lamp_ref[...]
    lam = (jnp.exp(jnp.sum(lp[0:1] * lp[1:2], axis=1, keepdims=True))
           - jnp.exp(jnp.sum(lp[2:3] * lp[3:4], axis=1, keepdims=True)) + LAM_INIT)
    inv1 = 1.0 / jnp.sum(ls[0], axis=0, keepdims=True)
    inv2 = lam / jnp.sum(ls[1], axis=0, keepdims=True)
    o = (acc_sc[0] * inv1 - acc_sc[1] * inv2).T
    y = o * lax.rsqrt(jnp.mean(o * o, axis=-1, keepdims=True) + EPS) * sw_ref[...] * (1.0 - LAM_INIT)
    g = g_ref[...].astype(F32)
    o_ref[...] = (y * (g * jax.nn.sigmoid(g))).astype(o_ref.dtype)


def _attn_call(p_nr, p_plain, lam_params, subln_w, l):
    _, b, t_all, _ = p_nr.shape
    tq = _pick(l, (1024, 512, 256, 128))
    tk = ATTN_TK
    n_big, tail = divmod(t_all, tk)
    assert n_big >= 2 and tail % V7X_MXU_DIM == 0, "needs two full key chunks and a tail of whole MXU passes"
    h = DIFF_HEADS
    return pl.pallas_call(
        functools.partial(_attn_kernel, tk=tk),
        grid=(b, h, l // tq),
        in_specs=[pl.BlockSpec((4, QK_DIM), lambda bb, hh, i: (0, 0)),
                  pl.BlockSpec((None, None, tq, V_DIM), lambda bb, hh, i: (hh, bb, i, 0)),
                  pl.BlockSpec((None, None, t_all, V_DIM), lambda bb, hh, i: (h + hh, bb, 0, 0)),
                  pl.BlockSpec((None, None, t_all, V_DIM), lambda bb, hh, i: (hh, bb, 0, 0)),
                  pl.BlockSpec((None, None, tq, V_DIM), lambda bb, hh, i: (h + hh, bb, i, 0)),
                  pl.BlockSpec((1, V_DIM), lambda bb, hh, i: (0, 0))],
        out_specs=pl.BlockSpec((None, tq, V_DIM), lambda bb, hh, i: (bb, i, hh)),
        out_shape=jax.ShapeDtypeStruct((b, l, DIFF_WIDTH), BF16),
        scratch_shapes=[pltpu.VMEM((n_big, V_DIM, tk), BF16),
                        pltpu.VMEM((V_DIM, max(tail, V7X_MXU_DIM)), BF16),
                        pltpu.VMEM((2, tk, tq), F32),
                        pltpu.VMEM((2, tk, tq), BF16),
                        pltpu.VMEM((2, V_DIM, tq), F32)],
        compiler_params=_params(ARB, ARB, ARB),
        name="diff_attn",
    )(lam_params, p_nr, p_nr, p_plain, p_plain, subln_w.reshape(1, V_DIM))


def _ret_kernel(dec_ref, q_ref, k_ref, v_ref, g_ref, nw_ref, o_ref, of_sc, ob_sc, *, chunk):
    l = q_ref.shape[0]
    lc = k_ref.shape[0] - l
    nc = l // chunk
    cdim = (((1,), (1,)), ((), ()))
    rdim = (((0,), (0,)), ((), ()))

    dec = dec_ref[...]
    lg = jnp.log1p(-jnp.exp2(-dec))
    lg_f, lg_b = lg[0:1, 0:1], lg[1:2, 0:1]

    def col_iota(n):
        return lax.broadcasted_iota(jnp.int32, (n, 1), 0).astype(F32)

    ic = col_iota(chunk)
    xi_f = jnp.exp(lg_f * (ic + 1.0))
    xi_b = jnp.exp(lg_b * (chunk - ic))
    zeta_f = jnp.exp(lg_f * (chunk - 1.0 - ic))
    zeta_b = jnp.exp(lg_b * ic)
    gc_f = jnp.exp(lg_f * chunk)
    gc_b = jnp.exp(lg_b * chunk)
    ri = lax.broadcasted_iota(jnp.int32, (chunk, chunk), 0)
    ci = lax.broadcasted_iota(jnp.int32, (chunk, chunk), 1)
    dist = (ri - ci).astype(F32)
    dmat = jnp.where(ri >= ci, jnp.exp(lg_f * jnp.maximum(dist, 0.0)), jnp.exp(lg_b * jnp.maximum(-dist, 0.0)))

    im = col_iota(lc)
    kctx = k_ref[pl.ds(l, lc), :].astype(F32)
    vctx = v_ref[pl.ds(l, lc), :]
    r0_f = lax.dot_general((kctx * jnp.exp(lg_f * (lc - 1.0 - im))).astype(BF16), vctx, rdim,
                           preferred_element_type=F32)
    r0_b = lax.dot_general((kctx * jnp.exp(lg_b * im)).astype(BF16), vctx, rdim, preferred_element_type=F32)

    def chunk_kv(rows, zeta):
        kz = (k_ref[rows, :].astype(F32) * zeta).astype(BF16)
        return lax.dot_general(kz, v_ref[rows, :], rdim, preferred_element_type=F32)

    def scan_step(j, carry):
        s_f, s_b = carry
        rows_f = pl.ds(pl.multiple_of(j * chunk, chunk), chunk)
        qc = q_ref[rows_f, :]
        scores = lax.dot_general(qc, k_ref[rows_f, :], cdim, preferred_element_type=F32) * dmat
        o_f = (jnp.dot(scores.astype(BF16), v_ref[rows_f, :], preferred_element_type=F32)
               + jnp.dot(qc, s_f.astype(BF16), preferred_element_type=F32) * xi_f)
        s_f = gc_f * s_f + chunk_kv(rows_f, zeta_f)
        rows_b = pl.ds(pl.multiple_of((nc - 1 - j) * chunk, chunk), chunk)
        o_b = jnp.dot(q_ref[rows_b, :], s_b.astype(BF16), preferred_element_type=F32) * xi_b
        s_b = gc_b * s_b + chunk_kv(rows_b, zeta_b)
        return (s_f, s_b), (rows_f, o_f), (rows_b, o_b)

    def finish(rows, o):
        y = o * lax.rsqrt(jnp.mean(o * o, axis=-1, keepdims=True) + EPS) * nw_ref[...]
        g = g_ref[rows, :].astype(F32)
        o_ref[rows, :] = (y * (g * jax.nn.sigmoid(g))).astype(o_ref.dtype)

    half = nc // 2
    upper = lambda c: pl.ds(pl.multiple_of((c - half) * chunk, chunk), chunk)

    def first_half(j, carry):
        carry, (rows_f, o_f), (_, o_b) = scan_step(j, carry)
        of_sc[rows_f, :] = o_f
        ob_sc[upper(nc - 1 - j), :] = o_b
        return carry

    def second_half(j, carry):
        carry, (rows_f, o_f), (rows_b, o_b) = scan_step(j, carry)
        finish(rows_f, o_f + ob_sc[upper(j), :])
        finish(rows_b, o_b + of_sc[rows_b, :])
        return carry

    carry = lax.fori_loop(0, half, first_half, (r0_f, r0_b), unroll=RET_UNROLL)
    lax.fori_loop(half, nc, second_half, carry, unroll=RET_UNROLL)


def _ret_call(p_rope, p_plain, decs, ret_norm_w, l):
    _, b, t_all, _ = p_rope.shape
    h = RET_HEADS
    chunk = _pick(l, (256, 128))
    assert (l // chunk) % 2 == 0, "the two scan directions meet in the middle of an even chunk count"
    return pl.pallas_call(
        functools.partial(_ret_kernel, chunk=chunk),
        grid=(b, h),
        in_specs=[pl.BlockSpec((None, 2, QK_DIM), lambda bb, hh: (hh, 0, 0)),
                  pl.BlockSpec((None, None, l, QK_DIM), lambda bb, hh: (hh, bb, 0, 0)),
                  pl.BlockSpec((None, None, t_all, QK_DIM), lambda bb, hh: (h + hh, bb, 0, 0)),
                  pl.BlockSpec((None, None, t_all, V_DIM), lambda bb, hh: (2 * DIFF_HEADS + hh, bb, 0, 0)),
                  pl.BlockSpec((None, None, l, V_DIM), lambda bb, hh: (2 * DIFF_HEADS + h + hh, bb, 0, 0)),
                  pl.BlockSpec((1, V_DIM), lambda bb, hh: (0, 0))],
        out_specs=pl.BlockSpec((None, l, V_DIM), lambda bb, hh: (bb, 0, hh)),
        out_shape=jax.ShapeDtypeStruct((b, l, RET_WIDTH), BF16),
        scratch_shapes=[pltpu.VMEM((l // 2, V_DIM), F32),
                        pltpu.VMEM((l // 2, V_DIM), F32)],
        compiler_params=_params(ARB, ARB),
        name="retention",
    )(decs, p_rope, p_rope, p_plain, p_plain, ret_norm_w.reshape(1, V_DIM))


def _outproj_kernel(ad_ref, ar_ref, wd_ref, wr_ref, x_ref, gate_ref, o_ref):
    y = jnp.dot(ad_ref[...], wd_ref[...], preferred_element_type=F32)
    y = y + jnp.dot(ar_ref[...], wr_ref[...], preferred_element_type=F32)
    o_ref[...] = x_ref[...] + gate_ref[...] * y


def _outproj_call(o_d, o_r, w_out, x, gate):
    b, l, d = x.shape
    kd, kr = o_d.shape[-1], o_r.shape[-1]
    tm = _pick(l, (1024, 512, 256, 128))
    tn = _pick(d, (1024, 512, 256, 128))
    per_b = l // tm
    return pl.pallas_call(
        _outproj_kernel,
        grid=(b * per_b, d // tn),
        in_specs=[pl.BlockSpec((tm, kd), lambda i, j: (i, 0)),
                  pl.BlockSpec((tm, kr), lambda i, j: (i, 0)),
                  pl.BlockSpec((kd, tn), lambda i, j: (0, j)),
                  pl.BlockSpec((kr, tn), lambda i, j: (kd // kr, j)),
                  pl.BlockSpec((tm, tn), lambda i, j: (i, j)),
                  pl.BlockSpec((None, 1, tn), lambda i, j: (i // per_b, 0, j))],
        out_specs=pl.BlockSpec((tm, tn), lambda i, j: (i, j)),
        out_shape=jax.ShapeDtypeStruct((b * l, d), F32),
        compiler_params=_params(ARB, ARB),
        name="outproj",
    )(o_d.reshape(b * l, kd), o_r.reshape(b * l, kr), w_out, w_out, x.reshape(b * l, d),
      gate.reshape(b, 1, d)).reshape(b, l, d)


def _rope_tables(l, lc):
    n_rows = l // GRID_W
    half = QK_DIM // 2
    inv_freq = ROPE_BASE ** (-jnp.arange(0, half, 2, dtype=F32) / half)
    ang_r = jnp.arange(n_rows, dtype=F32)[:, None] * inv_freq
    ang_c = jnp.arange(GRID_W, dtype=F32)[:, None] * inv_freq

    def spread(fn):
        by_row = jnp.broadcast_to(fn(ang_r)[:, None, :], (n_rows, GRID_W, half // 2)).reshape(l, half // 2)
        by_col = jnp.broadcast_to(fn(ang_c)[None, :, :], (n_rows, GRID_W, half // 2)).reshape(l, half // 2)
        return jnp.concatenate([by_row, by_row, by_col, by_col], axis=-1)

    cos, sin = spread(jnp.cos), spread(jnp.sin)
    lo = (jnp.arange(QK_DIM) % half) < (half // 2)
    cos = jnp.concatenate([cos, jnp.ones((lc, QK_DIM), F32)], axis=0)
    sin_lo = jnp.concatenate([jnp.where(lo, -sin, 0.0), jnp.zeros((lc, QK_DIM), F32)], axis=0)
    sin_hi = jnp.concatenate([jnp.where(lo, 0.0, sin), jnp.zeros((lc, QK_DIM), F32)], axis=0)
    return cos, sin_lo, sin_hi


def _layer(x, ctx, c, c_ctx, norm_w, ada_w, ada_b, w_in, q_norm_w, k_norm_w, lam_params, subln_w,
           dec_f, dec_b, ret_norm_w, w_out):
    b, l, d = x.shape
    lc = ctx.shape[1]
    t_all = l + lc

    rows = -(-(b + 1) // 8) * 8
    cc = jnp.zeros((rows, d), F32).at[:b].set(c).at[b].set(c_ctx)
    mod = _ada_call(cc, ada_w, ada_b)
    shift, scale, gate = mod[:b, :d], mod[:b, d:2 * d], mod[:b, 2 * d:]
    shift_c = jnp.broadcast_to(mod[b, :d], (b, d))
    scale_c = jnp.broadcast_to(mod[b, d:2 * d], (b, d))
    mods = jnp.stack([scale, shift, scale_c, shift_c], axis=1)

    h_all = _norm_call(x, ctx, norm_w, mods).reshape(b * t_all, d)

    d_in = w_in.shape[1]
    w_qk = _cast_to_bf16(w_in, O_DQ, O_DV)
    tables = _rope_tables(l, lc)
    blocks = lambda lo, hi, base: list(range((lo - base) // INPROJ_TN, (hi - base) // INPROJ_TN))
    qk_vecs = jnp.stack([q_norm_w, k_norm_w]).reshape(2, 1, QK_DIM)
    ret_vecs = jnp.stack([jnp.ones((QK_DIM,), F32), jnp.full((QK_DIM,), QK_DIM ** -0.5, F32)]).reshape(2, 1, QK_DIM)
    call = functools.partial(_inproj_call, h_all, tables=tables, rows_per_batch=t_all)
    p_nr, w_rest = call(w=w_qk, vecs=qk_vecs, mode="normrope", col_blocks=blocks(O_DQ, O_DV, O_DQ), out_w=V_DIM,
                        cast=(w_in, O_DV, d_in - O_DV))
    p_rope, w_out_b = call(w=w_rest, vecs=ret_vecs, mode="rope", col_blocks=blocks(O_RQ, O_RV, O_DV), out_w=QK_DIM,
                           cast=(w_out, 0, w_out.shape[1]))
    p_plain = call(w=w_rest, vecs=None, mode="plain",
                   col_blocks=blocks(O_DV, O_RQ, O_DV) + blocks(O_RV, d_in, O_DV), out_w=V_DIM)
    p_nr = p_nr.reshape(-1, b, t_all, V_DIM)
    p_rope = p_rope.reshape(-1, b, t_all, QK_DIM)
    p_plain = p_plain.reshape(-1, b, t_all, V_DIM)

    o_d = _attn_call(p_nr, p_plain, lam_params, subln_w, l)
    decs = jnp.broadcast_to(jnp.stack([dec_f, dec_b], axis=1)[:, :, None], (RET_HEADS, 2, QK_DIM)).astype(F32)
    o_r = _ret_call(p_rope, p_plain, decs, ret_norm_w, l)

    return _outproj_call(o_d, o_r, w_out_b, x, gate)


def kernel(x, c, ctx, c_ctx, norm_w, ada_w, ada_b, w_in, diff_q_norm_w, diff_k_norm_w, diff_lambda_q1,
           diff_lambda_k1, diff_lambda_q2, diff_lambda_k2, diff_subln_w, ret_decay_fwd, ret_decay_bwd,
           ret_norm_w, w_out):
    depth = norm_w.shape[0]
    assert depth == 1, "LAM_INIT is the layer-0 value"
    for layer in range(depth):
        lam_params = jnp.stack([diff_lambda_q1[layer], diff_lambda_k1[layer],
                                diff_lambda_q2[layer], diff_lambda_k2[layer]]).astype(F32)
        x = _layer(x, ctx, c, c_ctx, norm_w[layer], ada_w[layer], ada_b[layer], w_in[layer],
                   diff_q_norm_w[layer], diff_k_norm_w[layer], lam_params, diff_subln_w[layer],
                   ret_decay_fwd[layer], ret_decay_bwd[layer], ret_norm_w[layer], w_out[layer])
    return x
```

```python
import functools
import math

import jax
import jax.numpy as jnp
from jax import lax
from jax.experimental import pallas as pl
from jax.experimental.pallas import tpu as pltpu

F32 = jnp.float32
BF16 = jnp.bfloat16

GRID_W = 64
DIFF_HEADS = 8
QK_DIM = 128
V_DIM = 256
RET_HEADS = 8
ROPE_BASE = 10000.0
EPS = 1e-6
LAM_INIT = 0.8 - 0.6 * math.exp(-0.3 * 0)

DIFF_QK_W = DIFF_HEADS * 2 * QK_DIM
DIFF_WIDTH = DIFF_HEADS * V_DIM
RET_QK_W = RET_HEADS * QK_DIM
RET_WIDTH = RET_HEADS * V_DIM
O_DQ = 0
O_DK = O_DQ + DIFF_QK_W
O_DV = O_DK + DIFF_QK_W
O_DG = O_DV + DIFF_WIDTH
O_RQ = O_DG + DIFF_WIDTH
O_RK = O_RQ + RET_QK_W
O_RV = O_RK + RET_QK_W
O_RG = O_RV + RET_WIDTH

V7X_VMEM_LIMIT_BYTES = 56 * 1024 * 1024
V7X_MXU_DIM = 256
ARB = "arbitrary"
RET_UNROLL = 8
INPROJ_TN = 1024
ATTN_TK = 2048
NORM_GROUP = 3
CAST_ROWS, CAST_COLS = 256, 2048


def _pick(n, candidates):
    for c in candidates:
        if n % c == 0:
            return c
    raise ValueError(f"no tile in {candidates} divides {n}")


def _params(*sem):
    return pltpu.CompilerParams(dimension_semantics=sem, vmem_limit_bytes=V7X_VMEM_LIMIT_BYTES)


def _ada_kernel(c_ref, w_ref, b_ref, o_ref):
    c = c_ref[...]
    a = (c * jax.nn.sigmoid(c)).astype(BF16)
    o_ref[...] = jnp.dot(a, w_ref[...].astype(BF16), preferred_element_type=F32) + b_ref[...]


def _ada_call(cc, ada_w, ada_b):
    rows, d = cc.shape
    n = ada_w.shape[1]
    tn = _pick(n, (1024, 512, 256, 128))
    return pl.pallas_call(
        _ada_kernel,
        grid=(n // tn,),
        in_specs=[pl.BlockSpec((rows, d), lambda j: (0, 0)),
                  pl.BlockSpec((d, tn), lambda j: (0, j)),
                  pl.BlockSpec((1, tn), lambda j: (0, j))],
        out_specs=pl.BlockSpec((rows, tn), lambda j: (0, j)),
        out_shape=jax.ShapeDtypeStruct((rows, n), F32),
        compiler_params=_params(ARB),
        name="ada_mod",
    )(cc, ada_w, ada_b.reshape(1, n))


def _norm_kernel(x_ref, ctx_ref, nw_ref, mod_ref, o_ref, *, n_lat):
    i = pl.program_id(1)
    g = o_ref.shape[0]

    def norm_mod(xv, scale, shift):
        ms = jnp.mean(xv * xv, axis=-1, keepdims=True)
        gain = nw_ref[...] * (1.0 + scale)
        return (xv * lax.rsqrt(ms + EPS) * gain + shift).astype(o_ref.dtype)

    for k in range(g - 1):
        o_ref[k] = norm_mod(x_ref[k], mod_ref[0:1, :], mod_ref[1:2, :])

    @pl.when(i * g + g - 1 < n_lat)
    def _():
        o_ref[g - 1] = norm_mod(x_ref[g - 1], mod_ref[0:1, :], mod_ref[1:2, :])

    @pl.when(i * g + g - 1 >= n_lat)
    def _():
        o_ref[g - 1] = norm_mod(ctx_ref[...], mod_ref[2:3, :], mod_ref[3:4, :])


def _norm_call(x, ctx, norm_w, mods):
    b, l, d = x.shape
    lc = ctx.shape[1]
    tr = _pick(math.gcd(l, lc), (256, 128, 64, 32, 16))
    n_lat, n_ctx = l // tr, lc // tr
    n_tot = n_lat + n_ctx
    g = NORM_GROUP if (n_ctx == 1 and n_tot % NORM_GROUP == 0) else 1
    last_x = -(-n_lat // g) - 1
    return pl.pallas_call(
        functools.partial(_norm_kernel, n_lat=n_lat),
        grid=(b, n_tot // g),
        in_specs=[pl.BlockSpec((None, g, tr, d), lambda bb, i: (bb, jnp.minimum(i, last_x), 0, 0)),
                  pl.BlockSpec((None, tr, d), lambda bb, i: (bb, jnp.maximum(i * g + g - 1 - n_lat, 0), 0)),
                  pl.BlockSpec((1, d), lambda bb, i: (0, 0)),
                  pl.BlockSpec((None, 4, d), lambda bb, i: (bb, 0, 0))],
        out_specs=pl.BlockSpec((None, g, tr, d), lambda bb, i: (bb, i, 0, 0)),
        out_shape=jax.ShapeDtypeStruct((b, n_tot, tr, d), BF16),
        compiler_params=_params(ARB, ARB),
        name="norm_mod",
    )(x.reshape(b, n_lat, tr, d), ctx, norm_w.reshape(1, d), mods).reshape(b, l + lc, d)


def _rope(xh, cos, sin_lo, sin_hi):
    return xh * cos + pltpu.roll(xh, 96, 1) * sin_lo + pltpu.roll(xh, 32, 1) * sin_hi


def _inproj_plain_kernel(cb_ref, a_ref, w_ref, o_ref, *, out_w):
    del cb_ref
    acc = jnp.dot(a_ref[...], w_ref[...], preferred_element_type=F32)
    for c in range(acc.shape[1] // out_w):
        o_ref[c] = acc[:, c * out_w:(c + 1) * out_w].astype(o_ref.dtype)


def _inproj_rope_kernel(cb_ref, a_ref, w_ref, cos_ref, slo_ref, shi_ref, vec_ref, *rest, norm, out_w, cast_steps):
    del cb_ref
    if cast_steps:
        cast_in_ref, o_ref, cast_out_ref, acc_sc = rest

        @pl.when(pl.program_id(0) < cast_steps)
        def _():
            cast_out_ref[...] = cast_in_ref[...].astype(cast_out_ref.dtype)
    else:
        o_ref, acc_sc = rest

    @pl.when(pl.program_id(0) == 0)
    def _():
        acc_sc[...] = jnp.zeros_like(acc_sc)

    cos, slo, shi, vec = cos_ref[...], slo_ref[...], shi_ref[...], vec_ref[...]
    per = out_w // QK_DIM
    for t in range(acc_sc.shape[1] // QK_DIM):
        xh = acc_sc[:, t * QK_DIM:(t + 1) * QK_DIM]
        if norm:
            xh = xh * lax.rsqrt(jnp.mean(xh * xh, axis=-1, keepdims=True) + EPS) * vec
            r = _rope(xh, cos, slo, shi)
        else:
            r = _rope(xh, cos, slo, shi) * vec
        lane0 = (t % per) * QK_DIM
        o_ref[t // per, :, lane0:lane0 + QK_DIM] = r.astype(o_ref.dtype)
    acc_sc[...] = jnp.dot(a_ref[...], w_ref[...], preferred_element_type=F32)


def _cast_to_bf16(src, col0, ncols):
    return src[:, col0:col0 + ncols].astype(BF16)


def _inproj_call(a2d, w, tables, vecs, *, mode, col_blocks, out_w, rows_per_batch, cast=None):
    m, d = a2d.shape
    tn = INPROJ_TN
    tm = _pick(rows_per_batch, (768, 512, 384, 256, 128))
    n_i, n_j = m // tm, len(col_blocks)
    n_out = n_j * tn // out_w
    col_tab = jnp.asarray(col_blocks, jnp.int32)
    out_shape = jax.ShapeDtypeStruct((n_out, m, out_w), BF16)
    if mode == "plain":
        return pl.pallas_call(
            functools.partial(_inproj_plain_kernel, out_w=out_w),
            grid_spec=pltpu.PrefetchScalarGridSpec(
                num_scalar_prefetch=1,
                grid=(n_i, n_j),
                in_specs=[pl.BlockSpec((tm, d), lambda i, j, cb: (i, 0)),
                          pl.BlockSpec((d, tn), lambda i, j, cb: (0, cb[j]))],
                out_specs=pl.BlockSpec((tn // out_w, tm, out_w), lambda i, j, cb: (j, i, 0)),
            ),
            out_shape=out_shape,
            compiler_params=_params(ARB, ARB),
            name="inproj_plain",
        )(col_tab, a2d, w)

    steps = n_i * n_j
    t_blocks = rows_per_batch // tm
    j_per_vec = n_j // vecs.shape[0]
    cur = lambda s: jnp.minimum(s, steps - 1)
    prev = lambda s: jnp.maximum(s - 1, 0)
    tab_spec = pl.BlockSpec((tm, QK_DIM), lambda s, cb: ((prev(s) // n_j) % t_blocks, 0))
    in_specs = [pl.BlockSpec((tm, d), lambda s, cb: (cur(s) // n_j, 0)),
                pl.BlockSpec((d, tn), lambda s, cb: (0, cb[cur(s) % n_j])),
                tab_spec, tab_spec, tab_spec,
                pl.BlockSpec((None, 1, QK_DIM), lambda s, cb: ((prev(s) % n_j) // j_per_vec, 0, 0))]
    out_specs = [pl.BlockSpec((tn // out_w, tm, out_w), lambda s, cb: (prev(s) % n_j, prev(s) // n_j, 0))]
    out_shapes = [out_shape]
    args = [a2d, w, *tables, vecs]

    cast_src, cast_col0, cast_ncols = cast
    cast_rows = cast_src.shape[0]
    n_c = cast_ncols // CAST_COLS
    cast_steps = (cast_rows // CAST_ROWS) * n_c
    ride = (cast_ncols % CAST_COLS == 0 and cast_col0 % CAST_COLS == 0 and cast_rows % CAST_ROWS == 0
            and cast_steps <= steps)
    if ride:
        blk = lambda s: jnp.minimum(s, cast_steps - 1)
        c0 = cast_col0 // CAST_COLS
        in_specs.append(pl.BlockSpec((CAST_ROWS, CAST_COLS), lambda s, cb: (blk(s) // n_c, c0 + blk(s) % n_c)))
        out_specs.append(pl.BlockSpec((CAST_ROWS, CAST_COLS), lambda s, cb: (blk(s) // n_c, blk(s) % n_c)))
        out_shapes.append(jax.ShapeDtypeStruct((cast_rows, cast_ncols), BF16))
        args.append(cast_src)
    res = pl.pallas_call(
        functools.partial(_inproj_rope_kernel, norm=(mode == "normrope"), out_w=out_w,
                          cast_steps=cast_steps if ride else 0),
        grid_spec=pltpu.PrefetchScalarGridSpec(
            num_scalar_prefetch=1,
            grid=(steps + 1,),
            in_specs=in_specs,
            out_specs=out_specs,
            scratch_shapes=[pltpu.VMEM((tm, tn), F32)],
        ),
        out_shape=out_shapes,
        compiler_params=_params(ARB),
        name="inproj_" + mode,
    )(col_tab, *args)
    if ride:
        return res[0], res[1]
    return res[0], _cast_to_bf16(cast_src, cast_col0, cast_ncols)


def _attn_kernel(lamp_ref, q_ref, k_ref, v_ref, g_ref, sw_ref, o_ref, vt_sc, vt_tail_sc, s_sc, p_sc, acc_sc, *, tk):
    t_all = k_ref.shape[0]
    tq = q_ref.shape[0]
    n_big, tail = divmod(t_all, tk)
    first = 1 if tail else 0
    n = n_big + first
    size = lambda c: tail if c < first else tk
    c_exp = (QK_DIM ** -0.5) * math.log2(math.e)

    def transposed(rows):
        return v_ref[rows, :].astype(F32).T.astype(BF16)

    @pl.when(pl.program_id(2) == 0)
    def _():
        def transpose_chunk(c, carry):
            vt_sc[c] = transposed(pl.ds(pl.multiple_of(c * tk, tk), tk))
            return carry
        lax.fori_loop(0, n_big, transpose_chunk, 0)
        if tail:
            vt_tail_sc[...] = transposed(pl.ds(n_big * tk, tail))

    qt = q_ref[...].astype(F32).T.astype(BF16)
    qts = (qt[:QK_DIM], qt[QK_DIM:])
    acc_sc[...] = jnp.zeros_like(acc_sc)

    def stage_qk(c, rows):
        if rows != tk:
            start = n_big * tk
        else:
            start = (c - first) * tk if isinstance(c, int) else pl.multiple_of((c - first) * tk, tk)
        kc = k_ref[pl.ds(start, rows), :]
        cms = []
        for sub in range(2):
            s = jnp.dot(kc[:, sub * QK_DIM:(sub + 1) * QK_DIM], qts[sub],
                        preferred_element_type=F32)
            s_sc[sub, :rows] = s
            cms.append(jnp.max(s, axis=0, keepdims=True))
        return tuple(cms)

    def stage_softmax(rows, cms, ms, ls):
        new_m, new_l, alphas = [], [], []
        for sub in range(2):
            m_new = jnp.maximum(ms[sub], cms[sub])
            alpha = jnp.exp2((ms[sub] - m_new) * c_exp)
            p = jnp.exp2(s_sc[sub, :rows] * c_exp - m_new * c_exp)
            new_l.append(alpha * ls[sub] + jnp.sum(p.reshape(rows // 8, 8, tq), axis=0))
            p_sc[sub, :rows] = p.astype(BF16)
            new_m.append(m_new)
            alphas.append(alpha)
        return tuple(new_m), tuple(new_l), tuple(alphas)

    def stage_pv(c, rows, alphas):
        vt = vt_sc[c - first] if rows == tk else vt_tail_sc[...]
        for sub in range(2):
            acc_sc[sub] = alphas[sub] * acc_sc[sub] + jnp.dot(vt, p_sc[sub, :rows], preferred_element_type=F32)

    def tick(t, carry):
        cms, ms, ls, alphas = carry
        static = isinstance(t, int)
        if not static or 0 <= t - 2 < n:
            stage_pv(t - 2, size(t - 2) if static else tk, alphas)
        if not static or 0 <= t - 1 < n:
            ms, ls, alphas = stage_softmax(size(t - 1) if static else tk, cms, ms, ls)
        if not static or t < n:
            cms = stage_qk(t, size(t) if static else tk)
        return cms, ms, ls, alphas

    neg = jnp.full((1, tq), -jnp.inf, F32)
    zero = jnp.zeros((1, tq), F32)
    carry = ((neg, neg), (neg, neg), (jnp.zeros((8, tq), F32),) * 2, (zero, zero))
    loop_lo = first + 2
    loop_hi = max(loop_lo, n)
    for t in range(loop_lo):
        carry = tick(t, carry)
    carry = lax.fori_loop(loop_lo, loop_hi, tick, carry)
    for t in range(loop_hi, n + 2):
        carry = tick(t, carry)
    _, ms, ls, _ = carry

    lp = lamp_ref[...]
    lam = (jnp.exp(jnp.sum(lp[0:1] * lp[1:2], axis=1, keepdims=True))
           - jnp.exp(jnp.sum(lp[2:3] * lp[3:4], axis=1, keepdims=True)) + LAM_INIT)
    inv1 = 1.0 / jnp.sum(ls[0], axis=0, keepdims=True)
    inv2 = lam / jnp.sum(ls[1], axis=0, keepdims=True)
    o = (acc_sc[0] * inv1 - acc_sc[1] * inv2).T
    y = o * lax.rsqrt(jnp.mean(o * o, axis=-1, keepdims=True) + EPS) * sw_ref[...] * (1.0 - LAM_INIT)
    g = g_ref[...].astype(F32)
    o_ref[...] = (y * (g * jax.nn.sigmoid(g))).astype(o_ref.dtype)


def _attn_call(p_nr, p_plain, lam_params, subln_w, l):
    _, b, t_all, _ = p_nr.shape
    tq = _pick(l, (1024, 512, 256, 128))
    tk = ATTN_TK
    n_big, tail = divmod(t_all, tk)
    assert n_big >= 2 and tail % V7X_MXU_DIM == 0, "needs two full key chunks and a tail of whole MXU passes"
    h = DIFF_HEADS
    return pl.pallas_call(
        functools.partial(_attn_kernel, tk=tk),
        grid=(b, h, l // tq),
        in_specs=[pl.BlockSpec((4, QK_DIM), lambda bb, hh, i: (0, 0)),
                  pl.BlockSpec((None, None, tq, V_DIM), lambda bb, hh, i: (hh, bb, i, 0)),
                  pl.BlockSpec((None, None, t_all, V_DIM), lambda bb, hh, i: (h + hh, bb, 0, 0)),
                  pl.BlockSpec((None, None, t_all, V_DIM), lambda bb, hh, i: (hh, bb, 0, 0)),
                  pl.BlockSpec((None, None, tq, V_DIM), lambda bb, hh, i: (h + hh, bb, i, 0)),
                  pl.BlockSpec((1, V_DIM), lambda bb, hh, i: (0, 0))],
        out_specs=pl.BlockSpec((None, tq, V_DIM), lambda bb, hh, i: (bb, i, hh)),
        out_shape=jax.ShapeDtypeStruct((b, l, DIFF_WIDTH), BF16),
        scratch_shapes=[pltpu.VMEM((n_big, V_DIM, tk), BF16),
                        pltpu.VMEM((V_DIM, max(tail, V7X_MXU_DIM)), BF16),
                        pltpu.VMEM((2, tk, tq), F32),
                        pltpu.VMEM((2, tk, tq), BF16),
                        pltpu.VMEM((2, V_DIM, tq), F32)],
        compiler_params=_params(ARB, ARB, ARB),
        name="diff_attn",
    )(lam_params, p_nr, p_nr, p_plain, p_plain, subln_w.reshape(1, V_DIM))


def _ret_kernel(dec_ref, q_ref, k_ref, v_ref, g_ref, nw_ref, o_ref, of_sc, ob_sc, *, chunk):
    l = q_ref.shape[0]
    lc = k_ref.shape[0] - l
    nc = l // chunk
    cdim = (((1,), (1,)), ((), ()))
    rdim = (((0,), (0,)), ((), ()))

    dec = dec_ref[...]
    lg = jnp.log1p(-jnp.exp2(-dec))
    lg_f, lg_b = lg[0:1, 0:1], lg[1:2, 0:1]

    def col_iota(n):
        return lax.broadcasted_iota(jnp.int32, (n, 1), 0).astype(F32)

    ic = col_iota(chunk)
    xi_f = jnp.exp(lg_f * (ic + 1.0))
    xi_b = jnp.exp(lg_b * (chunk - ic))
    zeta_f = jnp.exp(lg_f * (chunk - 1.0 - ic))
    zeta_b = jnp.exp(lg_b * ic)
    gc_f = jnp.exp(lg_f * chunk)
    gc_b = jnp.exp(lg_b * chunk)
    ri = lax.broadcasted_iota(jnp.int32, (chunk, chunk), 0)
    ci = lax.broadcasted_iota(jnp.int32, (chunk, chunk), 1)
    dist = (ri - ci).astype(F32)
    dmat = jnp.where(ri >= ci, jnp.exp(lg_f * jnp.maximum(dist, 0.0)), jnp.exp(lg_b * jnp.maximum(-dist, 0.0)))

    im = col_iota(lc)
    kctx = k_ref[pl.ds(l, lc), :].astype(F32)
    vctx = v_ref[pl.ds(l, lc), :]
    r0_f = lax.dot_general((kctx * jnp.exp(lg_f * (lc - 1.0 - im))).astype(BF16), vctx, rdim,
                           preferred_element_type=F32)
    r0_b = lax.dot_general((kctx * jnp.exp(lg_b * im)).astype(BF16), vctx, rdim, preferred_element_type=F32)

    def chunk_kv(rows, zeta):
        kz = (k_ref[rows, :].astype(F32) * zeta).astype(BF16)
        return lax.dot_general(kz, v_ref[rows, :], rdim, preferred_element_type=F32)

    def scan_step(j, carry):
        s_f, s_b = carry
        rows_f = pl.ds(pl.multiple_of(j * chunk, chunk), chunk)
        qc = q_ref[rows_f, :]
        scores = lax.dot_general(qc, k_ref[rows_f, :], cdim, preferred_element_type=F32) * dmat
        o_f = (jnp.dot(scores.astype(BF16), v_ref[rows_f, :], preferred_element_type=F32)
               + jnp.dot(qc, s_f.astype(BF16), preferred_element_type=F32) * xi_f)
        s_f = gc_f * s_f + chunk_kv(rows_f, zeta_f)
        rows_b = pl.ds(pl.multiple_of((nc - 1 - j) * chunk, chunk), chunk)
        o_b = jnp.dot(q_ref[rows_b, :], s_b.astype(BF16), preferred_element_type=F32) * xi_b
        s_b = gc_b * s_b + chunk_kv(rows_b, zeta_b)
        return (s_f, s_b), (rows_f, o_f), (rows_b, o_b)

    def finish(rows, o):
        y = o * lax.rsqrt(jnp.mean(o * o, axis=-1, keepdims=True) + EPS) * nw_ref[...]
        g = g_ref[rows, :].astype(F32)
        o_ref[rows, :] = (y * (g * jax.nn.sigmoid(g))).astype(o_ref.dtype)

    half = nc // 2
    upper = lambda c: pl.ds(pl.multiple_of((c - half) * chunk, chunk), chunk)

    def first_half(j, carry):
        carry, (rows_f, o_f), (_, o_b) = scan_step(j, carry)
        of_sc[rows_f, :] = o_f
        ob_sc[upper(nc - 1 - j), :] = o_b
        return carry

    def second_half(j, carry):
        carry, (rows_f, o_f), (rows_b, o_b) = scan_step(j, carry)
        finish(rows_f, o_f + ob_sc[upper(j), :])
        finish(rows_b, o_b + of_sc[rows_b, :])
        return carry

    carry = lax.fori_loop(0, half, first_half, (r0_f, r0_b), unroll=RET_UNROLL)
    lax.fori_loop(half, nc, second_half, carry, unroll=RET_UNROLL)


def _ret_call(p_rope, p_plain, decs, ret_norm_w, l):
    _, b, t_all, _ = p_rope.shape
    h = RET_HEADS
    chunk = _pick(l, (256, 128))
    assert (l // chunk) % 2 == 0, "the two scan directions meet in the middle of an even chunk count"
    return pl.pallas_call(
        functools.partial(_ret_kernel, chunk=chunk),
        grid=(b, h),
        in_specs=[pl.BlockSpec((None, 2, QK_DIM), lambda bb, hh: (hh, 0, 0)),
                  pl.BlockSpec((None, None, l, QK_DIM), lambda bb, hh: (hh, bb, 0, 0)),
                  pl.BlockSpec((None, None, t_all, QK_DIM), lambda bb, hh: (h + hh, bb, 0, 0)),
                  pl.BlockSpec((None, None, t_all, V_DIM), lambda bb, hh: (2 * DIFF_HEADS + hh, bb, 0, 0)),
                  pl.BlockSpec((None, None, l, V_DIM), lambda bb, hh: (2 * DIFF_HEADS + h + hh, bb, 0, 0)),
                  pl.BlockSpec((1, V_DIM), lambda bb, hh: (0, 0))],
        out_specs=pl.BlockSpec((None, l, V_DIM), lambda bb, hh: (bb, 0, hh)),
        out_shape=jax.ShapeDtypeStruct((b, l, RET_WIDTH), BF16),
        scratch_shapes=[pltpu.VMEM((l // 2, V_DIM), F32),
                        pltpu.VMEM((l // 2, V_DIM), F32)],
        compiler_params=_params(ARB, ARB),
        name="retention",
    )(decs, p_rope, p_rope, p_plain, p_plain, ret_norm_w.reshape(1, V_DIM))


def _outproj_kernel(ad_ref, ar_ref, wd_ref, wr_ref, x_ref, gate_ref, o_ref):
    y = jnp.dot(ad_ref[...], wd_ref[...], preferred_element_type=F32)
    y = y + jnp.dot(ar_ref[...], wr_ref[...], preferred_element_type=F32)
    o_ref[...] = x_ref[...] + gate_ref[...] * y


def _outproj_call(o_d, o_r, w_out, x, gate):
    b, l, d = x.shape
    kd, kr = o_d.shape[-1], o_r.shape[-1]
    tm = _pick(l, (1024, 512, 256, 128))
    tn = _pick(d, (1024, 512, 256, 128))
    per_b = l // tm
    return pl.pallas_call(
        _outproj_kernel,
        grid=(b * per_b, d // tn),
        in_specs=[pl.BlockSpec((tm, kd), lambda i, j: (i, 0)),
                  pl.BlockSpec((tm, kr), lambda i, j: (i, 0)),
                  pl.BlockSpec((kd, tn), lambda i, j: (0, j)),
                  pl.BlockSpec((kr, tn), lambda i, j: (kd // kr, j)),
                  pl.BlockSpec((tm, tn), lambda i, j: (i, j)),
                  pl.BlockSpec((None, 1, tn), lambda i, j: (i // per_b, 0, j))],
        out_specs=pl.BlockSpec((tm, tn), lambda i, j: (i, j)),
        out_shape=jax.ShapeDtypeStruct((b * l, d), F32),
        compiler_params=_params(ARB, ARB),
        name="outproj",
    )(o_d.reshape(b * l, kd), o_r.reshape(b * l, kr), w_out, w_out, x.reshape(b * l, d),
      gate.reshape(b, 1, d)).reshape(b, l, d)


def _rope_tables(l, lc):
    n_rows = l // GRID_W
    half = QK_DIM // 2
    inv_freq = ROPE_BASE ** (-jnp.arange(0, half, 2, dtype=F32) / half)
    ang_r = jnp.arange(n_rows, dtype=F32)[:, None] * inv_freq
    ang_c = jnp.arange(GRID_W, dtype=F32)[:, None] * inv_freq

    def spread(fn):
        by_row = jnp.broadcast_to(fn(ang_r)[:, None, :], (n_rows, GRID_W, half // 2)).reshape(l, half // 2)
        by_col = jnp.broadcast_to(fn(ang_c)[None, :, :], (n_rows, GRID_W, half // 2)).reshape(l, half // 2)
        return jnp.concatenate([by_row, by_row, by_col, by_col], axis=-1)

    cos, sin = spread(jnp.cos), spread(jnp.sin)
    lo = (jnp.arange(QK_DIM) % half) < (half // 2)
    cos = jnp.concatenate([cos, jnp.ones((lc, QK_DIM), F32)], axis=0)
    sin_lo = jnp.concatenate([jnp.where(lo, -sin, 0.0), jnp.zeros((lc, QK_DIM), F32)], axis=0)
    sin_hi = jnp.concatenate([jnp.where(lo, 0.0, sin), jnp.zeros((lc, QK_DIM), F32)], axis=0)
    return cos, sin_lo, sin_hi


def _layer(x, ctx, c, c_ctx, norm_w, ada_w, ada_b, w_in, q_norm_w, k_norm_w, lam_params, subln_w,
           dec_f, dec_b, ret_norm_w, w_out):
    b, l, d = x.shape
    lc = ctx.shape[1]
    t_all = l + lc

    rows = -(-(b + 1) // 8) * 8
    cc = jnp.zeros((rows, d), F32).at[:b].set(c).at[b].set(c_ctx)
    mod = _ada_call(cc, ada_w, ada_b)
    shift, scale, gate = mod[:b, :d], mod[:b, d:2 * d], mod[:b, 2 * d:]
    shift_c = jnp.broadcast_to(mod[b, :d], (b, d))
    scale_c = jnp.broadcast_to(mod[b, d:2 * d], (b, d))
    mods = jnp.stack([scale, shift, scale_c, shift_c], axis=1)

    h_all = _norm_call(x, ctx, norm_w, mods).reshape(b * t_all, d)

    d_in = w_in.shape[1]
    w_qk = _cast_to_bf16(w_in, O_DQ, O_DV)
    tables = _rope_tables(l, lc)
    blocks = lambda lo, hi, base: list(range((lo - base) // INPROJ_TN, (hi - base) // INPROJ_TN))
    qk_vecs = jnp.stack([q_norm_w, k_norm_w]).reshape(2, 1, QK_DIM)
    ret_vecs = jnp.stack([jnp.ones((QK_DIM,), F32), jnp.full((QK_DIM,), QK_DIM ** -0.5, F32)]).reshape(2, 1, QK_DIM)
    call = functools.partial(_inproj_call, h_all, tables=tables, rows_per_batch=t_all)
    p_nr, w_rest = call(w=w_qk, vecs=qk_vecs, mode="normrope", col_blocks=blocks(O_DQ, O_DV, O_DQ), out_w=V_DIM,
                        cast=(w_in, O_DV, d_in - O_DV))
    p_rope, w_out_b = call(w=w_rest, vecs=ret_vecs, mode="rope", col_blocks=blocks(O_RQ, O_RV, O_DV), out_w=QK_DIM,
                           cast=(w_out, 0, w_out.shape[1]))
    p_plain = call(w=w_rest, vecs=None, mode="plain",
                   col_blocks=blocks(O_DV, O_RQ, O_DV) + blocks(O_RV, d_in, O_DV), out_w=V_DIM)
    p_nr = p_nr.reshape(-1, b, t_all, V_DIM)
    p_rope = p_rope.reshape(-1, b, t_all, QK_DIM)
    p_plain = p_plain.reshape(-1, b, t_all, V_DIM)

    o_d = _attn_call(p_nr, p_plain, lam_params, subln_w, l)
    decs = jnp.broadcast_to(jnp.stack([dec_f, dec_b], axis=1)[:, :, None], (RET_HEADS, 2, QK_DIM)).astype(F32)
    o_r = _ret_call(p_rope, p_plain, decs, ret_norm_w, l)

    return _outproj_call(o_d, o_r, w_out_b, x, gate)


def kernel(x, c, ctx, c_ctx, norm_w, ada_w, ada_b, w_in, diff_q_norm_w, diff_k_norm_w, diff_lambda_q1,
           diff_lambda_k1, diff_lambda_q2, diff_lambda_k2, diff_subln_w, ret_decay_fwd, ret_decay_bwd,
           ret_norm_w, w_out):
    depth = norm_w.shape[0]
    assert depth == 1, "LAM_INIT is the layer-0 value"
    for layer in range(depth):
        lam_params = jnp.stack([diff_lambda_q1[layer], diff_lambda_k1[layer],
                                diff_lambda_q2[layer], diff_lambda_k2[layer]]).astype(F32)
        x = _layer(x, ctx, c, c_ctx, norm_w[layer], ada_w[layer], ada_b[layer], w_in[layer],
                   diff_q_norm_w[layer], diff_k_norm_w[layer], lam_params, diff_subln_w[layer],
                   ret_decay_fwd[layer], ret_decay_bwd[layer], ret_norm_w[layer], w_out[layer])
    return x
```
